```python
import jax, jax.numpy as jnp
from jax import lax
import numpy as np

D_MODEL = 1024
BATCH = 8
SEQ = 4096
DEPTH = 1

CHUNK = 128
GM_WIDTH = D_MODEL
GM_GROUPS = 8
GM_GROUP_DIM = GM_WIDTH // GM_GROUPS
N_Q_HEADS = 16
N_KV_HEADS = 4
HEAD_DIM = 64
Q_REP = N_Q_HEADS // N_KV_HEADS
Q_WIDTH = N_Q_HEADS * HEAD_DIM
KV_WIDTH = N_KV_HEADS * HEAD_DIM
WINDOW = 128
BLOCK = WINDOW
N_BRANCH = 2
D_FF = 4 * D_MODEL
IN_WIDTH = 2 * GM_WIDTH + Q_WIDTH + 2 * KV_WIDTH + N_BRANCH * D_MODEL
N_MOD = 6
EPS = 1e-6

kernel_name = "hybrid_gmlp_swa_sink_adaln_block"


def rmsnorm(x, g):
    xf = x.astype(jnp.float32)
    y = xf * lax.rsqrt(jnp.mean(xf * xf, axis=-1, keepdims=True) + EPS)
    return (y * g.astype(jnp.float32)).astype(x.dtype)


def layernorm(x, g, b):
    xf = x.astype(jnp.float32)
    mu = jnp.mean(xf, axis=-1, keepdims=True)
    xc = xf - mu
    var = jnp.mean(xc * xc, axis=-1, keepdims=True)
    y = xc * lax.rsqrt(var + EPS) * g.astype(jnp.float32) + b.astype(jnp.float32)
    return y.astype(x.dtype)


def gmlp_spatial_gating(u, v, ln_g, ln_b, ws, bs):
    B, S, _ = v.shape
    n_chunks = S // CHUNK
    u = jax.nn.gelu(u)
    v = layernorm(jax.nn.gelu(v), ln_g, ln_b)
    vc = v.reshape(B, n_chunks, CHUNK, GM_GROUPS, GM_GROUP_DIM)
    causal = jnp.tril(jnp.ones((CHUNK, CHUNK), dtype=bool))
    ws_c = jnp.where(causal[None], ws, jnp.zeros_like(ws))
    s = jnp.einsum("gts,bnsgc->bntgc", ws_c, vc) + bs.T[None, None, :, :, None]
    return u * s.reshape(B, S, GM_WIDTH)


def sliding_window_sink_attention(q, k, v, sinks):
    B, S, _ = q.shape
    nb = S // BLOCK
    qb = q.reshape(B, nb, BLOCK, N_KV_HEADS, Q_REP, HEAD_DIM)
    kc = k.reshape(B, nb, BLOCK, N_KV_HEADS, HEAD_DIM)
    vc = v.reshape(B, nb, BLOCK, N_KV_HEADS, HEAD_DIM)
    shift = lambda t: jnp.concatenate([jnp.zeros_like(t[:, :1]), t[:, :-1]], axis=1)
    kb = jnp.concatenate([shift(kc), kc], axis=2)
    vb = jnp.concatenate([shift(vc), vc], axis=2)
    scores = jnp.einsum("bnqgrd,bnkgd->bngrqk", qb, kb).astype(jnp.float32) * (HEAD_DIM ** -0.5)
    qpos = jnp.arange(BLOCK)[:, None] + BLOCK
    kpos = jnp.arange(2 * BLOCK)[None, :]
    dist = qpos - kpos
    band = (dist >= 0) & (dist < WINDOW)
    abs_k = jnp.arange(nb)[:, None, None] * BLOCK + kpos[None] - BLOCK
    valid = band[None] & (abs_k >= 0)
    scores = jnp.where(valid[None, :, None, None], scores, -jnp.inf)
    sink = sinks.astype(jnp.float32).reshape(N_KV_HEADS, Q_REP)[None, None, :, :, None, None]
    m = jnp.maximum(jnp.max(scores, axis=-1, keepdims=True), sink)
    p = jnp.exp(scores - m)
    probs = p / (jnp.sum(p, axis=-1, keepdims=True) + jnp.exp(sink - m))
    out = jnp.einsum("bngrqk,bnkgd->bnqgrd", probs.astype(vb.dtype), vb)
    return out.reshape(B, S, Q_WIDTH)


def setup_inputs(seed: int = 0) -> dict:
    key = jax.random.key(seed)
    ks = jax.random.split(key, 20)
    f32 = jnp.float32
    nrm = lambda k, shape, s: jax.random.normal(k, shape, f32) * s
    D = D_MODEL
    return {
        "x": nrm(ks[0], (BATCH, SEQ, D), 1.0),
        "c": nrm(ks[1], (BATCH, D), 1.0),
        "w_ada": nrm(ks[2], (DEPTH, D, N_MOD * D), 0.5 * D ** -0.5),
        "b_ada": nrm(ks[3], (DEPTH, N_MOD * D), 0.02),
        "g_norm_mix": 1.0 + nrm(ks[4], (DEPTH, D), 0.02),
        "w_in": nrm(ks[5], (DEPTH, D, IN_WIDTH), D ** -0.5),
        "b_in": nrm(ks[6], (DEPTH, IN_WIDTH), 0.02),
        "gm_ln_g": 1.0 + nrm(ks[7], (DEPTH, GM_WIDTH), 0.02),
        "gm_ln_b": nrm(ks[8], (DEPTH, GM_WIDTH), 0.02),
        "gm_ws": nrm(ks[9], (DEPTH, GM_GROUPS, CHUNK, CHUNK), CHUNK ** -0.5),
        "gm_bs": 1.0 + nrm(ks[10], (DEPTH, GM_GROUPS, CHUNK), 0.02),
        "attn_sinks": nrm(ks[11], (DEPTH, N_Q_HEADS), 0.5),
        "w_branch_a": nrm(ks[12], (DEPTH, GM_WIDTH, D), GM_WIDTH ** -0.5),
        "w_branch_b": nrm(ks[13], (DEPTH, Q_WIDTH, D), Q_WIDTH ** -0.5),
        "w_out": nrm(ks[14], (DEPTH, D, D), D ** -0.5),
        "g_norm_mlp": 1.0 + nrm(ks[15], (DEPTH, D), 0.02),
        "w_up": nrm(ks[16], (DEPTH, D, D_FF), D ** -0.5),
        "w_down": nrm(ks[17], (DEPTH, D_FF, D), D_FF ** -0.5),
        "g_final": 1.0 + nrm(ks[18], (D,), 0.02),
    }


def reference(x, c, w_ada, b_ada, g_norm_mix, w_in, b_in, gm_ln_g, gm_ln_b, gm_ws, gm_bs,
              attn_sinks, w_branch_a, w_branch_b, w_out, g_norm_mlp, w_up, w_down, g_final):
    D = D_MODEL
    o_u, o_v = 0, GM_WIDTH
    o_q = 2 * GM_WIDTH
    o_k = o_q + Q_WIDTH
    o_va = o_k + KV_WIDTH
    o_ga = o_va + KV_WIDTH
    o_gb = o_ga + D
    for l in range(DEPTH):
        mod = jax.nn.silu(c) @ w_ada[l] + b_ada[l]
        sh_mix, sc_mix, gt_mix, sh_mlp, sc_mlp, gt_mlp = [m[:, None, :] for m in jnp.split(mod, N_MOD, axis=-1)]

        h = rmsnorm(x, g_norm_mix[l]) * (1.0 + sc_mix) + sh_mix
        z = h @ w_in[l] + b_in[l]
        ya = gmlp_spatial_gating(z[..., o_u:o_v], z[..., o_v:o_q], gm_ln_g[l], gm_ln_b[l], gm_ws[l], gm_bs[l])
        yb = sliding_window_sink_attention(z[..., o_q:o_k], z[..., o_k:o_va], z[..., o_va:o_ga], attn_sinks[l])
        merged = jax.nn.sigmoid(z[..., o_ga:o_gb]) * (ya @ w_branch_a[l]) \
            + jax.nn.sigmoid(z[..., o_gb:]) * (yb @ w_branch_b[l])
        x = x + gt_mix * (merged @ w_out[l])

        h = rmsnorm(x, g_norm_mlp[l]) * (1.0 + sc_mlp) + sh_mlp
        x = x + gt_mlp * (jnp.square(jax.nn.relu(h @ w_up[l])) @ w_down[l])
    return rmsnorm(x, g_final)
```

```python
import functools

import jax
import jax.numpy as jnp
from jax import lax
from jax.experimental import pallas as pl
from jax.experimental.pallas import tpu as pltpu

D_MODEL = 1024
SEQ = 4096
BLK = 128
GM_GROUPS = 8
GM_GROUP_DIM = D_MODEL // GM_GROUPS
N_Q_HEADS = 16
N_KV_HEADS = 4
HEAD_DIM = 64
Q_REP = N_Q_HEADS // N_KV_HEADS
KV_WIDTH = N_KV_HEADS * HEAD_DIM
D_FF = 4 * D_MODEL
N_MOD = 6
EPS = 1e-6

O_U = 0
O_V = O_U + D_MODEL
O_Q = O_V + D_MODEL
O_K = O_Q + D_MODEL
O_VA = O_K + KV_WIDTH
O_GA = O_VA + KV_WIDTH
O_GB = O_GA + D_MODEL
IN_WIDTH = O_GB + D_MODEL

TS_MIX = 256
TM_MLP = 512
VMEM_LIMIT_BYTES = 56 * 1024 * 1024

F32 = jnp.float32
BF16 = jnp.bfloat16


def _sigmoid(z):
    return 1.0 / (1.0 + jnp.exp(-z))


def _gelu_tanh(z):
    return 0.5 * z * (1.0 + jnp.tanh(0.7978845608028654 * (z + 0.044715 * (z * z * z))))


def _dot(a, b):
    return jnp.dot(a, b, preferred_element_type=F32)


def _rms_modulate(x, g, scale, shift):
    ms = jnp.mean(x * x, axis=-1, keepdims=True)
    h = x * lax.rsqrt(ms + EPS) * g
    return h * (1.0 + scale) + shift


def _ada_kernel(c_ref, w_ref, b_ref, o_ref):
    c = c_ref[...]
    o_ref[...] = jnp.dot(c * _sigmoid(c), w_ref[...], preferred_element_type=F32,
                         precision=lax.Precision.HIGHEST) + b_ref[...]


def _ada_call(c, w_ada, b_ada):
    batch = c.shape[0]
    return pl.pallas_call(
        _ada_kernel,
        grid=(N_MOD,),
        in_specs=[
            pl.BlockSpec((batch, D_MODEL), lambda n: (0, 0)),
            pl.BlockSpec((D_MODEL, D_MODEL), lambda n: (0, n)),
            pl.BlockSpec((1, D_MODEL), lambda n: (0, n)),
        ],
        out_specs=pl.BlockSpec((batch, D_MODEL), lambda n: (0, n)),
        out_shape=jax.ShapeDtypeStruct((batch, N_MOD * D_MODEL), F32),
        name="ada_mod",
    )(c, w_ada, b_ada)


def _mixer_kernel(x_ref, mod_ref, gmix_ref, win_ref, bin_ref, lng_ref, lnb_ref, ws_ref, bst_ref,
                  sinks_ref, wa_ref, wb_ref, wo_ref, o_ref,
                  hb_ref, u_ref, vn_ref, ya_ref, q_ref, kv_ref, yb_ref, mg_ref):
    t = pl.program_id(1)
    n_blk = TS_MIX // BLK

    @pl.when(t == 0)
    def _():
        kv_ref[0:BLK, :] = jnp.zeros((BLK, 2 * KV_WIDTH), BF16)

    x = x_ref[...]
    h = _rms_modulate(x, gmix_ref[...], mod_ref[1:2, :], mod_ref[0:1, :])
    hb_ref[...] = h.astype(BF16)

    def proj(lo, hi):
        return _dot(hb_ref[...], win_ref[:, lo:hi]) + bin_ref[:, lo:hi]

    row = lax.broadcasted_iota(jnp.int32, (BLK, BLK), 0)
    col = lax.broadcasted_iota(jnp.int32, (BLK, BLK), 1)
    tri = row >= col

    u_ref[...] = _gelu_tanh(proj(O_U, O_V))
    gv = _gelu_tanh(proj(O_V, O_Q))
    mu = jnp.mean(gv, axis=-1, keepdims=True)
    gc = gv - mu
    var = jnp.mean(gc * gc, axis=-1, keepdims=True)
    vn = gc * lax.rsqrt(var + EPS) * lng_ref[...] + lnb_ref[...]
    vn_ref[...] = vn.astype(BF16)
    for g in range(GM_GROUPS):
        cs = slice(g * GM_GROUP_DIM, (g + 1) * GM_GROUP_DIM)
        wg = jnp.where(tri, ws_ref[g], 0.0).astype(BF16)
        bcol = bst_ref[:, g:g + 1]
        for j in range(n_blk):
            rs = slice(j * BLK, (j + 1) * BLK)
            s = _dot(wg, vn_ref[rs, cs]) + bcol
            ya_ref[rs, cs] = (u_ref[rs, cs] * s).astype(BF16)
    mg_ref[...] = _sigmoid(proj(O_GA, O_GB)) * _dot(ya_ref[...], wa_ref[...])

    q_ref[...] = (proj(O_Q, O_K) * (HEAD_DIM ** -0.5)).astype(BF16)
    kv_ref[BLK:, :] = proj(O_K, O_GA).astype(BF16)
    no_prev = jnp.where(t == 0, -jnp.inf, 0.0).astype(F32)
    for j in range(n_blk):
        rs = slice(j * BLK, (j + 1) * BLK)
        band = slice(j * BLK, j * BLK + 2 * BLK)
        for g in range(N_KV_HEADS):
            kc = kv_ref[band, g * HEAD_DIM:(g + 1) * HEAD_DIM]
            vc = kv_ref[band, KV_WIDTH + g * HEAD_DIM:KV_WIDTH + (g + 1) * HEAD_DIM]
            for r in range(Q_REP):
                hd = g * Q_REP + r
                hs = slice(hd * HEAD_DIM, (hd + 1) * HEAD_DIM)
                s = lax.dot_general(q_ref[rs, hs], kc, (((1,), (1,)), ((), ())),
                                    preferred_element_type=F32)
                s_prev = s[:, :BLK]
                if j == 0:
                    s_prev = s_prev + no_prev
                live = jnp.where(tri, s[:, BLK:], s_prev)
                sink = sinks_ref[hd]
                m = jnp.maximum(jnp.max(live, axis=-1, keepdims=True), sink)
                p = jnp.exp(live - m)
                den = jnp.sum(p, axis=-1, keepdims=True) + jnp.exp(sink - m)
                pc = jnp.concatenate([jnp.where(tri, 0.0, p), jnp.where(tri, p, 0.0)], axis=1)
                o = _dot(pc.astype(BF16), vc) / den
                yb_ref[rs, hs] = o.astype(BF16)
    kv_ref[0:BLK, :] = kv_ref[TS_MIX:TS_MIX + BLK, :]

    merged = mg_ref[...] + _sigmoid(proj(O_GB, IN_WIDTH)) * _dot(yb_ref[...], wb_ref[...])
    y = _dot(merged.astype(BF16), wo_ref[...])
    o_ref[...] = x_ref[...] + mod_ref[2:3, :] * y


def _resident(shape):
    nd = len(shape)
    return pl.BlockSpec(shape, lambda *_: (0,) * nd, pipeline_mode=pl.Buffered(1))


def _mixer_call(x, mod, g_mix, w_in, b_in, ln_g, ln_b, ws, bs_t, sinks, wa, wb, wo):
    batch, seq, _ = x.shape
    tile = pl.BlockSpec((None, TS_MIX, D_MODEL), lambda b, t: (b, t, 0))
    return pl.pallas_call(
        _mixer_kernel,
        grid=(batch, seq // TS_MIX),
        in_specs=[
            tile,
            pl.BlockSpec((None, N_MOD, D_MODEL), lambda b, t: (b, 0, 0)),
            _resident((1, D_MODEL)),
            _resident((D_MODEL, IN_WIDTH)),
            _resident((1, IN_WIDTH)),
            _resident((1, D_MODEL)),
            _resident((1, D_MODEL)),
            _resident((GM_GROUPS, BLK, BLK)),
            _resident((BLK, GM_GROUPS)),
            pl.BlockSpec(memory_space=pltpu.SMEM),
            _resident((D_MODEL, D_MODEL)),
            _resident((D_MODEL, D_MODEL)),
            _resident((D_MODEL, D_MODEL)),
        ],
        out_specs=tile,
        out_shape=jax.ShapeDtypeStruct(x.shape, F32),
        scratch_shapes=[
            pltpu.VMEM((TS_MIX, D_MODEL), BF16),
            pltpu.VMEM((TS_MIX, D_MODEL), F32),
            pltpu.VMEM((TS_MIX, D_MODEL), BF16),
            pltpu.VMEM((TS_MIX, D_MODEL), BF16),
            pltpu.VMEM((TS_MIX, D_MODEL), BF16),
            pltpu.VMEM((BLK + TS_MIX, 2 * KV_WIDTH), BF16),
            pltpu.VMEM((TS_MIX, D_MODEL), BF16),
            pltpu.VMEM((TS_MIX, D_MODEL), F32),
        ],
        compiler_params=pltpu.CompilerParams(
            dimension_semantics=("arbitrary", "arbitrary"),
            vmem_limit_bytes=VMEM_LIMIT_BYTES),
        name="token_mixer",
    )(x, mod, g_mix, w_in, b_in, ln_g, ln_b, ws, bs_t, sinks, wa, wb, wo)


def _mlp_kernel(x_ref, mod_ref, gmlp_ref, wup_ref, wdn_ref, gfin_ref, o_ref, hb_ref, act_ref):
    x = x_ref[...]
    h = _rms_modulate(x, gmlp_ref[...], mod_ref[4:5, :], mod_ref[3:4, :])
    hb_ref[...] = h.astype(BF16)
    for c in range(D_FF // D_MODEL):
        cs = slice(c * D_MODEL, (c + 1) * D_MODEL)
        a = jnp.maximum(_dot(hb_ref[...], wup_ref[:, cs]), 0.0)
        act_ref[:, cs] = (a * a).astype(BF16)
    x2 = x_ref[...] + mod_ref[5:6, :] * _dot(act_ref[...], wdn_ref[...])
    ms = jnp.mean(x2 * x2, axis=-1, keepdims=True)
    o_ref[...] = x2 * lax.rsqrt(ms + EPS) * gfin_ref[...]


def _mlp_call(x, mod, g_mlp, w_up, w_down, g_final):
    batch, seq, _ = x.shape
    tile = pl.BlockSpec((None, TM_MLP, D_MODEL), lambda b, t: (b, t, 0))
    return pl.pallas_call(
        _mlp_kernel,
        grid=(batch, seq // TM_MLP),
        in_specs=[
            tile,
            pl.BlockSpec((None, N_MOD, D_MODEL), lambda b, t: (b, 0, 0)),
            _resident((1, D_MODEL)),
            _resident((D_MODEL, D_FF)),
            _resident((D_FF, D_MODEL)),
            _resident((1, D_MODEL)),
        ],
        out_specs=tile,
        out_shape=jax.ShapeDtypeStruct(x.shape, F32),
        scratch_shapes=[
            pltpu.VMEM((TM_MLP, D_MODEL), BF16),
            pltpu.VMEM((TM_MLP, D_FF), BF16),
        ],
        compiler_params=pltpu.CompilerParams(
            dimension_semantics=("arbitrary", "arbitrary"),
            vmem_limit_bytes=VMEM_LIMIT_BYTES),
        name="channel_mlp",
    )(x, mod, g_mlp, w_up, w_down, g_final)


def kernel(x, c, w_ada, b_ada, g_norm_mix, w_in, b_in, gm_ln_g, gm_ln_b, gm_ws, gm_bs, attn_sinks,
           w_branch_a, w_branch_b, w_out, g_norm_mlp, w_up, w_down, g_final):
    batch = x.shape[0]
    depth = w_in.shape[0]
    row = lambda v: v.reshape(1, -1)
    for l in range(depth):
        mod = _ada_call(c, w_ada[l], row(b_ada[l])).reshape(batch, N_MOD, D_MODEL)
        x = _mixer_call(
            x, mod, row(g_norm_mix[l]), w_in[l].astype(BF16), row(b_in[l]),
            row(gm_ln_g[l]), row(gm_ln_b[l]), gm_ws[l], gm_bs[l].T, attn_sinks[l],
            w_branch_a[l].astype(BF16), w_branch_b[l].astype(BF16), w_out[l].astype(BF16))
        assert depth == 1
        x = _mlp_call(x, mod, row(g_norm_mlp[l]), w_up[l].astype(BF16), w_down[l].astype(BF16),
                      row(g_final))
    return x
```

```python
import jax
import jax.numpy as jnp
from jax import lax
from jax.experimental import pallas as pl
from jax.experimental.pallas import tpu as pltpu

D_MODEL = 1024
BLK = 128
GM_GROUPS = 8
GM_GROUP_DIM = D_MODEL // GM_GROUPS
N_Q_HEADS = 16
N_KV_HEADS = 4
HEAD_DIM = 64
Q_REP = N_Q_HEADS // N_KV_HEADS
KV_WIDTH = N_KV_HEADS * HEAD_DIM
D_FF = 4 * D_MODEL
N_MOD = 6
EPS = 1e-6

O_U = 0
O_V = O_U + D_MODEL
O_Q = O_V + D_MODEL
O_K = O_Q + D_MODEL
O_VA = O_K + KV_WIDTH
O_GA = O_VA + KV_WIDTH
O_GB = O_GA + D_MODEL
IN_WIDTH = O_GB + D_MODEL

LANES = 128
BF16_ROWS = 16
HEADS_PER_ITEM = 4
VT_ROWS = HEAD_DIM + BF16_ROWS
QV_ROWS = D_MODEL + KV_WIDTH

TS_MIX = 256
TM_MLP = 512
VMEM_LIMIT_BYTES = 56 * 1024 * 1024

F32 = jnp.float32
BF16 = jnp.bfloat16
NT_DIMS = (((1,), (1,)), ((), ()))
TN_DIMS = (((0,), (0,)), ((), ()))


def _sigmoid(z):
    return 1.0 / (1.0 + jnp.exp(-z))


def _gelu_tanh(z):
    return 0.5 * z * (1.0 + jnp.tanh(0.7978845608028654 * (z + 0.044715 * (z * z * z))))


def _dot(a, b):
    return jnp.dot(a, b, preferred_element_type=F32)


def _rms_modulate(x, g, scale, shift):
    ms = jnp.mean(x * x, axis=-1, keepdims=True)
    h = x * lax.rsqrt(ms + EPS) * g
    return h * (1.0 + scale) + shift


def _ada_kernel(c_ref, w_ref, b_ref, o_ref):
    c = c_ref[...]
    o_ref[...] = jnp.dot(c * _sigmoid(c), w_ref[...], preferred_element_type=F32,
                         precision=lax.Precision.HIGHEST) + b_ref[...]


def _ada_call(c, w_ada, b_ada):
    batch = c.shape[0]
    return pl.pallas_call(
        _ada_kernel,
        grid=(N_MOD,),
        in_specs=[
            pl.BlockSpec((batch, D_MODEL), lambda n: (0, 0)),
            pl.BlockSpec((D_MODEL, D_MODEL), lambda n: (0, n)),
            pl.BlockSpec((1, D_MODEL), lambda n: (0, n)),
        ],
        out_specs=pl.BlockSpec((batch, D_MODEL), lambda n: (0, n)),
        out_shape=jax.ShapeDtypeStruct((batch, N_MOD * D_MODEL), F32),
        name="ada_mod",
    )(c, w_ada, b_ada)


def _mixer_kernel(x_ref, mod_ref, gmix_ref, win_ref, bin_ref, wqvt_ref, bqv_ref, lng_ref, lnb_ref,
                  ws_ref, bst_ref, sinks_ref, wa_ref, wb_ref, wo_ref, o_ref,
                  hb_ref, u_ref, vn_ref, ya_ref, qt_ref, kb_ref, vt_ref, ybt_ref, mg_ref):
    t = pl.program_id(1)
    n_blk = TS_MIX // BLK

    @pl.when(t == 0)
    def _():
        kb_ref[0:BLK, :] = jnp.zeros((BLK, KV_WIDTH), BF16)
        r = lax.broadcasted_iota(jnp.int32, vt_ref.shape, 0)
        ones_row = r == HEAD_DIM
        for g in range(1, N_KV_HEADS):
            ones_row = ones_row | (r == g * VT_ROWS + HEAD_DIM)
        vt_ref[...] = jnp.where(ones_row, 1.0, 0.0).astype(BF16)

    x = x_ref[...]
    h = _rms_modulate(x, gmix_ref[...], mod_ref[1:2, :], mod_ref[0:1, :])
    hb_ref[...] = h.astype(BF16)

    def proj(lo, hi):
        return _dot(hb_ref[...], win_ref[:, lo:hi]) + bin_ref[:, lo:hi]

    row = lax.broadcasted_iota(jnp.int32, (BLK, BLK), 0)
    col = lax.broadcasted_iota(jnp.int32, (BLK, BLK), 1)
    tri = row >= col

    u_ref[...] = _gelu_tanh(proj(O_U, O_V))
    gv = _gelu_tanh(proj(O_V, O_Q))
    mu = jnp.mean(gv, axis=-1, keepdims=True)
    gc = gv - mu
    var = jnp.mean(gc * gc, axis=-1, keepdims=True)
    vn = gc * lax.rsqrt(var + EPS) * lng_ref[...] + lnb_ref[...]
    vn_ref[...] = vn.astype(BF16)
    for g in range(GM_GROUPS):
        cs = slice(g * GM_GROUP_DIM, (g + 1) * GM_GROUP_DIM)
        wg = jnp.where(tri, ws_ref[g], 0.0).astype(BF16)
        bcol = bst_ref[:, g:g + 1]
        for j in range(n_blk):
            rs = slice(j * BLK, (j + 1) * BLK)
            s = _dot(wg, vn_ref[rs, cs]) + bcol
            ya_ref[rs, cs] = (u_ref[rs, cs] * s).astype(BF16)
    mg_ref[...] = _sigmoid(proj(O_GA, O_GB)) * _dot(ya_ref[...], wa_ref[...])

    qv = lax.dot_general(wqvt_ref[...], hb_ref[...], NT_DIMS, preferred_element_type=F32)
    for j in range(n_blk):
        cs = slice(j * BLK, (j + 1) * BLK)
        blk = qv[:, cs] + bqv_ref[...]
        qt_ref[:, cs] = (blk[:D_MODEL] * (HEAD_DIM ** -0.5)).astype(BF16)
        for g in range(N_KV_HEADS):
            vt_ref[g * VT_ROWS:g * VT_ROWS + HEAD_DIM, BLK + j * BLK:BLK + (j + 1) * BLK] = (
                blk[D_MODEL + g * HEAD_DIM:D_MODEL + (g + 1) * HEAD_DIM].astype(BF16))
    kb_ref[BLK:, :] = proj(O_K, O_VA).astype(BF16)

    width = HEADS_PER_ITEM * BLK
    key = lax.broadcasted_iota(jnp.int32, (BLK, width), 0)
    qry = lax.broadcasted_iota(jnp.int32, (BLK, width), 1) & (BLK - 1)
    cur = key <= qry
    no_prev = jnp.where(t == 0, -jnp.inf, 0.0).astype(F32)
    zero_half = jnp.zeros((HEAD_DIM, width), BF16)

    items = [(j, g, p) for j in range(n_blk) for g in range(N_KV_HEADS)
             for p in range(Q_REP // HEADS_PER_ITEM)]

    def scores(item):
        j, g, p = item
        h0 = g * Q_REP + p * HEADS_PER_ITEM
        cs = slice(j * BLK, (j + 1) * BLK)
        kband = kb_ref[j * BLK:(j + 2) * BLK, (g // 2) * LANES:(g // 2 + 1) * LANES]
        qg = jnp.concatenate(
            [qt_ref[(h0 + e) * HEAD_DIM:(h0 + e + 1) * HEAD_DIM, cs] for e in range(HEADS_PER_ITEM)],
            axis=1)
        rhs = jnp.concatenate([qg, zero_half] if g % 2 == 0 else [zero_half, qg], axis=0)
        return _dot(kband, rhs)

    def softmax(item, st):
        j, g, p = item
        h0 = g * Q_REP + p * HEADS_PER_ITEM
        s_prev = st[:BLK]
        if j == 0:
            s_prev = s_prev + no_prev
        live = jnp.where(cur, st[BLK:], s_prev)
        sink = sinks_ref[:, h0 * BLK:h0 * BLK + width]
        m = jnp.maximum(jnp.max(live, axis=0, keepdims=True), sink)
        pr = jnp.exp(live - m)
        p_sink = jnp.exp(sink - m)
        pcat = jnp.concatenate([jnp.where(cur, 0.0, pr), jnp.where(cur, pr, 0.0)], axis=0)
        return pcat.astype(BF16), p_sink

    def attend(item, pcat, p_sink):
        j, g, p = item
        h0 = g * Q_REP + p * HEADS_PER_ITEM
        ot = _dot(vt_ref[g * VT_ROWS:(g + 1) * VT_ROWS, j * BLK:(j + 2) * BLK], pcat)
        inv = 1.0 / (ot[HEAD_DIM:HEAD_DIM + 1] + p_sink)
        o = (ot[:HEAD_DIM] * inv).astype(BF16)
        for e in range(HEADS_PER_ITEM):
            ybt_ref[(h0 + e) * HEAD_DIM:(h0 + e + 1) * HEAD_DIM, j * BLK:(j + 1) * BLK] = (
                o[:, e * BLK:(e + 1) * BLK])

    st = scores(items[0])
    for i, item in enumerate(items):
        st_next = scores(items[i + 1]) if i + 1 < len(items) else None
        pcat, p_sink = softmax(item, st)
        attend(item, pcat, p_sink)
        st = st_next

    kb_ref[0:BLK, :] = kb_ref[TS_MIX:TS_MIX + BLK, :]
    vt_ref[:, 0:BLK] = vt_ref[:, TS_MIX:TS_MIX + BLK]

    acc_b = lax.dot_general(ybt_ref[...], wb_ref[...], TN_DIMS, preferred_element_type=F32)
    merged = mg_ref[...] + _sigmoid(proj(O_GB, IN_WIDTH)) * acc_b
    y = _dot(merged.astype(BF16), wo_ref[...])
    o_ref[...] = x_ref[...] + mod_ref[2:3, :] * y


def _resident(shape):
    nd = len(shape)
    return pl.BlockSpec(shape, lambda *_: (0,) * nd, pipeline_mode=pl.Buffered(1))


def _mixer_call(x, mod, g_mix, w_in, b_in, w_qvt, b_qv, ln_g, ln_b, ws, bs_t, sinks_row, wa, wb, wo):
    batch, seq, _ = x.shape
    tile = pl.BlockSpec((None, TS_MIX, D_MODEL), lambda b, t: (b, t, 0))
    return pl.pallas_call(
        _mixer_kernel,
        grid=(batch, seq // TS_MIX),
        in_specs=[
            tile,
            pl.BlockSpec((None, N_MOD, D_MODEL), lambda b, t: (b, 0, 0)),
            _resident((1, D_MODEL)),
            _resident((D_MODEL, IN_WIDTH)),
            _resident((1, IN_WIDTH)),
            _resident((QV_ROWS, D_MODEL)),
            _resident((QV_ROWS, LANES)),
            _resident((1, D_MODEL)),
            _resident((1, D_MODEL)),
            _resident((GM_GROUPS, BLK, BLK)),
            _resident((BLK, GM_GROUPS)),
            _resident((1, N_Q_HEADS * BLK)),
            _resident((D_MODEL, D_MODEL)),
            _resident((D_MODEL, D_MODEL)),
            _resident((D_MODEL, D_MODEL)),
        ],
        out_specs=tile,
        out_shape=jax.ShapeDtypeStruct(x.shape, F32),
        scratch_shapes=[
            pltpu.VMEM((TS_MIX, D_MODEL), BF16),
            pltpu.VMEM((TS_MIX, D_MODEL), F32),
            pltpu.VMEM((TS_MIX, D_MODEL), BF16),
            pltpu.VMEM((TS_MIX, D_MODEL), BF16),
            pltpu.VMEM((D_MODEL, TS_MIX), BF16),
            pltpu.VMEM((BLK + TS_MIX, KV_WIDTH), BF16),
            pltpu.VMEM((N_KV_HEADS * VT_ROWS, BLK + TS_MIX), BF16),
            pltpu.VMEM((D_MODEL, TS_MIX), BF16),
            pltpu.VMEM((TS_MIX, D_MODEL), F32),
        ],
        compiler_params=pltpu.CompilerParams(
            dimension_semantics=("arbitrary", "arbitrary"),
            vmem_limit_bytes=VMEM_LIMIT_BYTES),
        name="token_mixer",
    )(x, mod, g_mix, w_in, b_in, w_qvt, b_qv, ln_g, ln_b, ws, bs_t, sinks_row, wa, wb, wo)


def _mlp_kernel(x_ref, mod_ref, gmlp_ref, wup_ref, wdn_ref, gfin_ref, o_ref, hb_ref, act_ref):
    x = x_ref[...]
    h = _rms_modulate(x, gmlp_ref[...], mod_ref[4:5, :], mod_ref[3:4, :])
    hb_ref[...] = h.astype(BF16)
    for c in range(D_FF // D_MODEL):
        cs = slice(c * D_MODEL, (c + 1) * D_MODEL)
        a = jnp.maximum(_dot(hb_ref[...], wup_ref[:, cs]), 0.0)
        act_ref[:, cs] = (a * a).astype(BF16)
    x2 = x_ref[...] + mod_ref[5:6, :] * _dot(act_ref[...], wdn_ref[...])
    ms = jnp.mean(x2 * x2, axis=-1, keepdims=True)
    o_ref[...] = x2 * lax.rsqrt(ms + EPS) * gfin_ref[...]


def _mlp_call(x, mod, g_mlp, w_up, w_down, g_final):
    batch, seq, _ = x.shape
    tile = pl.BlockSpec((None, TM_MLP, D_MODEL), lambda b, t: (b, t, 0))
    return pl.pallas_call(
        _mlp_kernel,
        grid=(batch, seq // TM_MLP),
        in_specs=[
            tile,
            pl.BlockSpec((None, N_MOD, D_MODEL), lambda b, t: (b, 0, 0)),
            _resident((1, D_MODEL)),
            _resident((D_MODEL, D_FF)),
            _resident((D_FF, D_MODEL)),
            _resident((1, D_MODEL)),
        ],
        out_specs=tile,
        out_shape=jax.ShapeDtypeStruct(x.shape, F32),
        scratch_shapes=[
            pltpu.VMEM((TM_MLP, D_MODEL), BF16),
            pltpu.VMEM((TM_MLP, D_FF), BF16),
        ],
        compiler_params=pltpu.CompilerParams(
            dimension_semantics=("arbitrary", "arbitrary"),
            vmem_limit_bytes=VMEM_LIMIT_BYTES),
        name="channel_mlp",
    )(x, mod, g_mlp, w_up, w_down, g_final)


def kernel(x, c, w_ada, b_ada, g_norm_mix, w_in, b_in, gm_ln_g, gm_ln_b, gm_ws, gm_bs, attn_sinks,
           w_branch_a, w_branch_b, w_out, g_norm_mlp, w_up, w_down, g_final):
    batch = x.shape[0]
    depth = w_in.shape[0]
    row = lambda v: v.reshape(1, -1)
    for l in range(depth):
        mod = _ada_call(c, w_ada[l], row(b_ada[l])).reshape(batch, N_MOD, D_MODEL)
        w_qv = jnp.concatenate([w_in[l][:, O_Q:O_K], w_in[l][:, O_VA:O_GA]], axis=1)
        b_qv = jnp.concatenate([b_in[l][O_Q:O_K], b_in[l][O_VA:O_GA]])
        b_qv = jnp.broadcast_to(b_qv[:, None], (QV_ROWS, LANES))
        sinks_row = row(jnp.repeat(attn_sinks[l], BLK))
        x = _mixer_call(
            x, mod, row(g_norm_mix[l]), w_in[l].astype(BF16), row(b_in[l]),
            w_qv.T.astype(BF16), b_qv, row(gm_ln_g[l]), row(gm_ln_b[l]), gm_ws[l], gm_bs[l].T,
            sinks_row, w_branch_a[l].astype(BF16), w_branch_b[l].astype(BF16), w_out[l].astype(BF16))
        assert depth == 1
        x = _mlp_call(x, mod, row(g_norm_mlp[l]), w_up[l].astype(BF16), w_down[l].astype(BF16),
                      row(g_final))
    return x
```

```python
import jax
import jax.numpy as jnp
from jax import lax
from jax.experimental import pallas as pl
from jax.experimental.pallas import tpu as pltpu

D_MODEL = 1024
BLK = 128
GM_GROUPS = 8
GM_GROUP_DIM = D_MODEL // GM_GROUPS
N_Q_HEADS = 16
N_KV_HEADS = 4
HEAD_DIM = 64
Q_REP = N_Q_HEADS // N_KV_HEADS
KV_WIDTH = N_KV_HEADS * HEAD_DIM
D_FF = 4 * D_MODEL
N_MOD = 6
EPS = 1e-6

O_U = 0
O_V = O_U + D_MODEL
O_Q = O_V + D_MODEL
O_K = O_Q + D_MODEL
O_VA = O_K + KV_WIDTH
O_GA = O_VA + KV_WIDTH
O_GB = O_GA + D_MODEL
IN_WIDTH = O_GB + D_MODEL

T_U = 0
T_V = T_U + D_MODEL
T_Q = T_V + D_MODEL
T_VA = T_Q + D_MODEL
T_ROWS = T_VA + KV_WIDTH
N_K = 0
N_GA = N_K + KV_WIDTH
N_GB = N_GA + D_MODEL
N_COLS = N_GB + D_MODEL

LANES = 128
BF16_ROWS = 16
HEADS_PER_ITEM = 4
VT_ROWS = HEAD_DIM + BF16_ROWS

TS_MIX = 512
TM_MLP = 512
VMEM_LIMIT_BYTES = 56 * 1024 * 1024

F32 = jnp.float32
BF16 = jnp.bfloat16
NT_DIMS = (((1,), (1,)), ((), ()))
TN_DIMS = (((0,), (0,)), ((), ()))

GELU_C0 = 0.7978845608028654
GELU_C1 = GELU_C0 * 0.044715


def _sigmoid(z):
    return 0.5 * jnp.tanh(0.5 * z) + 0.5


def _gelu_tanh(z):
    hz = 0.5 * z
    return hz + hz * jnp.tanh(z * (GELU_C0 + GELU_C1 * (z * z)))


def _dot(a, b):
    return jnp.dot(a, b, preferred_element_type=F32)


def _rms_modulate(x, g, scale, shift):
    ms = jnp.mean(x * x, axis=-1, keepdims=True)
    h = x * lax.rsqrt(ms + EPS) * g
    return h * (1.0 + scale) + shift


def _lane_tile(col, n):
    return jnp.concatenate([col] * n, axis=1)


def _ada_kernel(c_ref, w_ref, b_ref, o_ref):
    c = c_ref[...]
    o_ref[...] = jnp.dot(c * _sigmoid(c), w_ref[...], preferred_element_type=F32,
                         precision=lax.Precision.HIGHEST) + b_ref[...]


def _ada_call(c, w_ada, b_ada):
    batch = c.shape[0]
    return pl.pallas_call(
        _ada_kernel,
        grid=(N_MOD,),
        in_specs=[
            pl.BlockSpec((batch, D_MODEL), lambda n: (0, 0)),
            pl.BlockSpec((D_MODEL, D_MODEL), lambda n: (0, n)),
            pl.BlockSpec((1, D_MODEL), lambda n: (0, n)),
        ],
        out_specs=pl.BlockSpec((batch, D_MODEL), lambda n: (0, n)),
        out_shape=jax.ShapeDtypeStruct((batch, N_MOD * D_MODEL), F32),
        name="ada_mod",
    )(c, w_ada, b_ada)


def _mixer_kernel(x_ref, mod_ref, gmix_ref, wt_ref, bt_ref, wn_ref, bn_ref, lng_ref, lnb_ref,
                  wst_ref, bs_ref, sinks_ref, wa_ref, wb_ref, wo_ref, o_ref,
                  hb_ref, ut_ref, vnt_ref, yat_ref, qt_ref, kb_ref, vt_ref, ybt_ref, mg_ref):
    t = pl.program_id(1)
    n_blk = TS_MIX // BLK

    @pl.when(t == 0)
    def _():
        kb_ref[0:BLK, :] = jnp.zeros((BLK, KV_WIDTH), BF16)
        r = lax.broadcasted_iota(jnp.int32, vt_ref.shape, 0)
        ones_row = r == HEAD_DIM
        for g in range(1, N_KV_HEADS):
            ones_row = ones_row | (r == g * VT_ROWS + HEAD_DIM)
        vt_ref[...] = jnp.where(ones_row, 1.0, 0.0).astype(BF16)

    x = x_ref[...]
    h = _rms_modulate(x, gmix_ref[...], mod_ref[1:2, :], mod_ref[0:1, :])
    hb_ref[...] = h.astype(BF16)

    def proj_t(lo, hi):
        z = lax.dot_general(wt_ref[lo:hi, :], hb_ref[...], NT_DIMS, preferred_element_type=F32)
        return z + _lane_tile(bt_ref[lo:hi, :], n_blk)

    def proj_n(lo, hi):
        return _dot(hb_ref[...], wn_ref[:, lo:hi]) + bn_ref[:, lo:hi]

    half = D_MODEL // 2
    zu0 = proj_t(T_U, T_U + half)
    zu1 = proj_t(T_U + half, T_V)
    ut_ref[:half, :] = _gelu_tanh(zu0)
    zv0 = proj_t(T_V, T_V + half)
    ut_ref[half:, :] = _gelu_tanh(zu1)
    zv1 = proj_t(T_V + half, T_Q)
    gv0 = _gelu_tanh(zv0)
    zq0 = proj_t(T_Q, T_Q + half)
    gv1 = _gelu_tanh(zv1)
    zq1 = proj_t(T_Q + half, T_VA)
    gv = jnp.concatenate([gv0, gv1], axis=0)
    mu = jnp.mean(gv, axis=0, keepdims=True)
    gc = gv - mu
    var = jnp.mean(gc * gc, axis=0, keepdims=True)
    vn = gc * lax.rsqrt(var + EPS) * _lane_tile(lng_ref[...], n_blk) + _lane_tile(lnb_ref[...], n_blk)
    vnt_ref[...] = vn.astype(BF16)

    src = lax.broadcasted_iota(jnp.int32, (BLK, BLK), 0)
    dst = lax.broadcasted_iota(jnp.int32, (BLK, BLK), 1)
    causal_t = src <= dst
    zero_blk = jnp.zeros((BLK, BLK), BF16)
    for pg in range(GM_GROUPS // 2):
        g0, g1 = 2 * pg, 2 * pg + 1
        r0 = slice(g0 * GM_GROUP_DIM, (g0 + 1) * GM_GROUP_DIM)
        r1 = slice(g1 * GM_GROUP_DIM, (g1 + 1) * GM_GROUP_DIM)
        w0 = jnp.where(causal_t, wst_ref[g0], 0.0).astype(BF16)
        w1 = jnp.where(causal_t, wst_ref[g1], 0.0).astype(BF16)
        bd = jnp.concatenate([jnp.concatenate([w0, zero_blk], axis=1),
                              jnp.concatenate([zero_blk, w1], axis=1)], axis=0)
        lhs = jnp.concatenate(
            [jnp.concatenate([vnt_ref[r0, j * BLK:(j + 1) * BLK], vnt_ref[r1, j * BLK:(j + 1) * BLK]],
                             axis=1) for j in range(n_blk)], axis=0)
        st = _dot(lhs, bd) + bs_ref[:, g0 * BLK:(g1 + 1) * BLK]
        for j in range(n_blk):
            cs = slice(j * BLK, (j + 1) * BLK)
            rs = slice(j * BLK, (j + 1) * BLK)
            yat_ref[r0, cs] = (ut_ref[r0, cs] * st[rs, :BLK]).astype(BF16)
            yat_ref[r1, cs] = (ut_ref[r1, cs] * st[rs, BLK:]).astype(BF16)
    acc_a = lax.dot_general(yat_ref[...], wa_ref[...], TN_DIMS, preferred_element_type=F32)
    mg_ref[...] = _sigmoid(proj_n(N_GA, N_GB)) * acc_a

    qt_ref[:half, :] = (zq0 * (HEAD_DIM ** -0.5)).astype(BF16)
    qt_ref[half:, :] = (zq1 * (HEAD_DIM ** -0.5)).astype(BF16)
    vat = proj_t(T_VA, T_ROWS)
    for g in range(N_KV_HEADS):
        vt_ref[g * VT_ROWS:g * VT_ROWS + HEAD_DIM, BLK:] = (
            vat[g * HEAD_DIM:(g + 1) * HEAD_DIM].astype(BF16))
    kb_ref[BLK:, :] = proj_n(N_K, N_GA).astype(BF16)

    width = HEADS_PER_ITEM * BLK
    key = lax.broadcasted_iota(jnp.int32, (BLK, width), 0)
    qry = lax.broadcasted_iota(jnp.int32, (BLK, width), 1) & (BLK - 1)
    cur = key <= qry
    no_prev = jnp.where(t == 0, -jnp.inf, 0.0).astype(F32)
    zero_half = jnp.zeros((HEAD_DIM, width), BF16)

    items = [(j, g, p) for j in range(n_blk) for g in range(N_KV_HEADS)
             for p in range(Q_REP // HEADS_PER_ITEM)]

    def scores(item):
        j, g, p = item
        h0 = g * Q_REP + p * HEADS_PER_ITEM
        cs = slice(j * BLK, (j + 1) * BLK)
        kband = kb_ref[j * BLK:(j + 2) * BLK, (g // 2) * LANES:(g // 2 + 1) * LANES]
        qg = jnp.concatenate(
            [qt_ref[(h0 + e) * HEAD_DIM:(h0 + e + 1) * HEAD_DIM, cs] for e in range(HEADS_PER_ITEM)],
            axis=1)
        rhs = jnp.concatenate([qg, zero_half] if g % 2 == 0 else [zero_half, qg], axis=0)
        return _dot(kband, rhs)

    def softmax(item, st):
        j, g, p = item
        h0 = g * Q_REP + p * HEADS_PER_ITEM
        s_prev = st[:BLK]
        if j == 0:
            s_prev = s_prev + no_prev
        live = jnp.where(cur, st[BLK:], s_prev)
        sink = sinks_ref[:, h0 * BLK:h0 * BLK + width]
        m = jnp.maximum(jnp.max(live, axis=0, keepdims=True), sink)
        pr = jnp.exp(live - m)
        p_sink = jnp.exp(sink - m)
        pcat = jnp.concatenate([jnp.where(cur, 0.0, pr), jnp.where(cur, pr, 0.0)], axis=0)
        return pcat.astype(BF16), p_sink

    def attend(item, pcat, p_sink):
        j, g, p = item
        h0 = g * Q_REP + p * HEADS_PER_ITEM
        ot = _dot(vt_ref[g * VT_ROWS:(g + 1) * VT_ROWS, j * BLK:(j + 2) * BLK], pcat)
        inv = 1.0 / (ot[HEAD_DIM:HEAD_DIM + 1] + p_sink)
        o = (ot[:HEAD_DIM] * inv).astype(BF16)
        for e in range(HEADS_PER_ITEM):
            ybt_ref[(h0 + e) * HEAD_DIM:(h0 + e + 1) * HEAD_DIM, j * BLK:(j + 1) * BLK] = (
                o[:, e * BLK:(e + 1) * BLK])

    st = scores(items[0])
    for i, item in enumerate(items):
        st_next = scores(items[i + 1]) if i + 1 < len(items) else None
        pcat, p_sink = softmax(item, st)
        attend(item, pcat, p_sink)
        st = st_next

    kb_ref[0:BLK, :] = kb_ref[TS_MIX:TS_MIX + BLK, :]
    vt_ref[:, 0:BLK] = vt_ref[:, TS_MIX:TS_MIX + BLK]

    acc_b = lax.dot_general(ybt_ref[...], wb_ref[...], TN_DIMS, preferred_element_type=F32)
    merged = mg_ref[...] + _sigmoid(proj_n(N_GB, N_COLS)) * acc_b
    y = _dot(merged.astype(BF16), wo_ref[...])
    o_ref[...] = x_ref[...] + mod_ref[2:3, :] * y


def _resident(shape):
    nd = len(shape)
    return pl.BlockSpec(shape, lambda *_: (0,) * nd, pipeline_mode=pl.Buffered(1))


def _mixer_call(x, mod, g_mix, w_t, b_t, w_n, b_n, ln_g, ln_b, ws_t, bs_row, sinks_row, wa, wb, wo):
    batch, seq, _ = x.shape
    tile = pl.BlockSpec((None, TS_MIX, D_MODEL), lambda b, t: (b, t, 0))
    return pl.pallas_call(
        _mixer_kernel,
        grid=(batch, seq // TS_MIX),
        in_specs=[
            tile,
            pl.BlockSpec((None, N_MOD, D_MODEL), lambda b, t: (b, 0, 0)),
            _resident((1, D_MODEL)),
            _resident((T_ROWS, D_MODEL)),
            _resident((T_ROWS, LANES)),
            _resident((D_MODEL, N_COLS)),
            _resident((1, N_COLS)),
            _resident((D_MODEL, LANES)),
            _resident((D_MODEL, LANES)),
            _resident((GM_GROUPS, BLK, BLK)),
            _resident((1, GM_GROUPS * BLK)),
            _resident((1, N_Q_HEADS * BLK)),
            _resident((D_MODEL, D_MODEL)),
            _resident((D_MODEL, D_MODEL)),
            _resident((D_MODEL, D_MODEL)),
        ],
        out_specs=tile,
        out_shape=jax.ShapeDtypeStruct(x.shape, F32),
        scratch_shapes=[
            pltpu.VMEM((TS_MIX, D_MODEL), BF16),
            pltpu.VMEM((D_MODEL, TS_MIX), F32),
            pltpu.VMEM((D_MODEL, TS_MIX), BF16),
            pltpu.VMEM((D_MODEL, TS_MIX), BF16),
            pltpu.VMEM((D_MODEL, TS_MIX), BF16),
            pltpu.VMEM((BLK + TS_MIX, KV_WIDTH), BF16),
            pltpu.VMEM((N_KV_HEADS * VT_ROWS, BLK + TS_MIX), BF16),
            pltpu.VMEM((D_MODEL, TS_MIX), BF16),
            pltpu.VMEM((TS_MIX, D_MODEL), F32),
        ],
        compiler_params=pltpu.CompilerParams(
            dimension_semantics=("arbitrary", "arbitrary"),
            vmem_limit_bytes=VMEM_LIMIT_BYTES),
        name="token_mixer",
    )(x, mod, g_mix, w_t, b_t, w_n, b_n, ln_g, ln_b, ws_t, bs_row, sinks_row, wa, wb, wo)


def _mlp_kernel(x_ref, mod_ref, gmlp_ref, wup_ref, wdn_ref, gfin_ref, o_ref, hb_ref, act_ref):
    x = x_ref[...]
    h = _rms_modulate(x, gmlp_ref[...], mod_ref[4:5, :], mod_ref[3:4, :])
    hb_ref[...] = h.astype(BF16)
    for c in range(D_FF // D_MODEL):
        cs = slice(c * D_MODEL, (c + 1) * D_MODEL)
        a = jnp.maximum(_dot(hb_ref[...], wup_ref[:, cs]), 0.0)
        act_ref[:, cs] = (a * a).astype(BF16)
    x2 = x_ref[...] + mod_ref[5:6, :] * _dot(act_ref[...], wdn_ref[...])
    ms = jnp.mean(x2 * x2, axis=-1, keepdims=True)
    o_ref[...] = x2 * lax.rsqrt(ms + EPS) * gfin_ref[...]


def _mlp_call(x, mod, g_mlp, w_up, w_down, g_final):
    batch, seq, _ = x.shape
    tile = pl.BlockSpec((None, TM_MLP, D_MODEL), lambda b, t: (b, t, 0))
    return pl.pallas_call(
        _mlp_kernel,
        grid=(batch, seq // TM_MLP),
        in_specs=[
            tile,
            pl.BlockSpec((None, N_MOD, D_MODEL), lambda b, t: (b, 0, 0)),
            _resident((1, D_MODEL)),
            _resident((D_MODEL, D_FF)),
            _resident((D_FF, D_MODEL)),
            _resident((1, D_MODEL)),
        ],
        out_specs=tile,
        out_shape=jax.ShapeDtypeStruct(x.shape, F32),
        scratch_shapes=[
            pltpu.VMEM((TM_MLP, D_MODEL), BF16),
            pltpu.VMEM((TM_MLP, D_FF), BF16),
        ],
        compiler_params=pltpu.CompilerParams(
            dimension_semantics=("arbitrary", "arbitrary"),
            vmem_limit_bytes=VMEM_LIMIT_BYTES),
        name="channel_mlp",
    )(x, mod, g_mlp, w_up, w_down, g_final)


def kernel(x, c, w_ada, b_ada, g_norm_mix, w_in, b_in, gm_ln_g, gm_ln_b, gm_ws, gm_bs, attn_sinks,
           w_branch_a, w_branch_b, w_out, g_norm_mlp, w_up, w_down, g_final):
    batch = x.shape[0]
    depth = w_in.shape[0]
    row = lambda v: v.reshape(1, -1)
    lane_col = lambda v: jnp.broadcast_to(v[:, None], (v.shape[0], LANES))
    for l in range(depth):
        mod = _ada_call(c, w_ada[l], row(b_ada[l])).reshape(batch, N_MOD, D_MODEL)
        wl = w_in[l].astype(BF16)
        bl = b_in[l]
        w_t = jnp.concatenate([wl[:, O_U:O_K], wl[:, O_VA:O_GA]], axis=1).T
        b_t = lane_col(jnp.concatenate([bl[O_U:O_K], bl[O_VA:O_GA]]))
        w_n = jnp.concatenate([wl[:, O_K:O_VA], wl[:, O_GA:]], axis=1)
        b_n = row(jnp.concatenate([bl[O_K:O_VA], bl[O_GA:]]))
        x = _mixer_call(
            x, mod, row(g_norm_mix[l]), w_t, b_t, w_n, b_n,
            lane_col(gm_ln_g[l]), lane_col(gm_ln_b[l]), jnp.swapaxes(gm_ws[l], 1, 2), row(gm_bs[l]),
            row(jnp.repeat(attn_sinks[l], BLK)),
            w_branch_a[l].astype(BF16), w_branch_b[l].astype(BF16), w_out[l].astype(BF16))
        assert depth == 1
        x = _mlp_call(x, mod, row(g_norm_mlp[l]), w_up[l].astype(BF16), w_down[l].astype(BF16),
                      row(g_final))
    return x
```

```python
import jax
import jax.numpy as jnp
from jax import lax
from jax.experimental import pallas as pl
from jax.experimental.pallas import tpu as pltpu

D_MODEL = 1024
BLK = 128
GM_GROUPS = 8
GM_GROUP_DIM = D_MODEL // GM_GROUPS
N_Q_HEADS = 16
N_KV_HEADS = 4
HEAD_DIM = 64
Q_REP = N_Q_HEADS // N_KV_HEADS
KV_WIDTH = N_KV_HEADS * HEAD_DIM
D_FF = 4 * D_MODEL
N_MOD = 6
EPS = 1e-6

O_U = 0
O_V = O_U + D_MODEL
O_Q = O_V + D_MODEL
O_K = O_Q + D_MODEL
O_VA = O_K + KV_WIDTH
O_GA = O_VA + KV_WIDTH
O_GB = O_GA + D_MODEL
IN_WIDTH = O_GB + D_MODEL

T_U = 0
T_V = T_U + D_MODEL
T_Q = T_V + D_MODEL
T_VA = T_Q + D_MODEL
T_ROWS = T_VA + KV_WIDTH
N_K = 0
N_GA = N_K + KV_WIDTH
N_GB = N_GA + D_MODEL
N_COLS = N_GB + D_MODEL

LANES = 128
BF16_ROWS = 16
HEADS_PER_ITEM = 4
VT_ROWS = HEAD_DIM + BF16_ROWS

ATT_LAG = 3
TS_MIX = 512
TM_MLP = 512
VMEM_LIMIT_BYTES = 56 * 1024 * 1024

F32 = jnp.float32
BF16 = jnp.bfloat16
NT_DIMS = (((1,), (1,)), ((), ()))
TN_DIMS = (((0,), (0,)), ((), ()))

GELU_C0 = 0.7978845608028654
GELU_C1 = GELU_C0 * 0.044715


def _sigmoid(z):
    return 0.5 * jnp.tanh(0.5 * z) + 0.5


def _gelu_tanh(z):
    hz = 0.5 * z
    return hz + hz * jnp.tanh(z * (GELU_C0 + GELU_C1 * (z * z)))


def _dot(a, b):
    return jnp.dot(a, b, preferred_element_type=F32)


def _rms_modulate(x, g, scale, shift):
    ms = jnp.mean(x * x, axis=-1, keepdims=True)
    h = x * lax.rsqrt(ms + EPS) * g
    return h * (1.0 + scale) + shift


def _lane_tile(col, n):
    return jnp.concatenate([col] * n, axis=1)


def _ada_kernel(c_ref, w_ref, b_ref, o_ref):
    c = c_ref[...]
    o_ref[...] = jnp.dot(c * _sigmoid(c), w_ref[...], preferred_element_type=F32,
                         precision=lax.Precision.HIGHEST) + b_ref[...]


def _ada_call(c, w_ada, b_ada):
    batch = c.shape[0]
    return pl.pallas_call(
        _ada_kernel,
        grid=(N_MOD,),
        in_specs=[
            pl.BlockSpec((batch, D_MODEL), lambda n: (0, 0)),
            pl.BlockSpec((D_MODEL, D_MODEL), lambda n: (0, n)),
            pl.BlockSpec((1, D_MODEL), lambda n: (0, n)),
        ],
        out_specs=pl.BlockSpec((batch, D_MODEL), lambda n: (0, n)),
        out_shape=jax.ShapeDtypeStruct((batch, N_MOD * D_MODEL), F32),
        name="ada_mod",
    )(c, w_ada, b_ada)


def _mixer_kernel(x_ref, mod_ref, gmix_ref, wt_ref, bt_ref, wn_ref, bn_ref, lng_ref, lnb_ref,
                  wst_ref, bs_ref, sinks_ref, wa_ref, wb_ref, wo_ref, o_ref,
                  hb_ref, ut_ref, vnt_ref, yat_ref, qt_ref, kb_ref, vt_ref, ybt_ref, mg_ref, sgb_ref):
    t = pl.program_id(1)
    n_blk = TS_MIX // BLK

    @pl.when(t == 0)
    def _():
        kb_ref[0:BLK, :] = jnp.zeros((BLK, KV_WIDTH), BF16)
        r = lax.broadcasted_iota(jnp.int32, vt_ref.shape, 0)
        ones_row = r == HEAD_DIM
        for g in range(1, N_KV_HEADS):
            ones_row = ones_row | (r == g * VT_ROWS + HEAD_DIM)
        vt_ref[...] = jnp.where(ones_row, 1.0, 0.0).astype(BF16)

    x = x_ref[...]
    h = _rms_modulate(x, gmix_ref[...], mod_ref[1:2, :], mod_ref[0:1, :])
    hb_ref[...] = h.astype(BF16)

    def proj_t(lo, hi):
        z = lax.dot_general(wt_ref[lo:hi, :], hb_ref[...], NT_DIMS, preferred_element_type=F32)
        return z + _lane_tile(bt_ref[lo:hi, :], n_blk)

    def proj_n(lo, hi):
        z = jnp.dot(hb_ref[...], wn_ref[:, lo:hi], preferred_element_type=F32)
        return z + bn_ref[:, lo:hi]

    half = D_MODEL // 2
    zv0 = proj_t(T_V, T_V + half)
    zv1 = proj_t(T_V + half, T_Q)
    zu0 = proj_t(T_U, T_U + half)
    zu1 = proj_t(T_U + half, T_V)
    gv0 = _gelu_tanh(zv0)
    zq0 = proj_t(T_Q, T_Q + half)
    gv1 = _gelu_tanh(zv1)
    zq1 = proj_t(T_Q + half, T_VA)
    gv = jnp.concatenate([gv0, gv1], axis=0)
    mu = jnp.mean(gv, axis=0, keepdims=True)
    gc = gv - mu
    var = jnp.mean(gc * gc, axis=0, keepdims=True)
    vn = gc * lax.rsqrt(var + EPS) * _lane_tile(lng_ref[...], n_blk) + _lane_tile(lnb_ref[...], n_blk)
    vnt_ref[...] = vn.astype(BF16)
    vat = proj_t(T_VA, T_ROWS)
    kb_ref[BLK:, :] = proj_n(N_K, N_GA).astype(BF16)
    ut_ref[:half, :] = _gelu_tanh(zu0)
    mg_ref[:, :half] = _sigmoid(proj_n(N_GA, N_GA + half))
    ut_ref[half:, :] = _gelu_tanh(zu1)
    mg_ref[:, half:] = _sigmoid(proj_n(N_GA + half, N_GB))
    qt_ref[:half, :] = (zq0 * (HEAD_DIM ** -0.5)).astype(BF16)
    qt_ref[half:, :] = (zq1 * (HEAD_DIM ** -0.5)).astype(BF16)
    for g in range(N_KV_HEADS):
        vt_ref[g * VT_ROWS:g * VT_ROWS + HEAD_DIM, BLK:] = (
            vat[g * HEAD_DIM:(g + 1) * HEAD_DIM].astype(BF16))

    src = lax.broadcasted_iota(jnp.int32, (BLK, BLK), 0)
    dst = lax.broadcasted_iota(jnp.int32, (BLK, BLK), 1)
    causal_t = src <= dst
    zero_blk = jnp.zeros((BLK, BLK), BF16)
    for pg in range(GM_GROUPS // 2):
        g0, g1 = 2 * pg, 2 * pg + 1
        r0 = slice(g0 * GM_GROUP_DIM, (g0 + 1) * GM_GROUP_DIM)
        r1 = slice(g1 * GM_GROUP_DIM, (g1 + 1) * GM_GROUP_DIM)
        w0 = jnp.where(causal_t, wst_ref[g0], 0.0).astype(BF16)
        w1 = jnp.where(causal_t, wst_ref[g1], 0.0).astype(BF16)
        bd = jnp.concatenate([jnp.concatenate([w0, zero_blk], axis=1),
                              jnp.concatenate([zero_blk, w1], axis=1)], axis=0)
        lhs = jnp.concatenate(
            [jnp.concatenate([vnt_ref[r0, j * BLK:(j + 1) * BLK], vnt_ref[r1, j * BLK:(j + 1) * BLK]],
                             axis=1) for j in range(n_blk)], axis=0)
        st = jnp.dot(lhs, bd, preferred_element_type=F32) + bs_ref[:, g0 * BLK:(g1 + 1) * BLK]
        for j in range(n_blk):
            cs = slice(j * BLK, (j + 1) * BLK)
            rs = slice(j * BLK, (j + 1) * BLK)
            yat_ref[r0, cs] = (ut_ref[r0, cs] * st[rs, :BLK]).astype(BF16)
            yat_ref[r1, cs] = (ut_ref[r1, cs] * st[rs, BLK:]).astype(BF16)
    sgb_ref[...] = _sigmoid(proj_n(N_GB, N_COLS)).astype(BF16)

    width = HEADS_PER_ITEM * BLK
    key = lax.broadcasted_iota(jnp.int32, (BLK, width), 0)
    qry = lax.broadcasted_iota(jnp.int32, (BLK, width), 1) & (BLK - 1)
    cur = key <= qry
    no_prev = jnp.where(t == 0, -jnp.inf, 0.0).astype(F32)
    zero_half = jnp.zeros((HEAD_DIM, width), BF16)

    items = [(j, g, p) for j in range(n_blk) for g in range(N_KV_HEADS)
             for p in range(Q_REP // HEADS_PER_ITEM)]

    def scores(item):
        j, g, p = item
        h0 = g * Q_REP + p * HEADS_PER_ITEM
        cs = slice(j * BLK, (j + 1) * BLK)
        kband = kb_ref[j * BLK:(j + 2) * BLK, (g // 2) * LANES:(g // 2 + 1) * LANES]
        qg = jnp.concatenate(
            [qt_ref[(h0 + e) * HEAD_DIM:(h0 + e + 1) * HEAD_DIM, cs] for e in range(HEADS_PER_ITEM)],
            axis=1)
        rhs = jnp.concatenate([qg, zero_half] if g % 2 == 0 else [zero_half, qg], axis=0)
        return jnp.dot(kband, rhs, preferred_element_type=F32)

    def softmax(item, st):
        j, g, p = item
        h0 = g * Q_REP + p * HEADS_PER_ITEM
        s_prev = st[:BLK]
        if j == 0:
            s_prev = s_prev + no_prev
        live = jnp.where(cur, st[BLK:], s_prev)
        sink = sinks_ref[:, h0 * BLK:h0 * BLK + width]
        m = jnp.maximum(jnp.max(live, axis=0, keepdims=True), sink)
        pr = jnp.exp(live - m)
        p_sink = jnp.exp(sink - m)
        pcat = jnp.concatenate([jnp.where(cur, 0.0, pr), jnp.where(cur, pr, 0.0)], axis=0)
        return pcat.astype(BF16), p_sink

    def attend(item, pcat, p_sink):
        j, g, p = item
        h0 = g * Q_REP + p * HEADS_PER_ITEM
        ot = jnp.dot(vt_ref[g * VT_ROWS:(g + 1) * VT_ROWS, j * BLK:(j + 2) * BLK], pcat,
                     preferred_element_type=F32)
        inv = 1.0 / (ot[HEAD_DIM:HEAD_DIM + 1] + p_sink)
        o = (ot[:HEAD_DIM] * inv).astype(BF16)
        for e in range(HEADS_PER_ITEM):
            ybt_ref[(h0 + e) * HEAD_DIM:(h0 + e + 1) * HEAD_DIM, j * BLK:(j + 1) * BLK] = (
                o[:, e * BLK:(e + 1) * BLK])

    def gate_a(half_idx):
        rs = slice(half_idx * (TS_MIX // 2), (half_idx + 1) * (TS_MIX // 2))
        acc = lax.dot_general(yat_ref[:, rs], wa_ref[...], TN_DIMS, preferred_element_type=F32)
        mg_ref[rs, :] = mg_ref[rs, :] * acc

    n_items = len(items)
    fill = {n_items // 4 + ATT_LAG: 0, (3 * n_items) // 4 + ATT_LAG: 1}
    st, pc = {}, {}
    for step in range(n_items + 2 * ATT_LAG):
        if step < n_items:
            st[step] = scores(items[step])
        if step in fill:
            gate_a(fill[step])
        i = step - ATT_LAG
        if 0 <= i < n_items:
            pc[i] = softmax(items[i], st.pop(i))
        i = step - 2 * ATT_LAG
        if 0 <= i < n_items:
            attend(items[i], *pc.pop(i))

    kb_ref[0:BLK, :] = kb_ref[TS_MIX:TS_MIX + BLK, :]
    vt_ref[:, 0:BLK] = vt_ref[:, TS_MIX:TS_MIX + BLK]

    acc_b = lax.dot_general(ybt_ref[...], wb_ref[...], TN_DIMS, preferred_element_type=F32)
    merged = mg_ref[...] + sgb_ref[...].astype(F32) * acc_b
    y = jnp.dot(merged.astype(BF16), wo_ref[...], preferred_element_type=F32)
    o_ref[...] = x_ref[...] + mod_ref[2:3, :] * y


def _resident(shape):
    nd = len(shape)
    return pl.BlockSpec(shape, lambda *_: (0,) * nd, pipeline_mode=pl.Buffered(1))


def _mixer_call(x, mod, g_mix, w_t, b_t, w_n, b_n, ln_g, ln_b, ws_t, bs_row, sinks_row, wa, wb, wo):
    batch, seq, _ = x.shape
    tile = pl.BlockSpec((None, TS_MIX, D_MODEL), lambda b, t: (b, t, 0))
    return pl.pallas_call(
        _mixer_kernel,
        grid=(batch, seq // TS_MIX),
        in_specs=[
            tile,
            pl.BlockSpec((None, N_MOD, D_MODEL), lambda b, t: (b, 0, 0)),
            _resident((1, D_MODEL)),
            _resident((T_ROWS, D_MODEL)),
            _resident((T_ROWS, LANES)),
            _resident((D_MODEL, N_COLS)),
            _resident((1, N_COLS)),
            _resident((D_MODEL, LANES)),
            _resident((D_MODEL, LANES)),
            _resident((GM_GROUPS, BLK, BLK)),
            _resident((1, GM_GROUPS * BLK)),
            _resident((1, N_Q_HEADS * BLK)),
            _resident((D_MODEL, D_MODEL)),
            _resident((D_MODEL, D_MODEL)),
            _resident((D_MODEL, D_MODEL)),
        ],
        out_specs=tile,
        out_shape=jax.ShapeDtypeStruct(x.shape, F32),
        scratch_shapes=[
            pltpu.VMEM((TS_MIX, D_MODEL), BF16),
            pltpu.VMEM((D_MODEL, TS_MIX), F32),
            pltpu.VMEM((D_MODEL, TS_MIX), BF16),
            pltpu.VMEM((D_MODEL, TS_MIX), BF16),
            pltpu.VMEM((D_MODEL, TS_MIX), BF16),
            pltpu.VMEM((BLK + TS_MIX, KV_WIDTH), BF16),
            pltpu.VMEM((N_KV_HEADS * VT_ROWS, BLK + TS_MIX), BF16),
            pltpu.VMEM((D_MODEL, TS_MIX), BF16),
            pltpu.VMEM((TS_MIX, D_MODEL), F32),
            pltpu.VMEM((TS_MIX, D_MODEL), BF16),
        ],
        compiler_params=pltpu.CompilerParams(
            dimension_semantics=("arbitrary", "arbitrary"),
            vmem_limit_bytes=VMEM_LIMIT_BYTES),
        name="token_mixer",
    )(x, mod, g_mix, w_t, b_t, w_n, b_n, ln_g, ln_b, ws_t, bs_row, sinks_row, wa, wb, wo)


def _mlp_kernel(x_ref, mod_ref, gmlp_ref, wup_ref, wdn_ref, gfin_ref, o_ref, hb_ref, act_ref):
    x = x_ref[...]
    h = _rms_modulate(x, gmlp_ref[...], mod_ref[4:5, :], mod_ref[3:4, :])
    hb_ref[...] = h.astype(BF16)
    for c in range(D_FF // D_MODEL):
        cs = slice(c * D_MODEL, (c + 1) * D_MODEL)
        a = jnp.maximum(_dot(hb_ref[...], wup_ref[:, cs]), 0.0)
        act_ref[:, cs] = (a * a).astype(BF16)
    x2 = x_ref[...] + mod_ref[5:6, :] * _dot(act_ref[...], wdn_ref[...])
    ms = jnp.mean(x2 * x2, axis=-1, keepdims=True)
    o_ref[...] = x2 * lax.rsqrt(ms + EPS) * gfin_ref[...]


def _mlp_call(x, mod, g_mlp, w_up, w_down, g_final):
    batch, seq, _ = x.shape
    tile = pl.BlockSpec((None, TM_MLP, D_MODEL), lambda b, t: (b, t, 0))
    return pl.pallas_call(
        _mlp_kernel,
        grid=(batch, seq // TM_MLP),
        in_specs=[
            tile,
            pl.BlockSpec((None, N_MOD, D_MODEL), lambda b, t: (b, 0, 0)),
            _resident((1, D_MODEL)),
            _resident((D_MODEL, D_FF)),
            _resident((D_FF, D_MODEL)),
            _resident((1, D_MODEL)),
        ],
        out_specs=tile,
        out_shape=jax.ShapeDtypeStruct(x.shape, F32),
        scratch_shapes=[
            pltpu.VMEM((TM_MLP, D_MODEL), BF16),
            pltpu.VMEM((TM_MLP, D_FF), BF16),
        ],
        compiler_params=pltpu.CompilerParams(
            dimension_semantics=("arbitrary", "arbitrary"),
            vmem_limit_bytes=VMEM_LIMIT_BYTES),
        name="channel_mlp",
    )(x, mod, g_mlp, w_up, w_down, g_final)


def kernel(x, c, w_ada, b_ada, g_norm_mix, w_in, b_in, gm_ln_g, gm_ln_b, gm_ws, gm_bs, attn_sinks,
           w_branch_a, w_branch_b, w_out, g_norm_mlp, w_up, w_down, g_final):
    batch = x.shape[0]
    depth = w_in.shape[0]
    row = lambda v: v.reshape(1, -1)
    lane_col = lambda v: jnp.broadcast_to(v[:, None], (v.shape[0], LANES))
    for l in range(depth):
        mod = _ada_call(c, w_ada[l], row(b_ada[l])).reshape(batch, N_MOD, D_MODEL)
        wl = w_in[l].astype(BF16)
        bl = b_in[l]
        w_t = jnp.concatenate([wl[:, O_U:O_K], wl[:, O_VA:O_GA]], axis=1).T
        b_t = lane_col(jnp.concatenate([bl[O_U:O_K], bl[O_VA:O_GA]]))
        w_n = jnp.concatenate([wl[:, O_K:O_VA], wl[:, O_GA:]], axis=1)
        b_n = row(jnp.concatenate([bl[O_K:O_VA], bl[O_GA:]]))
        x = _mixer_call(
            x, mod, row(g_norm_mix[l]), w_t, b_t, w_n, b_n,
            lane_col(gm_ln_g[l]), lane_col(gm_ln_b[l]), jnp.swapaxes(gm_ws[l], 1, 2), row(gm_bs[l]),
            row(jnp.repeat(attn_sinks[l], BLK)),
            w_branch_a[l].astype(BF16), w_branch_b[l].astype(BF16), w_out[l].astype(BF16))
        assert depth == 1
        x = _mlp_call(x, mod, row(g_norm_mlp[l]), w_up[l].astype(BF16), w_down[l].astype(BF16),
                      row(g_final))
    return x
```

```python
import jax
import jax.numpy as jnp
from jax import lax
from jax.experimental import pallas as pl
from jax.experimental.pallas import tpu as pltpu

D_MODEL = 1024
BLK = 128
GM_GROUPS = 8
GM_GROUP_DIM = D_MODEL // GM_GROUPS
N_Q_HEADS = 16
N_KV_HEADS = 4
HEAD_DIM = 64
Q_REP = N_Q_HEADS // N_KV_HEADS
KV_WIDTH = N_KV_HEADS * HEAD_DIM
D_FF = 4 * D_MODEL
N_MOD = 6
EPS = 1e-6

O_U = 0
O_V = O_U + D_MODEL
O_Q = O_V + D_MODEL
O_K = O_Q + D_MODEL
O_VA = O_K + KV_WIDTH
O_GA = O_VA + KV_WIDTH
O_GB = O_GA + D_MODEL
IN_WIDTH = O_GB + D_MODEL

T_U = 0
T_V = T_U + D_MODEL
T_Q = T_V + D_MODEL
T_VA = T_Q + D_MODEL
T_ROWS = T_VA + KV_WIDTH
N_K = 0
N_GA = N_K + KV_WIDTH
N_GB = N_GA + D_MODEL
N_COLS = N_GB + D_MODEL

LANES = 128
BF16_ROWS = 16
HEADS_PER_ITEM = 4
VT_ROWS = HEAD_DIM + BF16_ROWS

ATT_LAG = 3
TS_MIX = 512
TM_MLP = 1024
VMEM_LIMIT_BYTES = 56 * 1024 * 1024

F32 = jnp.float32
BF16 = jnp.bfloat16
NT_DIMS = (((1,), (1,)), ((), ()))
TN_DIMS = (((0,), (0,)), ((), ()))

GELU_C0 = 0.7978845608028654
GELU_C1 = GELU_C0 * 0.044715


def _sigmoid(z):
    return 0.5 * jnp.tanh(0.5 * z) + 0.5


def _gelu_tanh(z):
    hz = 0.5 * z
    return hz + hz * jnp.tanh(z * (GELU_C0 + GELU_C1 * (z * z)))


def _dot(a, b):
    return jnp.dot(a, b, preferred_element_type=F32)


def _rms_modulate(x, g, scale, shift):
    ms = jnp.mean(x * x, axis=-1, keepdims=True)
    h = x * lax.rsqrt(ms + EPS) * g
    return h * (1.0 + scale) + shift


def _lane_tile(col, n):
    return jnp.concatenate([col] * n, axis=1)


def _ada_kernel(c_ref, w_ref, b_ref, o_ref):
    c = c_ref[...]
    o_ref[...] = jnp.dot(c * _sigmoid(c), w_ref[...], preferred_element_type=F32,
                         precision=lax.Precision.HIGHEST) + b_ref[...]


def _ada_call(c, w_ada, b_ada):
    batch = c.shape[0]
    return pl.pallas_call(
        _ada_kernel,
        grid=(N_MOD,),
        in_specs=[
            pl.BlockSpec((batch, D_MODEL), lambda n: (0, 0)),
            pl.BlockSpec((D_MODEL, D_MODEL), lambda n: (0, n)),
            pl.BlockSpec((1, D_MODEL), lambda n: (0, n)),
        ],
        out_specs=pl.BlockSpec((batch, D_MODEL), lambda n: (0, n)),
        out_shape=jax.ShapeDtypeStruct((batch, N_MOD * D_MODEL), F32),
        name="ada_mod",
    )(c, w_ada, b_ada)


def _mixer_kernel(x_ref, mod_ref, gmix_ref, wt_ref, bt_ref, wn_ref, bn_ref, lng_ref, lnb_ref,
                  wst_ref, bs_ref, sinks_ref, wa_ref, wb_ref, wo_ref, o_ref,
                  hb_ref, ut_ref, vnt_ref, yat_ref, qt_ref, kb_ref, vt_ref, ybt_ref, mg_ref, sgb_ref):
    t = pl.program_id(1)
    n_blk = TS_MIX // BLK

    @pl.when(t == 0)
    def _():
        kb_ref[0:BLK, :] = jnp.zeros((BLK, KV_WIDTH), BF16)
        r = lax.broadcasted_iota(jnp.int32, vt_ref.shape, 0)
        ones_row = r == HEAD_DIM
        for g in range(1, N_KV_HEADS):
            ones_row = ones_row | (r == g * VT_ROWS + HEAD_DIM)
        vt_ref[...] = jnp.where(ones_row, 1.0, 0.0).astype(BF16)

    x = x_ref[...]
    h = _rms_modulate(x, gmix_ref[...], mod_ref[1:2, :], mod_ref[0:1, :])
    hb_ref[...] = h.astype(BF16)

    def proj_t(lo, hi):
        z = lax.dot_general(wt_ref[lo:hi, :], hb_ref[...], NT_DIMS, preferred_element_type=F32)
        return z + _lane_tile(bt_ref[lo:hi, :], n_blk)

    def proj_n(lo, hi):
        z = jnp.dot(hb_ref[...], wn_ref[:, lo:hi], preferred_element_type=F32)
        return z + bn_ref[:, lo:hi]

    half = D_MODEL // 2
    zv0 = proj_t(T_V, T_V + half)
    zv1 = proj_t(T_V + half, T_Q)
    zu0 = proj_t(T_U, T_U + half)
    zu1 = proj_t(T_U + half, T_V)
    gv0 = _gelu_tanh(zv0)
    zq0 = proj_t(T_Q, T_Q + half)
    gv1 = _gelu_tanh(zv1)
    zq1 = proj_t(T_Q + half, T_VA)
    gv = jnp.concatenate([gv0, gv1], axis=0)
    mu = jnp.mean(gv, axis=0, keepdims=True)
    gc = gv - mu
    var = jnp.mean(gc * gc, axis=0, keepdims=True)
    vn = gc * lax.rsqrt(var + EPS) * _lane_tile(lng_ref[...], n_blk) + _lane_tile(lnb_ref[...], n_blk)
    vnt_ref[...] = vn.astype(BF16)
    vat = proj_t(T_VA, T_ROWS)
    kb_ref[BLK:, :] = proj_n(N_K, N_GA).astype(BF16)
    ut_ref[:half, :] = _gelu_tanh(zu0)
    mg_ref[:, :half] = _sigmoid(proj_n(N_GA, N_GA + half))
    ut_ref[half:, :] = _gelu_tanh(zu1)
    mg_ref[:, half:] = _sigmoid(proj_n(N_GA + half, N_GB))
    qt_ref[:half, :] = (zq0 * (HEAD_DIM ** -0.5)).astype(BF16)
    qt_ref[half:, :] = (zq1 * (HEAD_DIM ** -0.5)).astype(BF16)
    for g in range(N_KV_HEADS):
        vt_ref[g * VT_ROWS:g * VT_ROWS + HEAD_DIM, BLK:] = (
            vat[g * HEAD_DIM:(g + 1) * HEAD_DIM].astype(BF16))

    src = lax.broadcasted_iota(jnp.int32, (BLK, BLK), 0)
    dst = lax.broadcasted_iota(jnp.int32, (BLK, BLK), 1)
    causal_t = src <= dst
    zero_blk = jnp.zeros((BLK, BLK), BF16)
    for pg in range(GM_GROUPS // 2):
        g0, g1 = 2 * pg, 2 * pg + 1
        r0 = slice(g0 * GM_GROUP_DIM, (g0 + 1) * GM_GROUP_DIM)
        r1 = slice(g1 * GM_GROUP_DIM, (g1 + 1) * GM_GROUP_DIM)
        w0 = jnp.where(causal_t, wst_ref[g0], 0.0).astype(BF16)
        w1 = jnp.where(causal_t, wst_ref[g1], 0.0).astype(BF16)
        bd = jnp.concatenate([jnp.concatenate([w0, zero_blk], axis=1),
                              jnp.concatenate([zero_blk, w1], axis=1)], axis=0)
        lhs = jnp.concatenate(
            [jnp.concatenate([vnt_ref[r0, j * BLK:(j + 1) * BLK], vnt_ref[r1, j * BLK:(j + 1) * BLK]],
                             axis=1) for j in range(n_blk)], axis=0)
        st = jnp.dot(lhs, bd, preferred_element_type=F32) + bs_ref[:, g0 * BLK:(g1 + 1) * BLK]
        for j in range(n_blk):
            cs = slice(j * BLK, (j + 1) * BLK)
            rs = slice(j * BLK, (j + 1) * BLK)
            yat_ref[r0, cs] = (ut_ref[r0, cs] * st[rs, :BLK]).astype(BF16)
            yat_ref[r1, cs] = (ut_ref[r1, cs] * st[rs, BLK:]).astype(BF16)
    sgb_ref[...] = _sigmoid(proj_n(N_GB, N_COLS)).astype(BF16)

    width = HEADS_PER_ITEM * BLK
    key = lax.broadcasted_iota(jnp.int32, (BLK, width), 0)
    qry = lax.broadcasted_iota(jnp.int32, (BLK, width), 1) & (BLK - 1)
    cur = key <= qry
    no_prev = jnp.where(t == 0, -jnp.inf, 0.0).astype(F32)
    zero_half = jnp.zeros((HEAD_DIM, width), BF16)

    items = [(j, g, p) for j in range(n_blk) for g in range(N_KV_HEADS)
             for p in range(Q_REP // HEADS_PER_ITEM)]

    def scores(item):
        j, g, p = item
        h0 = g * Q_REP + p * HEADS_PER_ITEM
        cs = slice(j * BLK, (j + 1) * BLK)
        kband = kb_ref[j * BLK:(j + 2) * BLK, (g // 2) * LANES:(g // 2 + 1) * LANES]
        qg = jnp.concatenate(
            [qt_ref[(h0 + e) * HEAD_DIM:(h0 + e + 1) * HEAD_DIM, cs] for e in range(HEADS_PER_ITEM)],
            axis=1)
        rhs = jnp.concatenate([qg, zero_half] if g % 2 == 0 else [zero_half, qg], axis=0)
        return jnp.dot(kband, rhs, preferred_element_type=F32)

    def softmax(item, st):
        j, g, p = item
        h0 = g * Q_REP + p * HEADS_PER_ITEM
        s_prev = st[:BLK]
        if j == 0:
            s_prev = s_prev + no_prev
        live = jnp.where(cur, st[BLK:], s_prev)
        sink = sinks_ref[:, h0 * BLK:h0 * BLK + width]
        m = jnp.maximum(jnp.max(live, axis=0, keepdims=True), sink)
        pr = jnp.exp(live - m)
        p_sink = jnp.exp(sink - m)
        pcat = jnp.concatenate([jnp.where(cur, 0.0, pr), jnp.where(cur, pr, 0.0)], axis=0)
        return pcat.astype(BF16), p_sink

    def attend(item, pcat, p_sink):
        j, g, p = item
        h0 = g * Q_REP + p * HEADS_PER_ITEM
        ot = jnp.dot(vt_ref[g * VT_ROWS:(g + 1) * VT_ROWS, j * BLK:(j + 2) * BLK], pcat,
                     preferred_element_type=F32)
        inv = 1.0 / (ot[HEAD_DIM:HEAD_DIM + 1] + p_sink)
        o = (ot[:HEAD_DIM] * inv).astype(BF16)
        for e in range(HEADS_PER_ITEM):
            ybt_ref[(h0 + e) * HEAD_DIM:(h0 + e + 1) * HEAD_DIM, j * BLK:(j + 1) * BLK] = (
                o[:, e * BLK:(e + 1) * BLK])

    def gate_a(half_idx):
        rs = slice(half_idx * (TS_MIX // 2), (half_idx + 1) * (TS_MIX // 2))
        acc = lax.dot_general(yat_ref[:, rs], wa_ref[...], TN_DIMS, preferred_element_type=F32)
        mg_ref[rs, :] = mg_ref[rs, :] * acc

    n_items = len(items)
    fill = {n_items // 4 + ATT_LAG: 0, (3 * n_items) // 4 + ATT_LAG: 1}
    st, pc = {}, {}
    for step in range(n_items + 2 * ATT_LAG):
        if step < n_items:
            st[step] = scores(items[step])
        if step in fill:
            gate_a(fill[step])
        i = step - ATT_LAG
        if 0 <= i < n_items:
            pc[i] = softmax(items[i], st.pop(i))
        i = step - 2 * ATT_LAG
        if 0 <= i < n_items:
            attend(items[i], *pc.pop(i))

    kb_ref[0:BLK, :] = kb_ref[TS_MIX:TS_MIX + BLK, :]
    vt_ref[:, 0:BLK] = vt_ref[:, TS_MIX:TS_MIX + BLK]

    acc_b = lax.dot_general(ybt_ref[...], wb_ref[...], TN_DIMS, preferred_element_type=F32)
    merged = mg_ref[...] + sgb_ref[...].astype(F32) * acc_b
    y = jnp.dot(merged.astype(BF16), wo_ref[...], preferred_element_type=F32)
    o_ref[...] = x_ref[...] + mod_ref[2:3, :] * y


def _resident(shape):
    nd = len(shape)
    return pl.BlockSpec(shape, lambda *_: (0,) * nd, pipeline_mode=pl.Buffered(1))


def _mixer_call(x, mod, g_mix, w_t, b_t, w_n, b_n, ln_g, ln_b, ws_t, bs_row, sinks_row, wa, wb, wo):
    batch, seq, _ = x.shape
    tile = pl.BlockSpec((None, TS_MIX, D_MODEL), lambda b, t: (b, t, 0))
    return pl.pallas_call(
        _mixer_kernel,
        grid=(batch, seq // TS_MIX),
        in_specs=[
            tile,
            pl.BlockSpec((None, N_MOD, D_MODEL), lambda b, t: (b, 0, 0)),
            _resident((1, D_MODEL)),
            _resident((T_ROWS, D_MODEL)),
            _resident((T_ROWS, LANES)),
            _resident((D_MODEL, N_COLS)),
            _resident((1, N_COLS)),
            _resident((D_MODEL, LANES)),
            _resident((D_MODEL, LANES)),
            _resident((GM_GROUPS, BLK, BLK)),
            _resident((1, GM_GROUPS * BLK)),
            _resident((1, N_Q_HEADS * BLK)),
            _resident((D_MODEL, D_MODEL)),
            _resident((D_MODEL, D_MODEL)),
            _resident((D_MODEL, D_MODEL)),
        ],
        out_specs=tile,
        out_shape=jax.ShapeDtypeStruct(x.shape, F32),
        scratch_shapes=[
            pltpu.VMEM((TS_MIX, D_MODEL), BF16),
            pltpu.VMEM((D_MODEL, TS_MIX), F32),
            pltpu.VMEM((D_MODEL, TS_MIX), BF16),
            pltpu.VMEM((D_MODEL, TS_MIX), BF16),
            pltpu.VMEM((D_MODEL, TS_MIX), BF16),
            pltpu.VMEM((BLK + TS_MIX, KV_WIDTH), BF16),
            pltpu.VMEM((N_KV_HEADS * VT_ROWS, BLK + TS_MIX), BF16),
            pltpu.VMEM((D_MODEL, TS_MIX), BF16),
            pltpu.VMEM((TS_MIX, D_MODEL), F32),
            pltpu.VMEM((TS_MIX, D_MODEL), BF16),
        ],
        compiler_params=pltpu.CompilerParams(
            dimension_semantics=("arbitrary", "arbitrary"),
            vmem_limit_bytes=VMEM_LIMIT_BYTES),
        name="token_mixer",
    )(x, mod, g_mix, w_t, b_t, w_n, b_n, ln_g, ln_b, ws_t, bs_row, sinks_row, wa, wb, wo)


def _mlp_kernel(x_ref, mod_ref, gmlp_ref, wup_ref, wdn_ref, gfin_ref, o_ref, hb_ref, act_ref):
    x = x_ref[...]
    h = _rms_modulate(x, gmlp_ref[...], mod_ref[4:5, :], mod_ref[3:4, :])
    hb_ref[...] = h.astype(BF16)
    for c in range(D_FF // D_MODEL):
        cs = slice(c * D_MODEL, (c + 1) * D_MODEL)
        a = jnp.maximum(_dot(hb_ref[...], wup_ref[:, cs]), 0.0)
        act_ref[:, cs] = (a * a).astype(BF16)
    x2 = x_ref[...] + mod_ref[5:6, :] * _dot(act_ref[...], wdn_ref[...])
    ms = jnp.mean(x2 * x2, axis=-1, keepdims=True)
    o_ref[...] = x2 * lax.rsqrt(ms + EPS) * gfin_ref[...]


def _mlp_call(x, mod, g_mlp, w_up, w_down, g_final):
    batch, seq, _ = x.shape
    tile = pl.BlockSpec((None, TM_MLP, D_MODEL), lambda b, t: (b, t, 0))
    return pl.pallas_call(
        _mlp_kernel,
        grid=(batch, seq // TM_MLP),
        in_specs=[
            tile,
            pl.BlockSpec((None, N_MOD, D_MODEL), lambda b, t: (b, 0, 0)),
            _resident((1, D_MODEL)),
            _resident((D_MODEL, D_FF)),
            _resident((D_FF, D_MODEL)),
            _resident((1, D_MODEL)),
        ],
        out_specs=tile,
        out_shape=jax.ShapeDtypeStruct(x.shape, F32),
        scratch_shapes=[
            pltpu.VMEM((TM_MLP, D_MODEL), BF16),
            pltpu.VMEM((TM_MLP, D_FF), BF16),
        ],
        compiler_params=pltpu.CompilerParams(
            dimension_semantics=("arbitrary", "arbitrary"),
            vmem_limit_bytes=VMEM_LIMIT_BYTES),
        name="channel_mlp",
    )(x, mod, g_mlp, w_up, w_down, g_final)


def kernel(x, c, w_ada, b_ada, g_norm_mix, w_in, b_in, gm_ln_g, gm_ln_b, gm_ws, gm_bs, attn_sinks,
           w_branch_a, w_branch_b, w_out, g_norm_mlp, w_up, w_down, g_final):
    batch = x.shape[0]
    depth = w_in.shape[0]
    row = lambda v: v.reshape(1, -1)
    lane_col = lambda v: jnp.broadcast_to(v[:, None], (v.shape[0], LANES))
    for l in range(depth):
        mod = _ada_call(c, w_ada[l], row(b_ada[l])).reshape(batch, N_MOD, D_MODEL)
        wl = w_in[l].astype(BF16)
        bl = b_in[l]
        w_t = jnp.concatenate([wl[:, O_U:O_K], wl[:, O_VA:O_GA]], axis=1).T
        b_t = lane_col(jnp.concatenate([bl[O_U:O_K], bl[O_VA:O_GA]]))
        w_n = jnp.concatenate([wl[:, O_K:O_VA], wl[:, O_GA:]], axis=1)
        b_n = row(jnp.concatenate([bl[O_K:O_VA], bl[O_GA:]]))
        x = _mixer_call(
            x, mod, row(g_norm_mix[l]), w_t, b_t, w_n, b_n,
            lane_col(gm_ln_g[l]), lane_col(gm_ln_b[l]), jnp.swapaxes(gm_ws[l], 1, 2), row(gm_bs[l]),
            row(jnp.repeat(attn_sinks[l], BLK)),
            w_branch_a[l].astype(BF16), w_branch_b[l].astype(BF16), w_out[l].astype(BF16))
        assert depth == 1
        x = _mlp_call(x, mod, row(g_norm_mlp[l]), w_up[l].astype(BF16), w_down[l].astype(BF16),
                      row(g_final))
    return x
```

```python
import jax
import jax.numpy as jnp
from jax import lax
from jax.experimental import pallas as pl
from jax.experimental.pallas import tpu as pltpu

D_MODEL = 1024
BLK = 128
GM_GROUPS = 8
GM_GROUP_DIM = D_MODEL // GM_GROUPS
N_Q_HEADS = 16
N_KV_HEADS = 4
HEAD_DIM = 64
Q_REP = N_Q_HEADS // N_KV_HEADS
KV_WIDTH = N_KV_HEADS * HEAD_DIM
D_FF = 4 * D_MODEL
N_MOD = 6
EPS = 1e-6

O_U = 0
O_V = O_U + D_MODEL
O_Q = O_V + D_MODEL
O_K = O_Q + D_MODEL
O_VA = O_K + KV_WIDTH
O_GA = O_VA + KV_WIDTH
O_GB = O_GA + D_MODEL
IN_WIDTH = O_GB + D_MODEL

T_U = 0
T_V = T_U + D_MODEL
T_Q = T_V + D_MODEL
T_VA = T_Q + D_MODEL
T_ROWS = T_VA + KV_WIDTH

LANES = 128
BF16_ROWS = 16
HEADS_PER_ITEM = 4
VT_ROWS = HEAD_DIM + BF16_ROWS

ATT_LAG = 3
TS_MIX = 512
TM_MLP = 1024
VMEM_LIMIT_BYTES = 56 * 1024 * 1024

F32 = jnp.float32
BF16 = jnp.bfloat16
NT_DIMS = (((1,), (1,)), ((), ()))
TN_DIMS = (((0,), (0,)), ((), ()))

GELU_C0 = 0.7978845608028654
GELU_C1 = GELU_C0 * 0.044715


def _sigmoid(z):
    return 0.5 * jnp.tanh(0.5 * z) + 0.5


def _gelu_tanh(z):
    hz = 0.5 * z
    return hz + hz * jnp.tanh(z * (GELU_C0 + GELU_C1 * (z * z)))


def _dot(a, b):
    return jnp.dot(a, b, preferred_element_type=F32)


def _rms_modulate(x, g, scale, shift):
    ms = jnp.mean(x * x, axis=-1, keepdims=True)
    h = x * lax.rsqrt(ms + EPS) * g
    return h * (1.0 + scale) + shift


def _lane_tile(col, n):
    return jnp.concatenate([col] * n, axis=1)


def _ada_kernel(c_ref, w_ref, b_ref, o_ref):
    c = c_ref[...]
    o_ref[...] = jnp.dot(c * _sigmoid(c), w_ref[...], preferred_element_type=F32) + b_ref[...]


def _ada_call(c, w_ada, b_ada):
    batch = c.shape[0]
    return pl.pallas_call(
        _ada_kernel,
        grid=(N_MOD,),
        in_specs=[
            pl.BlockSpec((batch, D_MODEL), lambda n: (0, 0)),
            pl.BlockSpec((D_MODEL, D_MODEL), lambda n: (0, n)),
            pl.BlockSpec((1, D_MODEL), lambda n: (0, n)),
        ],
        out_specs=pl.BlockSpec((batch, D_MODEL), lambda n: (0, n)),
        out_shape=jax.ShapeDtypeStruct((batch, N_MOD * D_MODEL), F32),
        name="ada_mod",
    )(c, w_ada, b_ada)


def _mixer_kernel(x_ref, mod_ref, gmix_ref, win_ref, bt_ref, bin_ref, lng_ref, lnb_ref,
                  wst_ref, bs_ref, sinks_ref, wa_ref, wb_ref, wo_ref, wup_ref, wdn_ref,
                  o_ref, wupb_ref, wdnb_ref,
                  wt_ref, hb_ref, ut_ref, vnt_ref, yat_ref, qt_ref, kb_ref, vt_ref, ybt_ref, mg_ref, sgb_ref):
    t = pl.program_id(1)
    n_blk = TS_MIX // BLK

    @pl.when(t == 0)
    def _():
        kb_ref[0:BLK, :] = jnp.zeros((BLK, KV_WIDTH), BF16)
        r = lax.broadcasted_iota(jnp.int32, vt_ref.shape, 0)
        ones_row = r == HEAD_DIM
        for g in range(1, N_KV_HEADS):
            ones_row = ones_row | (r == g * VT_ROWS + HEAD_DIM)
        vt_ref[...] = jnp.where(ones_row, 1.0, 0.0).astype(BF16)

    @pl.when((pl.program_id(0) == 0) & (t == 0))
    def _():
        for lo in range(O_U, O_K, D_MODEL):
            wt_ref[T_U + lo:T_U + lo + D_MODEL, :] = win_ref[:, lo:lo + D_MODEL].T
        wt_ref[T_VA:T_ROWS, :] = win_ref[:, O_VA:O_GA].T

    wupb_ref[...] = wup_ref[...].astype(BF16)
    wdnb_ref[...] = wdn_ref[...].astype(BF16)

    x = x_ref[...]
    h = _rms_modulate(x, gmix_ref[...], mod_ref[1:2, :], mod_ref[0:1, :])
    hb_ref[...] = h.astype(BF16)

    def proj_t(lo, hi):
        z = lax.dot_general(wt_ref[lo:hi, :], hb_ref[...], NT_DIMS, preferred_element_type=F32)
        return z + _lane_tile(bt_ref[lo:hi, :], n_blk)

    def proj_n(lo, hi):
        z = jnp.dot(hb_ref[...], win_ref[:, lo:hi], preferred_element_type=F32)
        return z + bin_ref[:, lo:hi]

    half = D_MODEL // 2
    zv0 = proj_t(T_V, T_V + half)
    zv1 = proj_t(T_V + half, T_Q)
    zu0 = proj_t(T_U, T_U + half)
    zu1 = proj_t(T_U + half, T_V)
    gv0 = _gelu_tanh(zv0)
    zq0 = proj_t(T_Q, T_Q + half)
    gv1 = _gelu_tanh(zv1)
    zq1 = proj_t(T_Q + half, T_VA)
    gv = jnp.concatenate([gv0, gv1], axis=0)
    mu = jnp.mean(gv, axis=0, keepdims=True)
    gc = gv - mu
    var = jnp.mean(gc * gc, axis=0, keepdims=True)
    vn = gc * lax.rsqrt(var + EPS) * _lane_tile(lng_ref[...], n_blk) + _lane_tile(lnb_ref[...], n_blk)
    vnt_ref[...] = vn.astype(BF16)
    vat = proj_t(T_VA, T_ROWS)
    kb_ref[BLK:, :] = proj_n(O_K, O_VA).astype(BF16)
    ut_ref[:half, :] = _gelu_tanh(zu0)
    mg_ref[:, :half] = _sigmoid(proj_n(O_GA, O_GA + half))
    ut_ref[half:, :] = _gelu_tanh(zu1)
    mg_ref[:, half:] = _sigmoid(proj_n(O_GA + half, O_GB))
    sgb_ref[...] = _sigmoid(proj_n(O_GB, IN_WIDTH)).astype(BF16)
    qt_ref[:half, :] = (zq0 * (HEAD_DIM ** -0.5)).astype(BF16)
    qt_ref[half:, :] = (zq1 * (HEAD_DIM ** -0.5)).astype(BF16)
    for g in range(N_KV_HEADS):
        vt_ref[g * VT_ROWS:g * VT_ROWS + HEAD_DIM, BLK:] = (
            vat[g * HEAD_DIM:(g + 1) * HEAD_DIM].astype(BF16))

    src = lax.broadcasted_iota(jnp.int32, (BLK, BLK), 0)
    dst = lax.broadcasted_iota(jnp.int32, (BLK, BLK), 1)
    causal_t = src <= dst
    zero_blk = jnp.zeros((BLK, BLK), BF16)
    for pg in range(GM_GROUPS // 2):
        g0, g1 = 2 * pg, 2 * pg + 1
        r0 = slice(g0 * GM_GROUP_DIM, (g0 + 1) * GM_GROUP_DIM)
        r1 = slice(g1 * GM_GROUP_DIM, (g1 + 1) * GM_GROUP_DIM)
        w0 = jnp.where(causal_t, wst_ref[g0], 0.0).astype(BF16)
        w1 = jnp.where(causal_t, wst_ref[g1], 0.0).astype(BF16)
        bd = jnp.concatenate([jnp.concatenate([w0, zero_blk], axis=1),
                              jnp.concatenate([zero_blk, w1], axis=1)], axis=0)
        lhs = jnp.concatenate(
            [jnp.concatenate([vnt_ref[r0, j * BLK:(j + 1) * BLK], vnt_ref[r1, j * BLK:(j + 1) * BLK]],
                             axis=1) for j in range(n_blk)], axis=0)
        st = jnp.dot(lhs, bd, preferred_element_type=F32) + bs_ref[:, g0 * BLK:(g1 + 1) * BLK]
        for j in range(n_blk):
            cs = slice(j * BLK, (j + 1) * BLK)
            rs = slice(j * BLK, (j + 1) * BLK)
            yat_ref[r0, cs] = (ut_ref[r0, cs] * st[rs, :BLK]).astype(BF16)
            yat_ref[r1, cs] = (ut_ref[r1, cs] * st[rs, BLK:]).astype(BF16)

    width = HEADS_PER_ITEM * BLK
    key = lax.broadcasted_iota(jnp.int32, (BLK, width), 0)
    qry = lax.broadcasted_iota(jnp.int32, (BLK, width), 1) & (BLK - 1)
    cur = key <= qry
    no_prev = jnp.where(t == 0, -jnp.inf, 0.0).astype(F32)
    zero_half = jnp.zeros((HEAD_DIM, width), BF16)

    items = [(j, g, p) for j in range(n_blk) for g in range(N_KV_HEADS)
             for p in range(Q_REP // HEADS_PER_ITEM)]

    def scores(item):
        j, g, p = item
        h0 = g * Q_REP + p * HEADS_PER_ITEM
        cs = slice(j * BLK, (j + 1) * BLK)
        kband = kb_ref[j * BLK:(j + 2) * BLK, (g // 2) * LANES:(g // 2 + 1) * LANES]
        qg = jnp.concatenate(
            [qt_ref[(h0 + e) * HEAD_DIM:(h0 + e + 1) * HEAD_DIM, cs] for e in range(HEADS_PER_ITEM)],
            axis=1)
        rhs = jnp.concatenate([qg, zero_half] if g % 2 == 0 else [zero_half, qg], axis=0)
        return jnp.dot(kband, rhs, preferred_element_type=F32)

    def softmax(item, st):
        j, g, p = item
        h0 = g * Q_REP + p * HEADS_PER_ITEM
        s_prev = st[:BLK]
        if j == 0:
            s_prev = s_prev + no_prev
        live = jnp.where(cur, st[BLK:], s_prev)
        sink = sinks_ref[:, h0 * BLK:h0 * BLK + width]
        m = jnp.maximum(jnp.max(live, axis=0, keepdims=True), sink)
        pr = jnp.exp(live - m)
        p_sink = jnp.exp(sink - m)
        pcat = jnp.concatenate([jnp.where(cur, 0.0, pr), jnp.where(cur, pr, 0.0)], axis=0)
        return pcat.astype(BF16), p_sink

    def attend(item, pcat, p_sink):
        j, g, p = item
        h0 = g * Q_REP + p * HEADS_PER_ITEM
        ot = jnp.dot(vt_ref[g * VT_ROWS:(g + 1) * VT_ROWS, j * BLK:(j + 2) * BLK], pcat,
                     preferred_element_type=F32)
        inv = 1.0 / (ot[HEAD_DIM:HEAD_DIM + 1] + p_sink)
        o = (ot[:HEAD_DIM] * inv).astype(BF16)
        for e in range(HEADS_PER_ITEM):
            ybt_ref[(h0 + e) * HEAD_DIM:(h0 + e + 1) * HEAD_DIM, j * BLK:(j + 1) * BLK] = (
                o[:, e * BLK:(e + 1) * BLK])

    def gate_a(half_idx):
        rs = slice(half_idx * (TS_MIX // 2), (half_idx + 1) * (TS_MIX // 2))
        acc = lax.dot_general(yat_ref[:, rs], wa_ref[...], TN_DIMS, preferred_element_type=F32)
        mg_ref[rs, :] = mg_ref[rs, :] * acc

    n_items = len(items)
    fill = {n_items // 4 + ATT_LAG: 0, (3 * n_items) // 4 + ATT_LAG: 1}
    st, pc = {}, {}
    for step in range(n_items + 2 * ATT_LAG):
        if step < n_items:
            st[step] = scores(items[step])
        if step in fill:
            gate_a(fill[step])
        i = step - ATT_LAG
        if 0 <= i < n_items:
            pc[i] = softmax(items[i], st.pop(i))
        i = step - 2 * ATT_LAG
        if 0 <= i < n_items:
            attend(items[i], *pc.pop(i))

    kb_ref[0:BLK, :] = kb_ref[TS_MIX:TS_MIX + BLK, :]
    vt_ref[:, 0:BLK] = vt_ref[:, TS_MIX:TS_MIX + BLK]

    acc_b = lax.dot_general(ybt_ref[...], wb_ref[...], TN_DIMS, preferred_element_type=F32)
    merged = mg_ref[...] + sgb_ref[...].astype(F32) * acc_b
    y = jnp.dot(merged.astype(BF16), wo_ref[...], preferred_element_type=F32)
    o_ref[...] = x_ref[...] + mod_ref[2:3, :] * y


def _resident(shape):
    nd = len(shape)
    return pl.BlockSpec(shape, lambda *_: (0,) * nd, pipeline_mode=pl.Buffered(1))


def _mixer_call(x, mod, g_mix, w_in, b_t, b_in, ln_g, ln_b, ws_t, bs_row, sinks_row, wa, wb, wo,
                w_up, w_down):
    batch, seq, _ = x.shape
    n_tiles = seq // TS_MIX
    n_steps = batch * n_tiles
    tile = pl.BlockSpec((None, TS_MIX, D_MODEL), lambda b, t: (b, t, 0))
    up_rows = pl.BlockSpec((D_MODEL // n_steps, D_FF), lambda b, t: (b * n_tiles + t, 0))
    dn_rows = pl.BlockSpec((D_FF // n_steps, D_MODEL), lambda b, t: (b * n_tiles + t, 0))
    return pl.pallas_call(
        _mixer_kernel,
        grid=(batch, n_tiles),
        in_specs=[
            tile,
            pl.BlockSpec((None, N_MOD, D_MODEL), lambda b, t: (b, 0, 0)),
            _resident((1, D_MODEL)),
            _resident((D_MODEL, IN_WIDTH)),
            _resident((T_ROWS, LANES)),
            _resident((1, IN_WIDTH)),
            _resident((D_MODEL, LANES)),
            _resident((D_MODEL, LANES)),
            _resident((GM_GROUPS, BLK, BLK)),
            _resident((1, GM_GROUPS * BLK)),
            _resident((1, N_Q_HEADS * BLK)),
            _resident((D_MODEL, D_MODEL)),
            _resident((D_MODEL, D_MODEL)),
            _resident((D_MODEL, D_MODEL)),
            up_rows,
            dn_rows,
        ],
        out_specs=[tile, up_rows, dn_rows],
        out_shape=[jax.ShapeDtypeStruct(x.shape, F32),
                   jax.ShapeDtypeStruct(w_up.shape, BF16),
                   jax.ShapeDtypeStruct(w_down.shape, BF16)],
        scratch_shapes=[
            pltpu.VMEM((T_ROWS, D_MODEL), BF16),
            pltpu.VMEM((TS_MIX, D_MODEL), BF16),
            pltpu.VMEM((D_MODEL, TS_MIX), F32),
            pltpu.VMEM((D_MODEL, TS_MIX), BF16),
            pltpu.VMEM((D_MODEL, TS_MIX), BF16),
            pltpu.VMEM((D_MODEL, TS_MIX), BF16),
            pltpu.VMEM((BLK + TS_MIX, KV_WIDTH), BF16),
            pltpu.VMEM((N_KV_HEADS * VT_ROWS, BLK + TS_MIX), BF16),
            pltpu.VMEM((D_MODEL, TS_MIX), BF16),
            pltpu.VMEM((TS_MIX, D_MODEL), F32),
            pltpu.VMEM((TS_MIX, D_MODEL), BF16),
        ],
        compiler_params=pltpu.CompilerParams(
            dimension_semantics=("arbitrary", "arbitrary"),
            vmem_limit_bytes=VMEM_LIMIT_BYTES),
        name="token_mixer",
    )(x, mod, g_mix, w_in, b_t, b_in, ln_g, ln_b, ws_t, bs_row, sinks_row, wa, wb, wo, w_up, w_down)


def _mlp_kernel(x_ref, mod_ref, gmlp_ref, wup_ref, wdn_ref, gfin_ref, o_ref, hb_ref, act_ref):
    x = x_ref[...]
    h = _rms_modulate(x, gmlp_ref[...], mod_ref[4:5, :], mod_ref[3:4, :])
    hb_ref[...] = h.astype(BF16)
    for c in range(D_FF // D_MODEL):
        cs = slice(c * D_MODEL, (c + 1) * D_MODEL)
        a = jnp.maximum(_dot(hb_ref[...], wup_ref[:, cs]), 0.0)
        act_ref[:, cs] = (a * a).astype(BF16)
    x2 = x_ref[...] + mod_ref[5:6, :] * _dot(act_ref[...], wdn_ref[...])
    ms = jnp.mean(x2 * x2, axis=-1, keepdims=True)
    o_ref[...] = x2 * lax.rsqrt(ms + EPS) * gfin_ref[...]


def _mlp_call(x, mod, g_mlp, w_up, w_down, g_final):
    batch, seq, _ = x.shape
    tile = pl.BlockSpec((None, TM_MLP, D_MODEL), lambda b, t: (b, t, 0))
    return pl.pallas_call(
        _mlp_kernel,
        grid=(batch, seq // TM_MLP),
        in_specs=[
            tile,
            pl.BlockSpec((None, N_MOD, D_MODEL), lambda b, t: (b, 0, 0)),
            _resident((1, D_MODEL)),
            _resident((D_MODEL, D_FF)),
            _resident((D_FF, D_MODEL)),
            _resident((1, D_MODEL)),
        ],
        out_specs=tile,
        out_shape=jax.ShapeDtypeStruct(x.shape, F32),
        scratch_shapes=[
            pltpu.VMEM((TM_MLP, D_MODEL), BF16),
            pltpu.VMEM((TM_MLP, D_FF), BF16),
        ],
        compiler_params=pltpu.CompilerParams(
            dimension_semantics=("arbitrary", "arbitrary"),
            vmem_limit_bytes=VMEM_LIMIT_BYTES),
        name="channel_mlp",
    )(x, mod, g_mlp, w_up, w_down, g_final)


def kernel(x, c, w_ada, b_ada, g_norm_mix, w_in, b_in, gm_ln_g, gm_ln_b, gm_ws, gm_bs, attn_sinks,
           w_branch_a, w_branch_b, w_out, g_norm_mlp, w_up, w_down, g_final):
    batch = x.shape[0]
    depth = w_in.shape[0]
    row = lambda v: v.reshape(1, -1)
    lane_col = lambda v: jnp.broadcast_to(v[:, None], (v.shape[0], LANES))
    for l in range(depth):
        mod = _ada_call(c, w_ada[l], row(b_ada[l])).reshape(batch, N_MOD, D_MODEL)
        bl = b_in[l]
        b_t = lane_col(jnp.concatenate([bl[O_U:O_K], bl[O_VA:O_GA]]))
        x, w_up_b, w_down_b = _mixer_call(
            x, mod, row(g_norm_mix[l]), w_in[l].astype(BF16), b_t, row(bl),
            lane_col(gm_ln_g[l]), lane_col(gm_ln_b[l]), jnp.swapaxes(gm_ws[l], 1, 2), row(gm_bs[l]),
            row(jnp.repeat(attn_sinks[l], BLK)),
            w_branch_a[l].astype(BF16), w_branch_b[l].astype(BF16), w_out[l].astype(BF16),
            w_up[l], w_down[l])
        assert depth == 1
        x = _mlp_call(x, mod, row(g_norm_mlp[l]), w_up_b, w_down_b, row(g_final))
    return x
```

```python
import jax
import jax.numpy as jnp
from jax import lax
from jax.experimental import pallas as pl
from jax.experimental.pallas import tpu as pltpu

D_MODEL = 1024
BLK = 128
GM_GROUPS = 8
GM_GROUP_DIM = D_MODEL // GM_GROUPS
N_Q_HEADS = 16
N_KV_HEADS = 4
HEAD_DIM = 64
Q_REP = N_Q_HEADS // N_KV_HEADS
KV_WIDTH = N_KV_HEADS * HEAD_DIM
D_FF = 4 * D_MODEL
N_MOD = 6
EPS = 1e-6

O_U = 0
O_V = O_U + D_MODEL
O_Q = O_V + D_MODEL
O_K = O_Q + D_MODEL
O_VA = O_K + KV_WIDTH
O_GA = O_VA + KV_WIDTH
O_GB = O_GA + D_MODEL
IN_WIDTH = O_GB + D_MODEL

T_U = 0
T_V = T_U + D_MODEL
T_Q = T_V + D_MODEL
T_VA = T_Q + D_MODEL
T_ROWS = T_VA + KV_WIDTH

LANES = 128
BF16_ROWS = 16
HEADS_PER_ITEM = 4
VT_ROWS = HEAD_DIM + BF16_ROWS

ADA_K = 128
ATT_LAG = 3
TS_MIX = 512
TM_MLP = 1024
VMEM_LIMIT_BYTES = 56 * 1024 * 1024

F32 = jnp.float32
BF16 = jnp.bfloat16
NT_DIMS = (((1,), (1,)), ((), ()))
TN_DIMS = (((0,), (0,)), ((), ()))

GELU_C0 = 0.7978845608028654
GELU_C1 = GELU_C0 * 0.044715


def _sigmoid(z):
    return 0.5 * jnp.tanh(0.5 * z) + 0.5


def _gelu_tanh(z):
    hz = 0.5 * z
    return hz + hz * jnp.tanh(z * (GELU_C0 + GELU_C1 * (z * z)))


def _dot(a, b):
    return jnp.dot(a, b, preferred_element_type=F32)


def _rms_modulate(x, g, scale, shift):
    ms = jnp.mean(x * x, axis=-1, keepdims=True)
    return (x * lax.rsqrt(ms + EPS)) * (g * (1.0 + scale)) + shift


def _lane_tile(col, n):
    return jnp.concatenate([col] * n, axis=1)


def _ada_kernel(c_ref, w_ref, b_ref, o_ref):
    k = pl.program_id(0)

    @pl.when(k == 0)
    def _():
        o_ref[...] = jnp.broadcast_to(b_ref[...], o_ref.shape)

    c = c_ref[...]
    o_ref[...] += jnp.dot(c * _sigmoid(c), w_ref[...], preferred_element_type=F32)


def _ada_call(c, w_ada, b_ada):
    batch = c.shape[0]
    width = N_MOD * D_MODEL
    return pl.pallas_call(
        _ada_kernel,
        grid=(D_MODEL // ADA_K,),
        in_specs=[
            pl.BlockSpec((batch, ADA_K), lambda k: (0, k)),
            pl.BlockSpec((ADA_K, width), lambda k: (k, 0)),
            pl.BlockSpec((1, width), lambda k: (0, 0)),
        ],
        out_specs=pl.BlockSpec((batch, width), lambda k: (0, 0)),
        out_shape=jax.ShapeDtypeStruct((batch, width), F32),
        compiler_params=pltpu.CompilerParams(dimension_semantics=("arbitrary",)),
        name="ada_mod",
    )(c, w_ada, b_ada)


def _mixer_kernel(x_ref, mod_ref, gmix_ref, win_ref, bt_ref, bin_ref, lng_ref, lnb_ref,
                  wst_ref, bs_ref, sinks_ref, wa_ref, wb_ref, wo_ref, wup_ref, wdn_ref,
                  o_ref, wupb_ref, wdnb_ref,
                  wt_ref, hb_ref, ut_ref, vnt_ref, yat_ref, qt_ref, kb_ref, vt_ref, ybt_ref, mg_ref, sga_ref, sgb_ref):
    t = pl.program_id(1)
    n_blk = TS_MIX // BLK

    @pl.when(t == 0)
    def _():
        kb_ref[0:BLK, :] = jnp.zeros((BLK, KV_WIDTH), BF16)
        r = lax.broadcasted_iota(jnp.int32, vt_ref.shape, 0)
        ones_row = r == HEAD_DIM
        for g in range(1, N_KV_HEADS):
            ones_row = ones_row | (r == g * VT_ROWS + HEAD_DIM)
        vt_ref[...] = jnp.where(ones_row, 1.0, 0.0).astype(BF16)

    @pl.when((pl.program_id(0) == 0) & (t == 0))
    def _():
        for lo in range(O_U, O_K, D_MODEL):
            wt_ref[T_U + lo:T_U + lo + D_MODEL, :] = win_ref[:, lo:lo + D_MODEL].T
        wt_ref[T_VA:T_ROWS, :] = win_ref[:, O_VA:O_GA].T

    wupb_ref[...] = wup_ref[...].astype(BF16)
    wdnb_ref[...] = wdn_ref[...].astype(BF16)

    x = x_ref[...]
    h = _rms_modulate(x, gmix_ref[...], mod_ref[1:2, :], mod_ref[0:1, :])
    hb_ref[...] = h.astype(BF16)

    def proj_t(lo, hi):
        z = lax.dot_general(wt_ref[lo:hi, :], hb_ref[...], NT_DIMS, preferred_element_type=F32)
        return z + _lane_tile(bt_ref[lo:hi, :], n_blk)

    def proj_n(lo, hi):
        z = jnp.dot(hb_ref[...], win_ref[:, lo:hi], preferred_element_type=F32)
        return z + bin_ref[:, lo:hi]

    half = D_MODEL // 2
    zv0 = proj_t(T_V, T_V + half)
    zv1 = proj_t(T_V + half, T_Q)
    zu0 = proj_t(T_U, T_U + half)
    zu1 = proj_t(T_U + half, T_V)
    gv0 = _gelu_tanh(zv0)
    zq0 = proj_t(T_Q, T_Q + half)
    gv1 = _gelu_tanh(zv1)
    zq1 = proj_t(T_Q + half, T_VA)
    gv = jnp.concatenate([gv0, gv1], axis=0)
    mu = jnp.mean(gv, axis=0, keepdims=True)
    gc = gv - mu
    var = jnp.mean(gc * gc, axis=0, keepdims=True)
    vn = gc * lax.rsqrt(var + EPS) * _lane_tile(lng_ref[...], n_blk) + _lane_tile(lnb_ref[...], n_blk)
    vnt_ref[...] = vn.astype(BF16)
    vat = proj_t(T_VA, T_ROWS)
    kb_ref[BLK:, :] = proj_n(O_K, O_VA).astype(BF16)
    ut_ref[:half, :] = _gelu_tanh(zu0).astype(BF16)
    sga_ref[:, :half] = _sigmoid(proj_n(O_GA, O_GA + half)).astype(BF16)
    ut_ref[half:, :] = _gelu_tanh(zu1).astype(BF16)
    sga_ref[:, half:] = _sigmoid(proj_n(O_GA + half, O_GB)).astype(BF16)
    sgb_ref[...] = _sigmoid(proj_n(O_GB, IN_WIDTH)).astype(BF16)
    qt_ref[:half, :] = (zq0 * (HEAD_DIM ** -0.5)).astype(BF16)
    qt_ref[half:, :] = (zq1 * (HEAD_DIM ** -0.5)).astype(BF16)
    for g in range(N_KV_HEADS):
        vt_ref[g * VT_ROWS:g * VT_ROWS + HEAD_DIM, BLK:] = (
            vat[g * HEAD_DIM:(g + 1) * HEAD_DIM].astype(BF16))

    src = lax.broadcasted_iota(jnp.int32, (BLK, BLK), 0)
    dst = lax.broadcasted_iota(jnp.int32, (BLK, BLK), 1)
    causal_t = src <= dst
    zero_blk = jnp.zeros((BLK, BLK), BF16)
    for pg in range(GM_GROUPS // 2):
        g0, g1 = 2 * pg, 2 * pg + 1
        r0 = slice(g0 * GM_GROUP_DIM, (g0 + 1) * GM_GROUP_DIM)
        r1 = slice(g1 * GM_GROUP_DIM, (g1 + 1) * GM_GROUP_DIM)
        w0 = jnp.where(causal_t, wst_ref[g0], 0.0).astype(BF16)
        w1 = jnp.where(causal_t, wst_ref[g1], 0.0).astype(BF16)
        bd = jnp.concatenate([jnp.concatenate([w0, zero_blk], axis=1),
                              jnp.concatenate([zero_blk, w1], axis=1)], axis=0)
        lhs = jnp.concatenate(
            [jnp.concatenate([vnt_ref[r0, j * BLK:(j + 1) * BLK], vnt_ref[r1, j * BLK:(j + 1) * BLK]],
                             axis=1) for j in range(n_blk)], axis=0)
        st = jnp.dot(lhs, bd, preferred_element_type=F32) + bs_ref[:, g0 * BLK:(g1 + 1) * BLK]
        for j in range(n_blk):
            cs = slice(j * BLK, (j + 1) * BLK)
            rs = slice(j * BLK, (j + 1) * BLK)
            yat_ref[r0, cs] = (ut_ref[r0, cs].astype(F32) * st[rs, :BLK]).astype(BF16)
            yat_ref[r1, cs] = (ut_ref[r1, cs].astype(F32) * st[rs, BLK:]).astype(BF16)

    width = HEADS_PER_ITEM * BLK
    key = lax.broadcasted_iota(jnp.int32, (BLK, width), 0)
    qry = lax.broadcasted_iota(jnp.int32, (BLK, width), 1) & (BLK - 1)
    cur = key <= qry
    no_prev = jnp.where(t == 0, -jnp.inf, 0.0).astype(F32)
    zero_half = jnp.zeros((HEAD_DIM, width), BF16)

    items = [(j, g, p) for j in range(n_blk) for g in range(N_KV_HEADS)
             for p in range(Q_REP // HEADS_PER_ITEM)]

    def scores(item):
        j, g, p = item
        h0 = g * Q_REP + p * HEADS_PER_ITEM
        cs = slice(j * BLK, (j + 1) * BLK)
        kband = kb_ref[j * BLK:(j + 2) * BLK, (g // 2) * LANES:(g // 2 + 1) * LANES]
        qg = jnp.concatenate(
            [qt_ref[(h0 + e) * HEAD_DIM:(h0 + e + 1) * HEAD_DIM, cs] for e in range(HEADS_PER_ITEM)],
            axis=1)
        rhs = jnp.concatenate([qg, zero_half] if g % 2 == 0 else [zero_half, qg], axis=0)
        return jnp.dot(kband, rhs, preferred_element_type=F32)

    def softmax(item, st):
        j, g, p = item
        h0 = g * Q_REP + p * HEADS_PER_ITEM
        s_prev = st[:BLK]
        if j == 0:
            s_prev = s_prev + no_prev
        live = jnp.where(cur, st[BLK:], s_prev)
        sink = sinks_ref[:, h0 * BLK:h0 * BLK + width]
        m = jnp.maximum(jnp.max(live, axis=0, keepdims=True), sink)
        pr = jnp.exp(live - m)
        p_sink = jnp.exp(sink - m)
        pcat = jnp.concatenate([jnp.where(cur, 0.0, pr), jnp.where(cur, pr, 0.0)], axis=0)
        return pcat.astype(BF16), p_sink

    def attend(item, pcat, p_sink):
        j, g, p = item
        h0 = g * Q_REP + p * HEADS_PER_ITEM
        ot = jnp.dot(vt_ref[g * VT_ROWS:(g + 1) * VT_ROWS, j * BLK:(j + 2) * BLK], pcat,
                     preferred_element_type=F32)
        inv = 1.0 / (ot[HEAD_DIM:HEAD_DIM + 1] + p_sink)
        o = (ot[:HEAD_DIM] * inv).astype(BF16)
        for e in range(HEADS_PER_ITEM):
            ybt_ref[(h0 + e) * HEAD_DIM:(h0 + e + 1) * HEAD_DIM, j * BLK:(j + 1) * BLK] = (
                o[:, e * BLK:(e + 1) * BLK])

    def gate_a(half_idx):
        rs = slice(half_idx * (TS_MIX // 2), (half_idx + 1) * (TS_MIX // 2))
        acc = lax.dot_general(yat_ref[:, rs], wa_ref[...], TN_DIMS, preferred_element_type=F32)
        mg_ref[rs, :] = sga_ref[rs, :].astype(F32) * acc

    n_items = len(items)
    fill = {n_items // 4 + ATT_LAG: 0, (3 * n_items) // 4 + ATT_LAG: 1}
    st, pc = {}, {}
    for step in range(n_items + 2 * ATT_LAG):
        if step < n_items:
            st[step] = scores(items[step])
        if step in fill:
            gate_a(fill[step])
        i = step - ATT_LAG
        if 0 <= i < n_items:
            pc[i] = softmax(items[i], st.pop(i))
        i = step - 2 * ATT_LAG
        if 0 <= i < n_items:
            attend(items[i], *pc.pop(i))

    kb_ref[0:BLK, :] = kb_ref[TS_MIX:TS_MIX + BLK, :]
    vt_ref[:, 0:BLK] = vt_ref[:, TS_MIX:TS_MIX + BLK]

    acc_b = lax.dot_general(ybt_ref[...], wb_ref[...], TN_DIMS, preferred_element_type=F32)
    merged = mg_ref[...] + sgb_ref[...].astype(F32) * acc_b
    y = jnp.dot(merged.astype(BF16), wo_ref[...], preferred_element_type=F32)
    o_ref[...] = x_ref[...] + mod_ref[2:3, :] * y


def _resident(shape):
    nd = len(shape)
    return pl.BlockSpec(shape, lambda *_: (0,) * nd, pipeline_mode=pl.Buffered(1))


def _mixer_call(x, mod, g_mix, w_in, b_t, b_in, ln_g, ln_b, ws_t, bs_row, sinks_row, wa, wb, wo,
                w_up, w_down):
    batch, seq, _ = x.shape
    n_tiles = seq // TS_MIX
    n_steps = batch * n_tiles
    tile = pl.BlockSpec((None, TS_MIX, D_MODEL), lambda b, t: (b, t, 0))
    up_rows = pl.BlockSpec((D_MODEL // n_steps, D_FF), lambda b, t: (b * n_tiles + t, 0))
    dn_rows = pl.BlockSpec((D_FF // n_steps, D_MODEL), lambda b, t: (b * n_tiles + t, 0))
    return pl.pallas_call(
        _mixer_kernel,
        grid=(batch, n_tiles),
        in_specs=[
            tile,
            pl.BlockSpec((None, N_MOD, D_MODEL), lambda b, t: (b, 0, 0)),
            _resident((1, D_MODEL)),
            _resident((D_MODEL, IN_WIDTH)),
            _resident((T_ROWS, LANES)),
            _resident((1, IN_WIDTH)),
            _resident((D_MODEL, LANES)),
            _resident((D_MODEL, LANES)),
            _resident((GM_GROUPS, BLK, BLK)),
            _resident((1, GM_GROUPS * BLK)),
            _resident((1, N_Q_HEADS * BLK)),
            _resident((D_MODEL, D_MODEL)),
            _resident((D_MODEL, D_MODEL)),
            _resident((D_MODEL, D_MODEL)),
            up_rows,
            dn_rows,
        ],
        out_specs=[tile, up_rows, dn_rows],
        out_shape=[jax.ShapeDtypeStruct(x.shape, F32),
                   jax.ShapeDtypeStruct(w_up.shape, BF16),
                   jax.ShapeDtypeStruct(w_down.shape, BF16)],
        scratch_shapes=[
            pltpu.VMEM((T_ROWS, D_MODEL), BF16),
            pltpu.VMEM((TS_MIX, D_MODEL), BF16),
            pltpu.VMEM((D_MODEL, TS_MIX), BF16),
            pltpu.VMEM((D_MODEL, TS_MIX), BF16),
            pltpu.VMEM((D_MODEL, TS_MIX), BF16),
            pltpu.VMEM((D_MODEL, TS_MIX), BF16),
            pltpu.VMEM((BLK + TS_MIX, KV_WIDTH), BF16),
            pltpu.VMEM((N_KV_HEADS * VT_ROWS, BLK + TS_MIX), BF16),
            pltpu.VMEM((D_MODEL, TS_MIX), BF16),
            pltpu.VMEM((TS_MIX, D_MODEL), F32),
            pltpu.VMEM((TS_MIX, D_MODEL), BF16),
            pltpu.VMEM((TS_MIX, D_MODEL), BF16),
        ],
        compiler_params=pltpu.CompilerParams(
            dimension_semantics=("arbitrary", "arbitrary"),
            vmem_limit_bytes=VMEM_LIMIT_BYTES),
        name="token_mixer",
    )(x, mod, g_mix, w_in, b_t, b_in, ln_g, ln_b, ws_t, bs_row, sinks_row, wa, wb, wo, w_up, w_down)


def _mlp_kernel(x_ref, mod_ref, gmlp_ref, wup_ref, wdn_ref, gfin_ref, o_ref, hb_ref, act_ref):
    x = x_ref[...]
    h = _rms_modulate(x, gmlp_ref[...], mod_ref[4:5, :], mod_ref[3:4, :])
    hb_ref[...] = h.astype(BF16)
    for c in range(D_FF // D_MODEL):
        cs = slice(c * D_MODEL, (c + 1) * D_MODEL)
        a = jnp.maximum(_dot(hb_ref[...], wup_ref[:, cs]), 0.0)
        act_ref[:, cs] = (a * a).astype(BF16)
    x2 = x_ref[...] + mod_ref[5:6, :] * _dot(act_ref[...], wdn_ref[...])
    ms = jnp.mean(x2 * x2, axis=-1, keepdims=True)
    o_ref[...] = x2 * lax.rsqrt(ms + EPS) * gfin_ref[...]


def _mlp_call(x, mod, g_mlp, w_up, w_down, g_final):
    batch, seq, _ = x.shape
    tile = pl.BlockSpec((None, TM_MLP, D_MODEL), lambda b, t: (b, t, 0))
    return pl.pallas_call(
        _mlp_kernel,
        grid=(batch, seq // TM_MLP),
        in_specs=[
            tile,
            pl.BlockSpec((None, N_MOD, D_MODEL), lambda b, t: (b, 0, 0)),
            _resident((1, D_MODEL)),
            _resident((D_MODEL, D_FF)),
            _resident((D_FF, D_MODEL)),
            _resident((1, D_MODEL)),
        ],
        out_specs=tile,
        out_shape=jax.ShapeDtypeStruct(x.shape, F32),
        scratch_shapes=[
            pltpu.VMEM((TM_MLP, D_MODEL), BF16),
            pltpu.VMEM((TM_MLP, D_FF), BF16),
        ],
        compiler_params=pltpu.CompilerParams(
            dimension_semantics=("arbitrary", "arbitrary"),
            vmem_limit_bytes=VMEM_LIMIT_BYTES),
        name="channel_mlp",
    )(x, mod, g_mlp, w_up, w_down, g_final)


def kernel(x, c, w_ada, b_ada, g_norm_mix, w_in, b_in, gm_ln_g, gm_ln_b, gm_ws, gm_bs, attn_sinks,
           w_branch_a, w_branch_b, w_out, g_norm_mlp, w_up, w_down, g_final):
    batch = x.shape[0]
    depth = w_in.shape[0]
    row = lambda v: v.reshape(1, -1)
    lane_col = lambda v: jnp.broadcast_to(v[:, None], (v.shape[0], LANES))
    for l in range(depth):
        mod = _ada_call(c, w_ada[l], row(b_ada[l])).reshape(batch, N_MOD, D_MODEL)
        bl = b_in[l]
        b_t = lane_col(jnp.concatenate([bl[O_U:O_K], bl[O_VA:O_GA]]))
        x, w_up_b, w_down_b = _mixer_call(
            x, mod, row(g_norm_mix[l]), w_in[l].astype(BF16), b_t, row(bl),
            lane_col(gm_ln_g[l]), lane_col(gm_ln_b[l]), jnp.swapaxes(gm_ws[l], 1, 2), row(gm_bs[l]),
            row(jnp.repeat(attn_sinks[l], BLK)),
            w_branch_a[l].astype(BF16), w_branch_b[l].astype(BF16), w_out[l].astype(BF16),
            w_up[l], w_down[l])
        assert depth == 1
        x = _mlp_call(x, mod, row(g_norm_mlp[l]), w_up_b, w_down_b, row(g_final))
    return x
```

```python
import jax
import jax.numpy as jnp
from jax import lax
from jax.experimental import pallas as pl
from jax.experimental.pallas import tpu as pltpu

D_MODEL = 1024
BLK = 128
GM_GROUPS = 8
GM_GROUP_DIM = D_MODEL // GM_GROUPS
N_Q_HEADS = 16
N_KV_HEADS = 4
HEAD_DIM = 64
Q_REP = N_Q_HEADS // N_KV_HEADS
KV_WIDTH = N_KV_HEADS * HEAD_DIM
D_FF = 4 * D_MODEL
N_MOD = 6
EPS = 1e-6

O_U = 0
O_V = O_U + D_MODEL
O_Q = O_V + D_MODEL
O_K = O_Q + D_MODEL
O_VA = O_K + KV_WIDTH
O_GA = O_VA + KV_WIDTH
O_GB = O_GA + D_MODEL
IN_WIDTH = O_GB + D_MODEL

T_U = 0
T_V = T_U + D_MODEL
T_Q = T_V + D_MODEL
T_VA = T_Q + D_MODEL
T_ROWS = T_VA + KV_WIDTH

LANES = 128
BF16_ROWS = 16
HEADS_PER_ITEM = 4
VT_ROWS = HEAD_DIM + BF16_ROWS

ROW_GMIX, ROW_BIN, ROW_BS, ROW_SINK = 0, 1, 2, 3
ROWS_SUBLANES = 8
ROWS_WIDTH = 8192
ATT_LAG = 3
TS_MIX = 512
TM_MLP = 1024
VMEM_LIMIT_BYTES = 56 * 1024 * 1024

F32 = jnp.float32
BF16 = jnp.bfloat16
NT_DIMS = (((1,), (1,)), ((), ()))
TN_DIMS = (((0,), (0,)), ((), ()))

GELU_C0 = 0.7978845608028654
GELU_C1 = GELU_C0 * 0.044715


def _sigmoid(z):
    return 0.5 * jnp.tanh(0.5 * z) + 0.5


def _gelu_tanh(z):
    hz = 0.5 * z
    return hz + hz * jnp.tanh(z * (GELU_C0 + GELU_C1 * (z * z)))


def _dot(a, b):
    return jnp.dot(a, b, preferred_element_type=F32)


def _rms_modulate(x, g, scale, shift):
    ms = jnp.mean(x * x, axis=-1, keepdims=True)
    h = x * lax.rsqrt(ms + EPS) * g
    return h * (1.0 + scale) + shift


def _lane_tile(col, n):
    return jnp.concatenate([col] * n, axis=1)


def _ada_kernel(c_ref, w_ref, b_ref, o_ref):
    c = c_ref[...]
    o_ref[...] = jnp.dot(c * _sigmoid(c), w_ref[...], preferred_element_type=F32) + b_ref[...]


def _ada_call(c, w_ada, b_ada):
    batch = c.shape[0]
    return pl.pallas_call(
        _ada_kernel,
        grid=(N_MOD,),
        in_specs=[
            pl.BlockSpec((batch, D_MODEL), lambda n: (0, 0)),
            pl.BlockSpec((D_MODEL, D_MODEL), lambda n: (0, n)),
            pl.BlockSpec((1, D_MODEL), lambda n: (0, n)),
        ],
        out_specs=pl.BlockSpec((batch, D_MODEL), lambda n: (0, n)),
        out_shape=jax.ShapeDtypeStruct((batch, N_MOD * D_MODEL), F32),
        name="ada_mod",
    )(c, w_ada, b_ada)


def _mixer_kernel(x_ref, mod_ref, rows_ref, win_ref, bt_ref, lng_ref, lnb_ref,
                  wst_ref, wa_ref, wb_ref, wo_ref, wup_ref, wdn_ref,
                  o_ref, wupb_ref, wdnb_ref,
                  wt_ref, hb_ref, ut_ref, vnt_ref, yat_ref, qt_ref, kb_ref, vt_ref, ybt_ref, mg_ref, sgb_ref):
    t = pl.program_id(1)
    n_blk = TS_MIX // BLK

    @pl.when(t == 0)
    def _():
        kb_ref[0:BLK, :] = jnp.zeros((BLK, KV_WIDTH), BF16)
        r = lax.broadcasted_iota(jnp.int32, vt_ref.shape, 0)
        ones_row = r == HEAD_DIM
        for g in range(1, N_KV_HEADS):
            ones_row = ones_row | (r == g * VT_ROWS + HEAD_DIM)
        vt_ref[...] = jnp.where(ones_row, 1.0, 0.0).astype(BF16)

    @pl.when((pl.program_id(0) == 0) & (t == 0))
    def _():
        for lo in range(O_U, O_K, D_MODEL):
            wt_ref[T_U + lo:T_U + lo + D_MODEL, :] = win_ref[:, lo:lo + D_MODEL].T
        wt_ref[T_VA:T_ROWS, :] = win_ref[:, O_VA:O_GA].T

    wupb_ref[...] = wup_ref[...].astype(BF16)
    wdnb_ref[...] = wdn_ref[...].astype(BF16)

    x = x_ref[...]
    h = _rms_modulate(x, rows_ref[ROW_GMIX:ROW_GMIX + 1, :D_MODEL], mod_ref[1:2, :], mod_ref[0:1, :])
    hb_ref[...] = h.astype(BF16)

    def proj_t(lo, hi):
        z = lax.dot_general(wt_ref[lo:hi, :], hb_ref[...], NT_DIMS, preferred_element_type=F32)
        return z + _lane_tile(bt_ref[lo:hi, :], n_blk)

    def proj_n(lo, hi):
        z = jnp.dot(hb_ref[...], win_ref[:, lo:hi], preferred_element_type=F32)
        return z + rows_ref[ROW_BIN:ROW_BIN + 1, lo:hi]

    half = D_MODEL // 2
    zv0 = proj_t(T_V, T_V + half)
    zv1 = proj_t(T_V + half, T_Q)
    zu0 = proj_t(T_U, T_U + half)
    zu1 = proj_t(T_U + half, T_V)
    gv0 = _gelu_tanh(zv0)
    zq0 = proj_t(T_Q, T_Q + half)
    gv1 = _gelu_tanh(zv1)
    zq1 = proj_t(T_Q + half, T_VA)
    gv = jnp.concatenate([gv0, gv1], axis=0)
    mu = jnp.mean(gv, axis=0, keepdims=True)
    gc = gv - mu
    var = jnp.mean(gc * gc, axis=0, keepdims=True)
    vn = gc * lax.rsqrt(var + EPS) * _lane_tile(lng_ref[...], n_blk) + _lane_tile(lnb_ref[...], n_blk)
    vnt_ref[...] = vn.astype(BF16)
    vat = proj_t(T_VA, T_ROWS)
    kb_ref[BLK:, :] = proj_n(O_K, O_VA).astype(BF16)
    ut_ref[:half, :] = _gelu_tanh(zu0)
    mg_ref[:, :half] = _sigmoid(proj_n(O_GA, O_GA + half))
    ut_ref[half:, :] = _gelu_tanh(zu1)
    mg_ref[:, half:] = _sigmoid(proj_n(O_GA + half, O_GB))
    sgb_ref[...] = _sigmoid(proj_n(O_GB, IN_WIDTH)).astype(BF16)
    qt_ref[:half, :] = (zq0 * (HEAD_DIM ** -0.5)).astype(BF16)
    qt_ref[half:, :] = (zq1 * (HEAD_DIM ** -0.5)).astype(BF16)
    for g in range(N_KV_HEADS):
        vt_ref[g * VT_ROWS:g * VT_ROWS + HEAD_DIM, BLK:] = (
            vat[g * HEAD_DIM:(g + 1) * HEAD_DIM].astype(BF16))

    src = lax.broadcasted_iota(jnp.int32, (BLK, BLK), 0)
    dst = lax.broadcasted_iota(jnp.int32, (BLK, BLK), 1)
    causal_t = src <= dst
    zero_blk = jnp.zeros((BLK, BLK), BF16)
    for pg in range(GM_GROUPS // 2):
        g0, g1 = 2 * pg, 2 * pg + 1
        r0 = slice(g0 * GM_GROUP_DIM, (g0 + 1) * GM_GROUP_DIM)
        r1 = slice(g1 * GM_GROUP_DIM, (g1 + 1) * GM_GROUP_DIM)
        w0 = jnp.where(causal_t, wst_ref[g0], 0.0).astype(BF16)
        w1 = jnp.where(causal_t, wst_ref[g1], 0.0).astype(BF16)
        bd = jnp.concatenate([jnp.concatenate([w0, zero_blk], axis=1),
                              jnp.concatenate([zero_blk, w1], axis=1)], axis=0)
        lhs = jnp.concatenate(
            [jnp.concatenate([vnt_ref[r0, j * BLK:(j + 1) * BLK], vnt_ref[r1, j * BLK:(j + 1) * BLK]],
                             axis=1) for j in range(n_blk)], axis=0)
        st = jnp.dot(lhs, bd, preferred_element_type=F32) + rows_ref[ROW_BS:ROW_BS + 1, g0 * BLK:(g1 + 1) * BLK]
        for j in range(n_blk):
            cs = slice(j * BLK, (j + 1) * BLK)
            rs = slice(j * BLK, (j + 1) * BLK)
            yat_ref[r0, cs] = (ut_ref[r0, cs] * st[rs, :BLK]).astype(BF16)
            yat_ref[r1, cs] = (ut_ref[r1, cs] * st[rs, BLK:]).astype(BF16)

    width = HEADS_PER_ITEM * BLK
    key = lax.broadcasted_iota(jnp.int32, (BLK, width), 0)
    qry = lax.broadcasted_iota(jnp.int32, (BLK, width), 1) & (BLK - 1)
    cur = key <= qry
    no_prev = jnp.where(t == 0, -jnp.inf, 0.0).astype(F32)
    zero_half = jnp.zeros((HEAD_DIM, width), BF16)

    items = [(j, g, p) for j in range(n_blk) for g in range(N_KV_HEADS)
             for p in range(Q_REP // HEADS_PER_ITEM)]

    def scores(item):
        j, g, p = item
        h0 = g * Q_REP + p * HEADS_PER_ITEM
        cs = slice(j * BLK, (j + 1) * BLK)
        kband = kb_ref[j * BLK:(j + 2) * BLK, (g // 2) * LANES:(g // 2 + 1) * LANES]
        qg = jnp.concatenate(
            [qt_ref[(h0 + e) * HEAD_DIM:(h0 + e + 1) * HEAD_DIM, cs] for e in range(HEADS_PER_ITEM)],
            axis=1)
        rhs = jnp.concatenate([qg, zero_half] if g % 2 == 0 else [zero_half, qg], axis=0)
        return jnp.dot(kband, rhs, preferred_element_type=F32)

    def softmax(item, st):
        j, g, p = item
        h0 = g * Q_REP + p * HEADS_PER_ITEM
        s_prev = st[:BLK]
        if j == 0:
            s_prev = s_prev + no_prev
        live = jnp.where(cur, st[BLK:], s_prev)
        sink = rows_ref[ROW_SINK:ROW_SINK + 1, h0 * BLK:h0 * BLK + width]
        m = jnp.maximum(jnp.max(live, axis=0, keepdims=True), sink)
        pr = jnp.exp(live - m)
        p_sink = jnp.exp(sink - m)
        pcat = jnp.concatenate([jnp.where(cur, 0.0, pr), jnp.where(cur, pr, 0.0)], axis=0)
        return pcat.astype(BF16), p_sink

    def attend(item, pcat, p_sink):
        j, g, p = item
        h0 = g * Q_REP + p * HEADS_PER_ITEM
        ot = jnp.dot(vt_ref[g * VT_ROWS:(g + 1) * VT_ROWS, j * BLK:(j + 2) * BLK], pcat,
                     preferred_element_type=F32)
        inv = 1.0 / (ot[HEAD_DIM:HEAD_DIM + 1] + p_sink)
        o = (ot[:HEAD_DIM] * inv).astype(BF16)
        for e in range(HEADS_PER_ITEM):
            ybt_ref[(h0 + e) * HEAD_DIM:(h0 + e + 1) * HEAD_DIM, j * BLK:(j + 1) * BLK] = (
                o[:, e * BLK:(e + 1) * BLK])

    def gate_a(half_idx):
        rs = slice(half_idx * (TS_MIX // 2), (half_idx + 1) * (TS_MIX // 2))
        acc = lax.dot_general(yat_ref[:, rs], wa_ref[...], TN_DIMS, preferred_element_type=F32)
        mg_ref[rs, :] = mg_ref[rs, :] * acc

    n_items = len(items)
    fill = {n_items // 4 + ATT_LAG: 0, (3 * n_items) // 4 + ATT_LAG: 1}
    st, pc = {}, {}
    for step in range(n_items + 2 * ATT_LAG):
        if step < n_items:
            st[step] = scores(items[step])
        if step in fill:
            gate_a(fill[step])
        i = step - ATT_LAG
        if 0 <= i < n_items:
            pc[i] = softmax(items[i], st.pop(i))
        i = step - 2 * ATT_LAG
        if 0 <= i < n_items:
            attend(items[i], *pc.pop(i))

    kb_ref[0:BLK, :] = kb_ref[TS_MIX:TS_MIX + BLK, :]
    vt_ref[:, 0:BLK] = vt_ref[:, TS_MIX:TS_MIX + BLK]

    acc_b = lax.dot_general(ybt_ref[...], wb_ref[...], TN_DIMS, preferred_element_type=F32)
    merged = mg_ref[...] + sgb_ref[...].astype(F32) * acc_b
    y = jnp.dot(merged.astype(BF16), wo_ref[...], preferred_element_type=F32)
    o_ref[...] = x_ref[...] + mod_ref[2:3, :] * y


def _resident(shape):
    nd = len(shape)
    return pl.BlockSpec(shape, lambda *_: (0,) * nd, pipeline_mode=pl.Buffered(1))


def _mixer_call(x, mod, rows, w_in, b_t, ln_g, ln_b, ws_t, wa, wb, wo, w_up, w_down):
    batch, seq, _ = x.shape
    n_tiles = seq // TS_MIX
    n_steps = batch * n_tiles
    tile = pl.BlockSpec((None, TS_MIX, D_MODEL), lambda b, t: (b, t, 0))
    up_rows = pl.BlockSpec((D_MODEL // n_steps, D_FF), lambda b, t: (b * n_tiles + t, 0))
    dn_rows = pl.BlockSpec((D_FF // n_steps, D_MODEL), lambda b, t: (b * n_tiles + t, 0))
    return pl.pallas_call(
        _mixer_kernel,
        grid=(batch, n_tiles),
        in_specs=[
            tile,
            pl.BlockSpec((None, N_MOD, D_MODEL), lambda b, t: (b, 0, 0)),
            _resident((ROWS_SUBLANES, ROWS_WIDTH)),
            _resident((D_MODEL, IN_WIDTH)),
            _resident((T_ROWS, LANES)),
            _resident((D_MODEL, LANES)),
            _resident((D_MODEL, LANES)),
            _resident((GM_GROUPS, BLK, BLK)),
            _resident((D_MODEL, D_MODEL)),
            _resident((D_MODEL, D_MODEL)),
            _resident((D_MODEL, D_MODEL)),
            up_rows,
            dn_rows,
        ],
        out_specs=[tile, up_rows, dn_rows],
        out_shape=[jax.ShapeDtypeStruct(x.shape, F32),
                   jax.ShapeDtypeStruct(w_up.shape, BF16),
                   jax.ShapeDtypeStruct(w_down.shape, BF16)],
        scratch_shapes=[
            pltpu.VMEM((T_ROWS, D_MODEL), BF16),
            pltpu.VMEM((TS_MIX, D_MODEL), BF16),
            pltpu.VMEM((D_MODEL, TS_MIX), F32),
            pltpu.VMEM((D_MODEL, TS_MIX), BF16),
            pltpu.VMEM((D_MODEL, TS_MIX), BF16),
            pltpu.VMEM((D_MODEL, TS_MIX), BF16),
            pltpu.VMEM((BLK + TS_MIX, KV_WIDTH), BF16),
            pltpu.VMEM((N_KV_HEADS * VT_ROWS, BLK + TS_MIX), BF16),
            pltpu.VMEM((D_MODEL, TS_MIX), BF16),
            pltpu.VMEM((TS_MIX, D_MODEL), F32),
            pltpu.VMEM((TS_MIX, D_MODEL), BF16),
        ],
        compiler_params=pltpu.CompilerParams(
            dimension_semantics=("arbitrary", "arbitrary"),
            vmem_limit_bytes=VMEM_LIMIT_BYTES),
        name="token_mixer",
    )(x, mod, rows, w_in, b_t, ln_g, ln_b, ws_t, wa, wb, wo, w_up, w_down)


def _mlp_kernel(x_ref, mod_ref, gmlp_ref, wup_ref, wdn_ref, gfin_ref, o_ref, hb_ref, act_ref):
    x = x_ref[...]
    h = _rms_modulate(x, gmlp_ref[...], mod_ref[4:5, :], mod_ref[3:4, :])
    hb_ref[...] = h.astype(BF16)
    for c in range(D_FF // D_MODEL):
        cs = slice(c * D_MODEL, (c + 1) * D_MODEL)
        a = jnp.maximum(_dot(hb_ref[...], wup_ref[:, cs]), 0.0)
        act_ref[:, cs] = (a * a).astype(BF16)
    x2 = x_ref[...] + mod_ref[5:6, :] * _dot(act_ref[...], wdn_ref[...])
    ms = jnp.mean(x2 * x2, axis=-1, keepdims=True)
    o_ref[...] = x2 * lax.rsqrt(ms + EPS) * gfin_ref[...]


def _mlp_call(x, mod, g_mlp, w_up, w_down, g_final):
    batch, seq, _ = x.shape
    tile = pl.BlockSpec((None, TM_MLP, D_MODEL), lambda b, t: (b, t, 0))
    return pl.pallas_call(
        _mlp_kernel,
        grid=(batch, seq // TM_MLP),
        in_specs=[
            tile,
            pl.BlockSpec((None, N_MOD, D_MODEL), lambda b, t: (b, 0, 0)),
            _resident((1, D_MODEL)),
            _resident((D_MODEL, D_FF)),
            _resident((D_FF, D_MODEL)),
            _resident((1, D_MODEL)),
        ],
        out_specs=tile,
        out_shape=jax.ShapeDtypeStruct(x.shape, F32),
        scratch_shapes=[
            pltpu.VMEM((TM_MLP, D_MODEL), BF16),
            pltpu.VMEM((TM_MLP, D_FF), BF16),
        ],
        compiler_params=pltpu.CompilerParams(
            dimension_semantics=("arbitrary", "arbitrary"),
            vmem_limit_bytes=VMEM_LIMIT_BYTES),
        name="channel_mlp",
    )(x, mod, g_mlp, w_up, w_down, g_final)


def kernel(x, c, w_ada, b_ada, g_norm_mix, w_in, b_in, gm_ln_g, gm_ln_b, gm_ws, gm_bs, attn_sinks,
           w_branch_a, w_branch_b, w_out, g_norm_mlp, w_up, w_down, g_final):
    batch = x.shape[0]
    depth = w_in.shape[0]
    row = lambda v: v.reshape(1, -1)
    lane_col = lambda v: jnp.broadcast_to(v[:, None], (v.shape[0], LANES))
    for l in range(depth):
        mod = _ada_call(c, w_ada[l], row(b_ada[l])).reshape(batch, N_MOD, D_MODEL)
        bl = b_in[l]
        b_t = lane_col(jnp.concatenate([bl[O_U:O_K], bl[O_VA:O_GA]]))
        pad = lambda v: jnp.pad(v.reshape(-1), (0, ROWS_WIDTH - v.size))
        rows = jnp.zeros((ROWS_SUBLANES, ROWS_WIDTH), F32)
        rows = rows.at[ROW_GMIX].set(pad(g_norm_mix[l])).at[ROW_BIN].set(pad(bl))
        rows = rows.at[ROW_BS].set(pad(gm_bs[l])).at[ROW_SINK].set(pad(jnp.repeat(attn_sinks[l], BLK)))
        x, w_up_b, w_down_b = _mixer_call(
            x, mod, rows, w_in[l].astype(BF16), b_t,
            lane_col(gm_ln_g[l]), lane_col(gm_ln_b[l]), jnp.swapaxes(gm_ws[l], 1, 2),
            w_branch_a[l].astype(BF16), w_branch_b[l].astype(BF16), w_out[l].astype(BF16),
            w_up[l], w_down[l])
        assert depth == 1
        x = _mlp_call(x, mod, row(g_norm_mlp[l]), w_up_b, w_down_b, row(g_final))
    return x
```

```python
import jax
import jax.numpy as jnp
from jax import lax
from jax.experimental import pallas as pl
from jax.experimental.pallas import tpu as pltpu

D_MODEL = 1024
BLK = 128
GM_GROUPS = 8
GM_GROUP_DIM = D_MODEL // GM_GROUPS
N_Q_HEADS = 16
N_KV_HEADS = 4
HEAD_DIM = 64
Q_REP = N_Q_HEADS // N_KV_HEADS
KV_WIDTH = N_KV_HEADS * HEAD_DIM
D_FF = 4 * D_MODEL
N_MOD = 6
EPS = 1e-6

O_U = 0
O_V = O_U + D_MODEL
O_Q = O_V + D_MODEL
O_K = O_Q + D_MODEL
O_VA = O_K + KV_WIDTH
O_GA = O_VA + KV_WIDTH
O_GB = O_GA + D_MODEL
IN_WIDTH = O_GB + D_MODEL

T_U = 0
T_V = T_U + D_MODEL
T_Q = T_V + D_MODEL
T_VA = T_Q + D_MODEL
T_ROWS = T_VA + KV_WIDTH

LANES = 128
BF16_ROWS = 16
HEADS_PER_ITEM = 2
VT_ROWS = HEAD_DIM + BF16_ROWS

ROW_GMIX, ROW_BIN, ROW_BS, ROW_SINK = 0, 1, 2, 3
ROWS_SUBLANES = 8
ROWS_WIDTH = 8192
ATT_LAG = 6
TS_MIX = 512
TM_MLP = 1024
VMEM_LIMIT_BYTES = 56 * 1024 * 1024

F32 = jnp.float32
BF16 = jnp.bfloat16
NT_DIMS = (((1,), (1,)), ((), ()))
TN_DIMS = (((0,), (0,)), ((), ()))

GELU_C0 = 0.7978845608028654
GELU_C1 = GELU_C0 * 0.044715


def _sigmoid(z):
    return 0.5 * jnp.tanh(0.5 * z) + 0.5


def _gelu_tanh(z):
    hz = 0.5 * z
    return hz + hz * jnp.tanh(z * (GELU_C0 + GELU_C1 * (z * z)))


def _dot(a, b):
    return jnp.dot(a, b, preferred_element_type=F32)


def _rms_modulate(x, g, scale, shift):
    ms = jnp.mean(x * x, axis=-1, keepdims=True)
    h = x * lax.rsqrt(ms + EPS) * g
    return h * (1.0 + scale) + shift


def _lane_tile(col, n):
    return jnp.concatenate([col] * n, axis=1)


def _ada_kernel(c_ref, w_ref, b_ref, o_ref):
    c = c_ref[...]
    o_ref[...] = jnp.dot(c * _sigmoid(c), w_ref[...], preferred_element_type=F32) + b_ref[...]


def _ada_call(c, w_ada, b_ada):
    batch = c.shape[0]
    return pl.pallas_call(
        _ada_kernel,
        grid=(N_MOD,),
        in_specs=[
            pl.BlockSpec((batch, D_MODEL), lambda n: (0, 0)),
            pl.BlockSpec((D_MODEL, D_MODEL), lambda n: (0, n)),
            pl.BlockSpec((1, D_MODEL), lambda n: (0, n)),
        ],
        out_specs=pl.BlockSpec((batch, D_MODEL), lambda n: (0, n)),
        out_shape=jax.ShapeDtypeStruct((batch, N_MOD * D_MODEL), F32),
        name="ada_mod",
    )(c, w_ada, b_ada)


def _mixer_kernel(x_ref, mod_ref, rows_ref, win_ref, bt_ref, lng_ref, lnb_ref,
                  wst_ref, wa_ref, wb_ref, wo_ref, wup_ref, wdn_ref,
                  o_ref, wupb_ref, wdnb_ref,
                  wt_ref, hb_ref, ut_ref, vnt_ref, yat_ref, qt_ref, kb_ref, vt_ref, ybt_ref, mg_ref, sgb_ref):
    t = pl.program_id(1)
    n_blk = TS_MIX // BLK

    @pl.when(t == 0)
    def _():
        kb_ref[0:BLK, :] = jnp.zeros((BLK, KV_WIDTH), BF16)
        r = lax.broadcasted_iota(jnp.int32, vt_ref.shape, 0)
        ones_row = r == HEAD_DIM
        for g in range(1, N_KV_HEADS):
            ones_row = ones_row | (r == g * VT_ROWS + HEAD_DIM)
        vt_ref[...] = jnp.where(ones_row, 1.0, 0.0).astype(BF16)

    @pl.when((pl.program_id(0) == 0) & (t == 0))
    def _():
        for lo in range(O_U, O_K, D_MODEL):
            wt_ref[T_U + lo:T_U + lo + D_MODEL, :] = win_ref[:, lo:lo + D_MODEL].T
        wt_ref[T_VA:T_ROWS, :] = win_ref[:, O_VA:O_GA].T

    wupb_ref[...] = wup_ref[...].astype(BF16)
    wdnb_ref[...] = wdn_ref[...].astype(BF16)

    x = x_ref[...]
    h = _rms_modulate(x, rows_ref[ROW_GMIX:ROW_GMIX + 1, :D_MODEL], mod_ref[1:2, :], mod_ref[0:1, :])
    hb_ref[...] = h.astype(BF16)

    def proj_t(lo, hi):
        z = lax.dot_general(wt_ref[lo:hi, :], hb_ref[...], NT_DIMS, preferred_element_type=F32)
        return z + _lane_tile(bt_ref[lo:hi, :], n_blk)

    def proj_n(lo, hi):
        z = jnp.dot(hb_ref[...], win_ref[:, lo:hi], preferred_element_type=F32)
        return z + rows_ref[ROW_BIN:ROW_BIN + 1, lo:hi]

    half = D_MODEL // 2
    zv0 = proj_t(T_V, T_V + half)
    zv1 = proj_t(T_V + half, T_Q)
    zu0 = proj_t(T_U, T_U + half)
    zu1 = proj_t(T_U + half, T_V)
    gv0 = _gelu_tanh(zv0)
    zq0 = proj_t(T_Q, T_Q + half)
    gv1 = _gelu_tanh(zv1)
    zq1 = proj_t(T_Q + half, T_VA)
    gv = jnp.concatenate([gv0, gv1], axis=0)
    mu = jnp.mean(gv, axis=0, keepdims=True)
    gc = gv - mu
    var = jnp.mean(gc * gc, axis=0, keepdims=True)
    vn = gc * lax.rsqrt(var + EPS) * _lane_tile(lng_ref[...], n_blk) + _lane_tile(lnb_ref[...], n_blk)
    vnt_ref[...] = vn.astype(BF16)
    vat = proj_t(T_VA, T_ROWS)
    kb_ref[BLK:, :] = proj_n(O_K, O_VA).astype(BF16)
    ut_ref[:half, :] = _gelu_tanh(zu0)
    mg_ref[:, :half] = _sigmoid(proj_n(O_GA, O_GA + half))
    ut_ref[half:, :] = _gelu_tanh(zu1)
    mg_ref[:, half:] = _sigmoid(proj_n(O_GA + half, O_GB))
    sgb_ref[...] = _sigmoid(proj_n(O_GB, IN_WIDTH)).astype(BF16)
    qt_ref[:half, :] = (zq0 * (HEAD_DIM ** -0.5)).astype(BF16)
    qt_ref[half:, :] = (zq1 * (HEAD_DIM ** -0.5)).astype(BF16)
    for g in range(N_KV_HEADS):
        vt_ref[g * VT_ROWS:g * VT_ROWS + HEAD_DIM, BLK:] = (
            vat[g * HEAD_DIM:(g + 1) * HEAD_DIM].astype(BF16))

    src = lax.broadcasted_iota(jnp.int32, (BLK, BLK), 0)
    dst = lax.broadcasted_iota(jnp.int32, (BLK, BLK), 1)
    causal_t = src <= dst
    zero_blk = jnp.zeros((BLK, BLK), BF16)
    for pg in range(GM_GROUPS // 2):
        g0, g1 = 2 * pg, 2 * pg + 1
        r0 = slice(g0 * GM_GROUP_DIM, (g0 + 1) * GM_GROUP_DIM)
        r1 = slice(g1 * GM_GROUP_DIM, (g1 + 1) * GM_GROUP_DIM)
        w0 = jnp.where(causal_t, wst_ref[g0], 0.0).astype(BF16)
        w1 = jnp.where(causal_t, wst_ref[g1], 0.0).astype(BF16)
        bd = jnp.concatenate([jnp.concatenate([w0, zero_blk], axis=1),
                              jnp.concatenate([zero_blk, w1], axis=1)], axis=0)
        lhs = jnp.concatenate(
            [jnp.concatenate([vnt_ref[r0, j * BLK:(j + 1) * BLK], vnt_ref[r1, j * BLK:(j + 1) * BLK]],
                             axis=1) for j in range(n_blk)], axis=0)
        st = jnp.dot(lhs, bd, preferred_element_type=F32) + rows_ref[ROW_BS:ROW_BS + 1, g0 * BLK:(g1 + 1) * BLK]
        for j in range(n_blk):
            cs = slice(j * BLK, (j + 1) * BLK)
            rs = slice(j * BLK, (j + 1) * BLK)
            yat_ref[r0, cs] = (ut_ref[r0, cs] * st[rs, :BLK]).astype(BF16)
            yat_ref[r1, cs] = (ut_ref[r1, cs] * st[rs, BLK:]).astype(BF16)

    width = HEADS_PER_ITEM * BLK
    key = lax.broadcasted_iota(jnp.int32, (BLK, width), 0)
    qry = lax.broadcasted_iota(jnp.int32, (BLK, width), 1) & (BLK - 1)
    cur = key <= qry
    no_prev = jnp.where(t == 0, -jnp.inf, 0.0).astype(F32)
    zero_half = jnp.zeros((HEAD_DIM, width), BF16)

    items = [(j, g, p) for j in range(n_blk) for g in range(N_KV_HEADS)
             for p in range(Q_REP // HEADS_PER_ITEM)]

    def scores(item):
        j, g, p = item
        h0 = g * Q_REP + p * HEADS_PER_ITEM
        cs = slice(j * BLK, (j + 1) * BLK)
        kband = kb_ref[j * BLK:(j + 2) * BLK, (g // 2) * LANES:(g // 2 + 1) * LANES]
        qg = jnp.concatenate(
            [qt_ref[(h0 + e) * HEAD_DIM:(h0 + e + 1) * HEAD_DIM, cs] for e in range(HEADS_PER_ITEM)],
            axis=1)
        rhs = jnp.concatenate([qg, zero_half] if g % 2 == 0 else [zero_half, qg], axis=0)
        return jnp.dot(kband, rhs, preferred_element_type=F32)

    def softmax(item, st):
        j, g, p = item
        h0 = g * Q_REP + p * HEADS_PER_ITEM
        s_prev = st[:BLK]
        if j == 0:
            s_prev = s_prev + no_prev
        live = jnp.where(cur, st[BLK:], s_prev)
        sink = rows_ref[ROW_SINK:ROW_SINK + 1, h0 * BLK:h0 * BLK + width]
        m = jnp.maximum(jnp.max(live, axis=0, keepdims=True), sink)
        pr = jnp.exp(live - m)
        p_sink = jnp.exp(sink - m)
        pcat = jnp.concatenate([jnp.where(cur, 0.0, pr), jnp.where(cur, pr, 0.0)], axis=0)
        return pcat.astype(BF16), p_sink

    def attend(item, pcat, p_sink):
        j, g, p = item
        h0 = g * Q_REP + p * HEADS_PER_ITEM
        ot = jnp.dot(vt_ref[g * VT_ROWS:(g + 1) * VT_ROWS, j * BLK:(j + 2) * BLK], pcat,
                     preferred_element_type=F32)
        inv = 1.0 / (ot[HEAD_DIM:HEAD_DIM + 1] + p_sink)
        o = (ot[:HEAD_DIM] * inv).astype(BF16)
        for e in range(HEADS_PER_ITEM):
            ybt_ref[(h0 + e) * HEAD_DIM:(h0 + e + 1) * HEAD_DIM, j * BLK:(j + 1) * BLK] = (
                o[:, e * BLK:(e + 1) * BLK])

    def gate_a(half_idx):
        rs = slice(half_idx * (TS_MIX // 2), (half_idx + 1) * (TS_MIX // 2))
        acc = lax.dot_general(yat_ref[:, rs], wa_ref[...], TN_DIMS, preferred_element_type=F32)
        mg_ref[rs, :] = mg_ref[rs, :] * acc

    n_items = len(items)
    fill = {n_items // 4 + ATT_LAG: 0, (3 * n_items) // 4 + ATT_LAG: 1}
    st, pc = {}, {}
    for step in range(n_items + 2 * ATT_LAG):
        if step < n_items:
            st[step] = scores(items[step])
        if step in fill:
            gate_a(fill[step])
        i = step - ATT_LAG
        if 0 <= i < n_items:
            pc[i] = softmax(items[i], st.pop(i))
        i = step - 2 * ATT_LAG
        if 0 <= i < n_items:
            attend(items[i], *pc.pop(i))

    kb_ref[0:BLK, :] = kb_ref[TS_MIX:TS_MIX + BLK, :]
    vt_ref[:, 0:BLK] = vt_ref[:, TS_MIX:TS_MIX + BLK]

    acc_b = lax.dot_general(ybt_ref[...], wb_ref[...], TN_DIMS, preferred_element_type=F32)
    merged = mg_ref[...] + sgb_ref[...].astype(F32) * acc_b
    y = jnp.dot(merged.astype(BF16), wo_ref[...], preferred_element_type=F32)
    o_ref[...] = x_ref[...] + mod_ref[2:3, :] * y


def _resident(shape):
    nd = len(shape)
    return pl.BlockSpec(shape, lambda *_: (0,) * nd, pipeline_mode=pl.Buffered(1))


def _mixer_call(x, mod, rows, w_in, b_t, ln_g, ln_b, ws_t, wa, wb, wo, w_up, w_down):
    batch, seq, _ = x.shape
    n_tiles = seq // TS_MIX
    n_steps = batch * n_tiles
    tile = pl.BlockSpec((None, TS_MIX, D_MODEL), lambda b, t: (b, t, 0))
    up_rows = pl.BlockSpec((D_MODEL // n_steps, D_FF), lambda b, t: (b * n_tiles + t, 0))
    dn_rows = pl.BlockSpec((D_FF // n_steps, D_MODEL), lambda b, t: (b * n_tiles + t, 0))
    return pl.pallas_call(
        _mixer_kernel,
        grid=(batch, n_tiles),
        in_specs=[
            tile,
            pl.BlockSpec((None, N_MOD, D_MODEL), lambda b, t: (b, 0, 0)),
            _resident((ROWS_SUBLANES, ROWS_WIDTH)),
            _resident((D_MODEL, IN_WIDTH)),
            _resident((T_ROWS, LANES)),
            _resident((D_MODEL, LANES)),
            _resident((D_MODEL, LANES)),
            _resident((GM_GROUPS, BLK, BLK)),
            _resident((D_MODEL, D_MODEL)),
            _resident((D_MODEL, D_MODEL)),
            _resident((D_MODEL, D_MODEL)),
            up_rows,
            dn_rows,
        ],
        out_specs=[tile, up_rows, dn_rows],
        out_shape=[jax.ShapeDtypeStruct(x.shape, F32),
                   jax.ShapeDtypeStruct(w_up.shape, BF16),
                   jax.ShapeDtypeStruct(w_down.shape, BF16)],
        scratch_shapes=[
            pltpu.VMEM((T_ROWS, D_MODEL), BF16),
            pltpu.VMEM((TS_MIX, D_MODEL), BF16),
            pltpu.VMEM((D_MODEL, TS_MIX), F32),
            pltpu.VMEM((D_MODEL, TS_MIX), BF16),
            pltpu.VMEM((D_MODEL, TS_MIX), BF16),
            pltpu.VMEM((D_MODEL, TS_MIX), BF16),
            pltpu.VMEM((BLK + TS_MIX, KV_WIDTH), BF16),
            pltpu.VMEM((N_KV_HEADS * VT_ROWS, BLK + TS_MIX), BF16),
            pltpu.VMEM((D_MODEL, TS_MIX), BF16),
            pltpu.VMEM((TS_MIX, D_MODEL), F32),
            pltpu.VMEM((TS_MIX, D_MODEL), BF16),
        ],
        compiler_params=pltpu.CompilerParams(
            dimension_semantics=("arbitrary", "arbitrary"),
            vmem_limit_bytes=VMEM_LIMIT_BYTES),
        name="token_mixer",
    )(x, mod, rows, w_in, b_t, ln_g, ln_b, ws_t, wa, wb, wo, w_up, w_down)


def _mlp_kernel(x_ref, mod_ref, gmlp_ref, wup_ref, wdn_ref, gfin_ref, o_ref, hb_ref, act_ref):
    x = x_ref[...]
    h = _rms_modulate(x, gmlp_ref[...], mod_ref[4:5, :], mod_ref[3:4, :])
    hb_ref[...] = h.astype(BF16)
    for c in range(D_FF // D_MODEL):
        cs = slice(c * D_MODEL, (c + 1) * D_MODEL)
        a = jnp.maximum(_dot(hb_ref[...], wup_ref[:, cs]), 0.0)
        act_ref[:, cs] = (a * a).astype(BF16)
    x2 = x_ref[...] + mod_ref[5:6, :] * _dot(act_ref[...], wdn_ref[...])
    ms = jnp.mean(x2 * x2, axis=-1, keepdims=True)
    o_ref[...] = x2 * lax.rsqrt(ms + EPS) * gfin_ref[...]


def _mlp_call(x, mod, g_mlp, w_up, w_down, g_final):
    batch, seq, _ = x.shape
    tile = pl.BlockSpec((None, TM_MLP, D_MODEL), lambda b, t: (b, t, 0))
    return pl.pallas_call(
        _mlp_kernel,
        grid=(batch, seq // TM_MLP),
        in_specs=[
            tile,
            pl.BlockSpec((None, N_MOD, D_MODEL), lambda b, t: (b, 0, 0)),
            _resident((1, D_MODEL)),
            _resident((D_MODEL, D_FF)),
            _resident((D_FF, D_MODEL)),
            _resident((1, D_MODEL)),
        ],
        out_specs=tile,
        out_shape=jax.ShapeDtypeStruct(x.shape, F32),
        scratch_shapes=[
            pltpu.VMEM((TM_MLP, D_MODEL), BF16),
            pltpu.VMEM((TM_MLP, D_FF), BF16),
        ],
        compiler_params=pltpu.CompilerParams(
            dimension_semantics=("arbitrary", "arbitrary"),
            vmem_limit_bytes=VMEM_LIMIT_BYTES),
        name="channel_mlp",
    )(x, mod, g_mlp, w_up, w_down, g_final)


def kernel(x, c, w_ada, b_ada, g_norm_mix, w_in, b_in, gm_ln_g, gm_ln_b, gm_ws, gm_bs, attn_sinks,
           w_branch_a, w_branch_b, w_out, g_norm_mlp, w_up, w_down, g_final):
    batch = x.shape[0]
    depth = w_in.shape[0]
    row = lambda v: v.reshape(1, -1)
    lane_col = lambda v: jnp.broadcast_to(v[:, None], (v.shape[0], LANES))
    for l in range(depth):
        mod = _ada_call(c, w_ada[l], row(b_ada[l])).reshape(batch, N_MOD, D_MODEL)
        bl = b_in[l]
        b_t = lane_col(jnp.concatenate([bl[O_U:O_K], bl[O_VA:O_GA]]))
        pad = lambda v: jnp.pad(v.reshape(-1), (0, ROWS_WIDTH - v.size))
        rows = jnp.zeros((ROWS_SUBLANES, ROWS_WIDTH), F32)
        rows = rows.at[ROW_GMIX].set(pad(g_norm_mix[l])).at[ROW_BIN].set(pad(bl))
        rows = rows.at[ROW_BS].set(pad(gm_bs[l])).at[ROW_SINK].set(pad(jnp.repeat(attn_sinks[l], BLK)))
        x, w_up_b, w_down_b = _mixer_call(
            x, mod, rows, w_in[l].astype(BF16), b_t,
            lane_col(gm_ln_g[l]), lane_col(gm_ln_b[l]), jnp.swapaxes(gm_ws[l], 1, 2),
            w_branch_a[l].astype(BF16), w_branch_b[l].astype(BF16), w_out[l].astype(BF16),
            w_up[l], w_down[l])
        assert depth == 1
        x = _mlp_call(x, mod, row(g_norm_mlp[l]), w_up_b, w_down_b, row(g_final))
    return x
```

```python
import jax
import jax.numpy as jnp
from jax import lax
from jax.experimental import pallas as pl
from jax.experimental.pallas import tpu as pltpu

D_MODEL = 1024
BLK = 128
GM_GROUPS = 8
GM_GROUP_DIM = D_MODEL // GM_GROUPS
N_Q_HEADS = 16
N_KV_HEADS = 4
HEAD_DIM = 64
Q_REP = N_Q_HEADS // N_KV_HEADS
KV_WIDTH = N_KV_HEADS * HEAD_DIM
D_FF = 4 * D_MODEL
N_MOD = 6
EPS = 1e-6

O_U = 0
O_V = O_U + D_MODEL
O_Q = O_V + D_MODEL
O_K = O_Q + D_MODEL
O_VA = O_K + KV_WIDTH
O_GA = O_VA + KV_WIDTH
O_GB = O_GA + D_MODEL
IN_WIDTH = O_GB + D_MODEL

T_U = 0
T_V = T_U + D_MODEL
T_Q = T_V + D_MODEL
T_VA = T_Q + D_MODEL
T_ROWS = T_VA + KV_WIDTH

LANES = 128
BF16_ROWS = 16
HEADS_PER_ITEM = 2
VT_ROWS = HEAD_DIM + BF16_ROWS

ROW_GMIX, ROW_BIN, ROW_BS, ROW_SINK = 0, 1, 2, 3
ROWS_SUBLANES = 8
ROWS_WIDTH = 8192
ATT_LAG = 6
TS_MIX = 512
TM_MLP = 1024
VMEM_LIMIT_BYTES = 56 * 1024 * 1024

F32 = jnp.float32
BF16 = jnp.bfloat16
NT_DIMS = (((1,), (1,)), ((), ()))
TN_DIMS = (((0,), (0,)), ((), ()))

GELU_C0 = 0.7978845608028654
GELU_C1 = GELU_C0 * 0.044715


def _sigmoid(z):
    return 0.5 * jnp.tanh(0.5 * z) + 0.5


def _gelu_tanh(z):
    hz = 0.5 * z
    return hz + hz * jnp.tanh(z * (GELU_C0 + GELU_C1 * (z * z)))


def _dot(a, b):
    return jnp.dot(a, b, preferred_element_type=F32)


def _rms_modulate(x, g, scale, shift):
    ms = jnp.mean(x * x, axis=-1, keepdims=True)
    h = x * lax.rsqrt(ms + EPS) * g
    return h * (1.0 + scale) + shift


def _lane_tile(col, n):
    return jnp.concatenate([col] * n, axis=1)


def _ada_kernel(c_ref, w_ref, b_ref, o_ref):
    c = c_ref[...]
    o_ref[...] = jnp.dot(c * _sigmoid(c), w_ref[...], preferred_element_type=F32) + b_ref[...]


def _ada_call(c, w_ada, b_ada):
    batch = c.shape[0]
    return pl.pallas_call(
        _ada_kernel,
        grid=(N_MOD,),
        in_specs=[
            pl.BlockSpec((batch, D_MODEL), lambda n: (0, 0)),
            pl.BlockSpec((D_MODEL, D_MODEL), lambda n: (0, n)),
            pl.BlockSpec((1, D_MODEL), lambda n: (0, n)),
        ],
        out_specs=pl.BlockSpec((batch, D_MODEL), lambda n: (0, n)),
        out_shape=jax.ShapeDtypeStruct((batch, N_MOD * D_MODEL), F32),
        name="ada_mod",
    )(c, w_ada, b_ada)


def _mixer_kernel(x_ref, mod_ref, rows_ref, win_ref, bt_ref, lng_ref, lnb_ref,
                  wst_ref, wa_ref, wb_ref, wo_ref, wup_ref, wdn_ref,
                  o_ref, wupb_ref, wdnb_ref,
                  wt_ref, hb_ref, ut_ref, vnt_ref, yat_ref, qt_ref, kb_ref, vt_ref, ybt_ref, mg_ref, sgb_ref):
    t = pl.program_id(1)
    n_blk = TS_MIX // BLK

    @pl.when(t == 0)
    def _():
        kb_ref[0:BLK, :] = jnp.zeros((BLK, KV_WIDTH), BF16)
        r = lax.broadcasted_iota(jnp.int32, vt_ref.shape, 0)
        ones_row = r == HEAD_DIM
        for g in range(1, N_KV_HEADS):
            ones_row = ones_row | (r == g * VT_ROWS + HEAD_DIM)
        vt_ref[...] = jnp.where(ones_row, 1.0, 0.0).astype(BF16)

    @pl.when((pl.program_id(0) == 0) & (t == 0))
    def _():
        for lo in range(O_U, O_K, D_MODEL):
            wt_ref[T_U + lo:T_U + lo + D_MODEL, :] = win_ref[:, lo:lo + D_MODEL].T
        wt_ref[T_VA:T_ROWS, :] = win_ref[:, O_VA:O_GA].T

    wupb_ref[...] = wup_ref[...].astype(BF16)
    wdnb_ref[...] = wdn_ref[...].astype(BF16)

    x = x_ref[...]
    h = _rms_modulate(x, rows_ref[ROW_GMIX:ROW_GMIX + 1, :D_MODEL], mod_ref[1:2, :], mod_ref[0:1, :])
    hb_ref[...] = h.astype(BF16)

    def proj_t(lo, hi):
        z = lax.dot_general(wt_ref[lo:hi, :], hb_ref[...], NT_DIMS, preferred_element_type=F32)
        return z + _lane_tile(bt_ref[lo:hi, :], n_blk)

    def proj_n(lo, hi):
        z = jnp.dot(hb_ref[...], win_ref[:, lo:hi], preferred_element_type=F32)
        return z + rows_ref[ROW_BIN:ROW_BIN + 1, lo:hi]

    half = D_MODEL // 2
    zv0 = proj_t(T_V, T_V + half)
    zv1 = proj_t(T_V + half, T_Q)
    zu0 = proj_t(T_U, T_U + half)
    zu1 = proj_t(T_U + half, T_V)
    gv0 = _gelu_tanh(zv0)
    zq0 = proj_t(T_Q, T_Q + half)
    gv1 = _gelu_tanh(zv1)
    zq1 = proj_t(T_Q + half, T_VA)
    gv = jnp.concatenate([gv0, gv1], axis=0)
    mu = jnp.mean(gv, axis=0, keepdims=True)
    gc = gv - mu
    var = jnp.mean(gc * gc, axis=0, keepdims=True)
    rstd = lax.rsqrt(var + EPS)
    vn = (gv * rstd - mu * rstd) * _lane_tile(lng_ref[...], n_blk) + _lane_tile(lnb_ref[...], n_blk)
    vnt_ref[...] = vn.astype(BF16)
    vat = proj_t(T_VA, T_ROWS)
    kb_ref[BLK:, :] = proj_n(O_K, O_VA).astype(BF16)
    ut_ref[:half, :] = _gelu_tanh(zu0)
    mg_ref[:, :half] = _sigmoid(proj_n(O_GA, O_GA + half))
    ut_ref[half:, :] = _gelu_tanh(zu1)
    mg_ref[:, half:] = _sigmoid(proj_n(O_GA + half, O_GB))
    sgb_ref[...] = _sigmoid(proj_n(O_GB, IN_WIDTH)).astype(BF16)
    qt_ref[:half, :] = (zq0 * (HEAD_DIM ** -0.5)).astype(BF16)
    qt_ref[half:, :] = (zq1 * (HEAD_DIM ** -0.5)).astype(BF16)
    for g in range(N_KV_HEADS):
        vt_ref[g * VT_ROWS:g * VT_ROWS + HEAD_DIM, BLK:] = (
            vat[g * HEAD_DIM:(g + 1) * HEAD_DIM].astype(BF16))

    src = lax.broadcasted_iota(jnp.int32, (BLK, BLK), 0)
    dst = lax.broadcasted_iota(jnp.int32, (BLK, BLK), 1)
    causal_t = src <= dst
    zero_blk = jnp.zeros((BLK, BLK), BF16)
    for pg in range(GM_GROUPS // 2):
        g0, g1 = 2 * pg, 2 * pg + 1
        r0 = slice(g0 * GM_GROUP_DIM, (g0 + 1) * GM_GROUP_DIM)
        r1 = slice(g1 * GM_GROUP_DIM, (g1 + 1) * GM_GROUP_DIM)
        w0 = jnp.where(causal_t, wst_ref[g0], 0.0).astype(BF16)
        w1 = jnp.where(causal_t, wst_ref[g1], 0.0).astype(BF16)
        bd = jnp.concatenate([jnp.concatenate([w0, zero_blk], axis=1),
                              jnp.concatenate([zero_blk, w1], axis=1)], axis=0)
        lhs = jnp.concatenate(
            [jnp.concatenate([vnt_ref[r0, j * BLK:(j + 1) * BLK], vnt_ref[r1, j * BLK:(j + 1) * BLK]],
                             axis=1) for j in range(n_blk)], axis=0)
        st = jnp.dot(lhs, bd, preferred_element_type=F32) + rows_ref[ROW_BS:ROW_BS + 1, g0 * BLK:(g1 + 1) * BLK]
        for j in range(n_blk):
            cs = slice(j * BLK, (j + 1) * BLK)
            rs = slice(j * BLK, (j + 1) * BLK)
            yat_ref[r0, cs] = (ut_ref[r0, cs] * st[rs, :BLK]).astype(BF16)
            yat_ref[r1, cs] = (ut_ref[r1, cs] * st[rs, BLK:]).astype(BF16)

    width = HEADS_PER_ITEM * BLK
    key = lax.broadcasted_iota(jnp.int32, (BLK, width), 0)
    qry = lax.broadcasted_iota(jnp.int32, (BLK, width), 1) & (BLK - 1)
    cur = key <= qry
    no_prev = jnp.where(t == 0, -jnp.inf, 0.0).astype(F32)
    zero_half = jnp.zeros((HEAD_DIM, width), BF16)

    items = [(j, g, p) for j in range(n_blk) for g in range(N_KV_HEADS)
             for p in range(Q_REP // HEADS_PER_ITEM)]

    def scores(item):
        j, g, p = item
        h0 = g * Q_REP + p * HEADS_PER_ITEM
        cs = slice(j * BLK, (j + 1) * BLK)
        kband = kb_ref[j * BLK:(j + 2) * BLK, (g // 2) * LANES:(g // 2 + 1) * LANES]
        qg = jnp.concatenate(
            [qt_ref[(h0 + e) * HEAD_DIM:(h0 + e + 1) * HEAD_DIM, cs] for e in range(HEADS_PER_ITEM)],
            axis=1)
        rhs = jnp.concatenate([qg, zero_half] if g % 2 == 0 else [zero_half, qg], axis=0)
        return jnp.dot(kband, rhs, preferred_element_type=F32)

    def softmax(item, st):
        j, g, p = item
        h0 = g * Q_REP + p * HEADS_PER_ITEM
        s_prev = st[:BLK]
        if j == 0:
            s_prev = s_prev + no_prev
        live = jnp.where(cur, st[BLK:], s_prev)
        sink = rows_ref[ROW_SINK:ROW_SINK + 1, h0 * BLK:h0 * BLK + width]
        m = jnp.maximum(jnp.max(live, axis=0, keepdims=True), sink)
        pr = jnp.exp(live - m)
        p_sink = jnp.exp(sink - m)
        pcat = jnp.concatenate([jnp.where(cur, 0.0, pr), jnp.where(cur, pr, 0.0)], axis=0)
        return pcat.astype(BF16), p_sink

    def attend(item, pcat, p_sink):
        j, g, p = item
        h0 = g * Q_REP + p * HEADS_PER_ITEM
        ot = jnp.dot(vt_ref[g * VT_ROWS:(g + 1) * VT_ROWS, j * BLK:(j + 2) * BLK], pcat,
                     preferred_element_type=F32)
        inv = 1.0 / (ot[HEAD_DIM:HEAD_DIM + 1] + p_sink)
        o = (ot[:HEAD_DIM] * inv).astype(BF16)
        for e in range(HEADS_PER_ITEM):
            ybt_ref[(h0 + e) * HEAD_DIM:(h0 + e + 1) * HEAD_DIM, j * BLK:(j + 1) * BLK] = (
                o[:, e * BLK:(e + 1) * BLK])

    def gate_a(half_idx):
        rs = slice(half_idx * (TS_MIX // 2), (half_idx + 1) * (TS_MIX // 2))
        acc = lax.dot_general(yat_ref[:, rs], wa_ref[...], TN_DIMS, preferred_element_type=F32)
        mg_ref[rs, :] = mg_ref[rs, :] * acc

    n_items = len(items)
    fill = {n_items // 4 + ATT_LAG: 0, (3 * n_items) // 4 + ATT_LAG: 1}
    st, pc = {}, {}
    for step in range(n_items + 2 * ATT_LAG):
        if step < n_items:
            st[step] = scores(items[step])
        if step in fill:
            gate_a(fill[step])
        i = step - ATT_LAG
        if 0 <= i < n_items:
            pc[i] = softmax(items[i], st.pop(i))
        i = step - 2 * ATT_LAG
        if 0 <= i < n_items:
            attend(items[i], *pc.pop(i))

    kb_ref[0:BLK, :] = kb_ref[TS_MIX:TS_MIX + BLK, :]
    vt_ref[:, 0:BLK] = vt_ref[:, TS_MIX:TS_MIX + BLK]

    acc_b = lax.dot_general(ybt_ref[...], wb_ref[...], TN_DIMS, preferred_element_type=F32)
    merged = mg_ref[...] + sgb_ref[...].astype(F32) * acc_b
    y = jnp.dot(merged.astype(BF16), wo_ref[...], preferred_element_type=F32)
    o_ref[...] = x_ref[...] + mod_ref[2:3, :] * y


def _resident(shape):
    nd = len(shape)
    return pl.BlockSpec(shape, lambda *_: (0,) * nd, pipeline_mode=pl.Buffered(1))


def _mixer_call(x, mod, rows, w_in, b_t, ln_g, ln_b, ws_t, wa, wb, wo, w_up, w_down):
    batch, seq, _ = x.shape
    n_tiles = seq // TS_MIX
    n_steps = batch * n_tiles
    tile = pl.BlockSpec((None, TS_MIX, D_MODEL), lambda b, t: (b, t, 0))
    up_rows = pl.BlockSpec((D_MODEL // n_steps, D_FF), lambda b, t: (b * n_tiles + t, 0))
    dn_rows = pl.BlockSpec((D_FF // n_steps, D_MODEL), lambda b, t: (b * n_tiles + t, 0))
    return pl.pallas_call(
        _mixer_kernel,
        grid=(batch, n_tiles),
        in_specs=[
            tile,
            pl.BlockSpec((None, N_MOD, D_MODEL), lambda b, t: (b, 0, 0)),
            _resident((ROWS_SUBLANES, ROWS_WIDTH)),
            _resident((D_MODEL, IN_WIDTH)),
            _resident((T_ROWS, LANES)),
            _resident((D_MODEL, LANES)),
            _resident((D_MODEL, LANES)),
            _resident((GM_GROUPS, BLK, BLK)),
            _resident((D_MODEL, D_MODEL)),
            _resident((D_MODEL, D_MODEL)),
            _resident((D_MODEL, D_MODEL)),
            up_rows,
            dn_rows,
        ],
        out_specs=[tile, up_rows, dn_rows],
        out_shape=[jax.ShapeDtypeStruct(x.shape, F32),
                   jax.ShapeDtypeStruct(w_up.shape, BF16),
                   jax.ShapeDtypeStruct(w_down.shape, BF16)],
        scratch_shapes=[
            pltpu.VMEM((T_ROWS, D_MODEL), BF16),
            pltpu.VMEM((TS_MIX, D_MODEL), BF16),
            pltpu.VMEM((D_MODEL, TS_MIX), F32),
            pltpu.VMEM((D_MODEL, TS_MIX), BF16),
            pltpu.VMEM((D_MODEL, TS_MIX), BF16),
            pltpu.VMEM((D_MODEL, TS_MIX), BF16),
            pltpu.VMEM((BLK + TS_MIX, KV_WIDTH), BF16),
            pltpu.VMEM((N_KV_HEADS * VT_ROWS, BLK + TS_MIX), BF16),
            pltpu.VMEM((D_MODEL, TS_MIX), BF16),
            pltpu.VMEM((TS_MIX, D_MODEL), F32),
            pltpu.VMEM((TS_MIX, D_MODEL), BF16),
        ],
        compiler_params=pltpu.CompilerParams(
            dimension_semantics=("arbitrary", "arbitrary"),
            vmem_limit_bytes=VMEM_LIMIT_BYTES),
        name="token_mixer",
    )(x, mod, rows, w_in, b_t, ln_g, ln_b, ws_t, wa, wb, wo, w_up, w_down)


def _mlp_kernel(x_ref, mod_ref, gmlp_ref, wup_ref, wdn_ref, gfin_ref, o_ref, hb_ref, act_ref):
    x = x_ref[...]
    h = _rms_modulate(x, gmlp_ref[...], mod_ref[4:5, :], mod_ref[3:4, :])
    hb_ref[...] = h.astype(BF16)
    for c in range(D_FF // D_MODEL):
        cs = slice(c * D_MODEL, (c + 1) * D_MODEL)
        a = jnp.maximum(_dot(hb_ref[...], wup_ref[:, cs]), 0.0)
        act_ref[:, cs] = (a * a).astype(BF16)
    x2 = x_ref[...] + mod_ref[5:6, :] * _dot(act_ref[...], wdn_ref[...])
    ms = jnp.mean(x2 * x2, axis=-1, keepdims=True)
    o_ref[...] = x2 * lax.rsqrt(ms + EPS) * gfin_ref[...]


def _mlp_call(x, mod, g_mlp, w_up, w_down, g_final):
    batch, seq, _ = x.shape
    tile = pl.BlockSpec((None, TM_MLP, D_MODEL), lambda b, t: (b, t, 0))
    return pl.pallas_call(
        _mlp_kernel,
        grid=(batch, seq // TM_MLP),
        in_specs=[
            tile,
            pl.BlockSpec((None, N_MOD, D_MODEL), lambda b, t: (b, 0, 0)),
            _resident((1, D_MODEL)),
            _resident((D_MODEL, D_FF)),
            _resident((D_FF, D_MODEL)),
            _resident((1, D_MODEL)),
        ],
        out_specs=tile,
        out_shape=jax.ShapeDtypeStruct(x.shape, F32),
        scratch_shapes=[
            pltpu.VMEM((TM_MLP, D_MODEL), BF16),
            pltpu.VMEM((TM_MLP, D_FF), BF16),
        ],
        compiler_params=pltpu.CompilerParams(
            dimension_semantics=("arbitrary", "arbitrary"),
            vmem_limit_bytes=VMEM_LIMIT_BYTES),
        name="channel_mlp",
    )(x, mod, g_mlp, w_up, w_down, g_final)


def kernel(x, c, w_ada, b_ada, g_norm_mix, w_in, b_in, gm_ln_g, gm_ln_b, gm_ws, gm_bs, attn_sinks,
           w_branch_a, w_branch_b, w_out, g_norm_mlp, w_up, w_down, g_final):
    batch = x.shape[0]
    depth = w_in.shape[0]
    row = lambda v: v.reshape(1, -1)
    lane_col = lambda v: jnp.broadcast_to(v[:, None], (v.shape[0], LANES))
    for l in range(depth):
        mod = _ada_call(c, w_ada[l], row(b_ada[l])).reshape(batch, N_MOD, D_MODEL)
        bl = b_in[l]
        b_t = lane_col(jnp.concatenate([bl[O_U:O_K], bl[O_VA:O_GA]]))
        pad = lambda v: jnp.pad(v.reshape(-1), (0, ROWS_WIDTH - v.size))
        rows = jnp.zeros((ROWS_SUBLANES, ROWS_WIDTH), F32)
        rows = rows.at[ROW_GMIX].set(pad(g_norm_mix[l])).at[ROW_BIN].set(pad(bl))
        rows = rows.at[ROW_BS].set(pad(gm_bs[l])).at[ROW_SINK].set(pad(jnp.repeat(attn_sinks[l], BLK)))
        x, w_up_b, w_down_b = _mixer_call(
            x, mod, rows, w_in[l].astype(BF16), b_t,
            lane_col(gm_ln_g[l]), lane_col(gm_ln_b[l]), jnp.swapaxes(gm_ws[l], 1, 2),
            w_branch_a[l].astype(BF16), w_branch_b[l].astype(BF16), w_out[l].astype(BF16),
            w_up[l], w_down[l])
        assert depth == 1
        x = _mlp_call(x, mod, row(g_norm_mlp[l]), w_up_b, w_down_b, row(g_final))
    return x
```

```python
import jax
import jax.numpy as jnp
from jax import lax
from jax.experimental import pallas as pl
from jax.experimental.pallas import tpu as pltpu

D_MODEL = 1024
BLK = 128
GM_GROUPS = 8
GM_GROUP_DIM = D_MODEL // GM_GROUPS
N_Q_HEADS = 16
N_KV_HEADS = 4
HEAD_DIM = 64
Q_REP = N_Q_HEADS // N_KV_HEADS
KV_WIDTH = N_KV_HEADS * HEAD_DIM
D_FF = 4 * D_MODEL
N_MOD = 6
EPS = 1e-6

O_U = 0
O_V = O_U + D_MODEL
O_Q = O_V + D_MODEL
O_K = O_Q + D_MODEL
O_VA = O_K + KV_WIDTH
O_GA = O_VA + KV_WIDTH
O_GB = O_GA + D_MODEL
IN_WIDTH = O_GB + D_MODEL

T_U = 0
T_V = T_U + D_MODEL
T_Q = T_V + D_MODEL
T_VA = T_Q + D_MODEL
T_ROWS = T_VA + KV_WIDTH

LANES = 128
BF16_ROWS = 16
HEADS_PER_ITEM = 2
VT_ROWS = HEAD_DIM + BF16_ROWS

ROW_GMIX, ROW_BIN, ROW_BS, ROW_SINK = 0, 1, 2, 3
ROWS_SUBLANES = 8
ROWS_WIDTH = 8192
ATT_LAG = 6
TS_MIX = 512
TM_MLP = 1024
VMEM_LIMIT_BYTES = 56 * 1024 * 1024

F32 = jnp.float32
BF16 = jnp.bfloat16
NT_DIMS = (((1,), (1,)), ((), ()))
TN_DIMS = (((0,), (0,)), ((), ()))

GELU_C0 = 0.7978845608028654
GELU_C1 = GELU_C0 * 0.044715


def _sigmoid(z):
    return 0.5 * jnp.tanh(0.5 * z) + 0.5


def _gelu_tanh(z):
    hz = 0.5 * z
    return hz + hz * jnp.tanh(z * (GELU_C0 + GELU_C1 * (z * z)))


def _dot(a, b):
    return jnp.dot(a, b, preferred_element_type=F32)


def _rms_modulate(x, g, scale, shift):
    ms = jnp.mean(x * x, axis=-1, keepdims=True)
    h = x * lax.rsqrt(ms + EPS) * g
    return h * (1.0 + scale) + shift


def _lane_tile(col, n):
    return jnp.concatenate([col] * n, axis=1)


def _ada_kernel(c_ref, w_ref, b_ref, o_ref):
    c = c_ref[...]
    o_ref[...] = jnp.dot(c * _sigmoid(c), w_ref[...], preferred_element_type=F32) + b_ref[...]


def _ada_call(c, w_ada, b_ada):
    batch = c.shape[0]
    return pl.pallas_call(
        _ada_kernel,
        grid=(N_MOD,),
        in_specs=[
            pl.BlockSpec((batch, D_MODEL), lambda n: (0, 0)),
            pl.BlockSpec((D_MODEL, D_MODEL), lambda n: (0, n)),
            pl.BlockSpec((1, D_MODEL), lambda n: (0, n)),
        ],
        out_specs=pl.BlockSpec((batch, D_MODEL), lambda n: (0, n)),
        out_shape=jax.ShapeDtypeStruct((batch, N_MOD * D_MODEL), F32),
        name="ada_mod",
    )(c, w_ada, b_ada)


def _mixer_kernel(x_ref, mod_ref, rows_ref, win_ref, bt_ref, lng_ref, lnb_ref,
                  wst_ref, wa_ref, wb_ref, wo_ref, wup_ref, wdn_ref,
                  o_ref, wupb_ref, wdnb_ref,
                  wt_ref, hb_ref, hbt_ref, ut_ref, vnt_ref, yat_ref, qt_ref, kb_ref, vt_ref, ybt_ref, mg_ref, sgb_ref):
    t = pl.program_id(1)
    n_blk = TS_MIX // BLK

    @pl.when(t == 0)
    def _():
        kb_ref[0:BLK, :] = jnp.zeros((BLK, KV_WIDTH), BF16)
        r = lax.broadcasted_iota(jnp.int32, vt_ref.shape, 0)
        ones_row = r == HEAD_DIM
        for g in range(1, N_KV_HEADS):
            ones_row = ones_row | (r == g * VT_ROWS + HEAD_DIM)
        vt_ref[...] = jnp.where(ones_row, 1.0, 0.0).astype(BF16)

    @pl.when((pl.program_id(0) == 0) & (t == 0))
    def _():
        for lo in range(O_U, O_K, D_MODEL):
            wt_ref[T_U + lo:T_U + lo + D_MODEL, :] = win_ref[:, lo:lo + D_MODEL].T
        wt_ref[T_VA:T_ROWS, :] = win_ref[:, O_VA:O_GA].T

    wupb_ref[...] = wup_ref[...].astype(BF16)
    wdnb_ref[...] = wdn_ref[...].astype(BF16)

    x = x_ref[...]
    h = _rms_modulate(x, rows_ref[ROW_GMIX:ROW_GMIX + 1, :D_MODEL], mod_ref[1:2, :], mod_ref[0:1, :])
    hb_ref[...] = h.astype(BF16)
    hbt_ref[...] = h.astype(BF16).T

    def proj_t(lo, hi):
        z = jnp.dot(wt_ref[lo:hi, :], hbt_ref[...], preferred_element_type=F32)
        return z + _lane_tile(bt_ref[lo:hi, :], n_blk)

    def proj_n(lo, hi):
        z = jnp.dot(hb_ref[...], win_ref[:, lo:hi], preferred_element_type=F32)
        return z + rows_ref[ROW_BIN:ROW_BIN + 1, lo:hi]

    half = D_MODEL // 2
    zv0 = proj_t(T_V, T_V + half)
    zv1 = proj_t(T_V + half, T_Q)
    zu0 = proj_t(T_U, T_U + half)
    zu1 = proj_t(T_U + half, T_V)
    gv0 = _gelu_tanh(zv0)
    zq0 = proj_t(T_Q, T_Q + half)
    gv1 = _gelu_tanh(zv1)
    zq1 = proj_t(T_Q + half, T_VA)
    gv = jnp.concatenate([gv0, gv1], axis=0)
    mu = jnp.mean(gv, axis=0, keepdims=True)
    gc = gv - mu
    var = jnp.mean(gc * gc, axis=0, keepdims=True)
    vn = gc * lax.rsqrt(var + EPS) * _lane_tile(lng_ref[...], n_blk) + _lane_tile(lnb_ref[...], n_blk)
    vnt_ref[...] = vn.astype(BF16)
    vat = proj_t(T_VA, T_ROWS)
    kb_ref[BLK:, :] = proj_n(O_K, O_VA).astype(BF16)
    ut_ref[:half, :] = _gelu_tanh(zu0)
    mg_ref[:, :half] = _sigmoid(proj_n(O_GA, O_GA + half))
    ut_ref[half:, :] = _gelu_tanh(zu1)
    mg_ref[:, half:] = _sigmoid(proj_n(O_GA + half, O_GB))
    sgb_ref[...] = _sigmoid(proj_n(O_GB, IN_WIDTH)).astype(BF16)
    qt_ref[:half, :] = (zq0 * (HEAD_DIM ** -0.5)).astype(BF16)
    qt_ref[half:, :] = (zq1 * (HEAD_DIM ** -0.5)).astype(BF16)
    for g in range(N_KV_HEADS):
        vt_ref[g * VT_ROWS:g * VT_ROWS + HEAD_DIM, BLK:] = (
            vat[g * HEAD_DIM:(g + 1) * HEAD_DIM].astype(BF16))

    src = lax.broadcasted_iota(jnp.int32, (BLK, BLK), 0)
    dst = lax.broadcasted_iota(jnp.int32, (BLK, BLK), 1)
    causal_t = src <= dst
    zero_blk = jnp.zeros((BLK, BLK), BF16)
    for pg in range(GM_GROUPS // 2):
        g0, g1 = 2 * pg, 2 * pg + 1
        r0 = slice(g0 * GM_GROUP_DIM, (g0 + 1) * GM_GROUP_DIM)
        r1 = slice(g1 * GM_GROUP_DIM, (g1 + 1) * GM_GROUP_DIM)
        w0 = jnp.where(causal_t, wst_ref[g0], 0.0).astype(BF16)
        w1 = jnp.where(causal_t, wst_ref[g1], 0.0).astype(BF16)
        bd = jnp.concatenate([jnp.concatenate([w0, zero_blk], axis=1),
                              jnp.concatenate([zero_blk, w1], axis=1)], axis=0)
        lhs = jnp.concatenate(
            [jnp.concatenate([vnt_ref[r0, j * BLK:(j + 1) * BLK], vnt_ref[r1, j * BLK:(j + 1) * BLK]],
                             axis=1) for j in range(n_blk)], axis=0)
        st = jnp.dot(lhs, bd, preferred_element_type=F32) + rows_ref[ROW_BS:ROW_BS + 1, g0 * BLK:(g1 + 1) * BLK]
        for j in range(n_blk):
            cs = slice(j * BLK, (j + 1) * BLK)
            rs = slice(j * BLK, (j + 1) * BLK)
            yat_ref[r0, cs] = (ut_ref[r0, cs] * st[rs, :BLK]).astype(BF16)
            yat_ref[r1, cs] = (ut_ref[r1, cs] * st[rs, BLK:]).astype(BF16)

    width = HEADS_PER_ITEM * BLK
    key = lax.broadcasted_iota(jnp.int32, (BLK, width), 0)
    qry = lax.broadcasted_iota(jnp.int32, (BLK, width), 1) & (BLK - 1)
    cur = key <= qry
    no_prev = jnp.where(t == 0, -jnp.inf, 0.0).astype(F32)
    zero_half = jnp.zeros((HEAD_DIM, width), BF16)

    items = [(j, g, p) for j in range(n_blk) for g in range(N_KV_HEADS)
             for p in range(Q_REP // HEADS_PER_ITEM)]

    def scores(item):
        j, g, p = item
        h0 = g * Q_REP + p * HEADS_PER_ITEM
        cs = slice(j * BLK, (j + 1) * BLK)
        kband = kb_ref[j * BLK:(j + 2) * BLK, (g // 2) * LANES:(g // 2 + 1) * LANES]
        qg = jnp.concatenate(
            [qt_ref[(h0 + e) * HEAD_DIM:(h0 + e + 1) * HEAD_DIM, cs] for e in range(HEADS_PER_ITEM)],
            axis=1)
        rhs = jnp.concatenate([qg, zero_half] if g % 2 == 0 else [zero_half, qg], axis=0)
        return jnp.dot(kband, rhs, preferred_element_type=F32)

    def softmax(item, st):
        j, g, p = item
        h0 = g * Q_REP + p * HEADS_PER_ITEM
        s_prev = st[:BLK]
        if j == 0:
            s_prev = s_prev + no_prev
        live = jnp.where(cur, st[BLK:], s_prev)
        sink = rows_ref[ROW_SINK:ROW_SINK + 1, h0 * BLK:h0 * BLK + width]
        m = jnp.maximum(jnp.max(live, axis=0, keepdims=True), sink)
        pr = jnp.exp(live - m)
        p_sink = jnp.exp(sink - m)
        pcat = jnp.concatenate([jnp.where(cur, 0.0, pr), jnp.where(cur, pr, 0.0)], axis=0)
        return pcat.astype(BF16), p_sink

    def attend(item, pcat, p_sink):
        j, g, p = item
        h0 = g * Q_REP + p * HEADS_PER_ITEM
        ot = jnp.dot(vt_ref[g * VT_ROWS:(g + 1) * VT_ROWS, j * BLK:(j + 2) * BLK], pcat,
                     preferred_element_type=F32)
        inv = 1.0 / (ot[HEAD_DIM:HEAD_DIM + 1] + p_sink)
        o = (ot[:HEAD_DIM] * inv).astype(BF16)
        for e in range(HEADS_PER_ITEM):
            ybt_ref[(h0 + e) * HEAD_DIM:(h0 + e + 1) * HEAD_DIM, j * BLK:(j + 1) * BLK] = (
                o[:, e * BLK:(e + 1) * BLK])

    def gate_a(half_idx):
        rs = slice(half_idx * (TS_MIX // 2), (half_idx + 1) * (TS_MIX // 2))
        acc = lax.dot_general(yat_ref[:, rs], wa_ref[...], TN_DIMS, preferred_element_type=F32)
        mg_ref[rs, :] = mg_ref[rs, :] * acc

    n_items = len(items)
    fill = {n_items // 4 + ATT_LAG: 0, (3 * n_items) // 4 + ATT_LAG: 1}
    st, pc = {}, {}
    for step in range(n_items + 2 * ATT_LAG):
        if step < n_items:
            st[step] = scores(items[step])
        if step in fill:
            gate_a(fill[step])
        i = step - ATT_LAG
        if 0 <= i < n_items:
            pc[i] = softmax(items[i], st.pop(i))
        i = step - 2 * ATT_LAG
        if 0 <= i < n_items:
            attend(items[i], *pc.pop(i))

    kb_ref[0:BLK, :] = kb_ref[TS_MIX:TS_MIX + BLK, :]
    vt_ref[:, 0:BLK] = vt_ref[:, TS_MIX:TS_MIX + BLK]

    acc_b = lax.dot_general(ybt_ref[...], wb_ref[...], TN_DIMS, preferred_element_type=F32)
    merged = mg_ref[...] + sgb_ref[...].astype(F32) * acc_b
    y = jnp.dot(merged.astype(BF16), wo_ref[...], preferred_element_type=F32)
    o_ref[...] = x_ref[...] + mod_ref[2:3, :] * y


def _resident(shape):
    nd = len(shape)
    return pl.BlockSpec(shape, lambda *_: (0,) * nd, pipeline_mode=pl.Buffered(1))


def _mixer_call(x, mod, rows, w_in, b_t, ln_g, ln_b, ws_t, wa, wb, wo, w_up, w_down):
    batch, seq, _ = x.shape
    n_tiles = seq // TS_MIX
    n_steps = batch * n_tiles
    tile = pl.BlockSpec((None, TS_MIX, D_MODEL), lambda b, t: (b, t, 0))
    up_rows = pl.BlockSpec((D_MODEL // n_steps, D_FF), lambda b, t: (b * n_tiles + t, 0))
    dn_rows = pl.BlockSpec((D_FF // n_steps, D_MODEL), lambda b, t: (b * n_tiles + t, 0))
    return pl.pallas_call(
        _mixer_kernel,
        grid=(batch, n_tiles),
        in_specs=[
            tile,
            pl.BlockSpec((None, N_MOD, D_MODEL), lambda b, t: (b, 0, 0)),
            _resident((ROWS_SUBLANES, ROWS_WIDTH)),
            _resident((D_MODEL, IN_WIDTH)),
            _resident((T_ROWS, LANES)),
            _resident((D_MODEL, LANES)),
            _resident((D_MODEL, LANES)),
            _resident((GM_GROUPS, BLK, BLK)),
            _resident((D_MODEL, D_MODEL)),
            _resident((D_MODEL, D_MODEL)),
            _resident((D_MODEL, D_MODEL)),
            up_rows,
            dn_rows,
        ],
        out_specs=[tile, up_rows, dn_rows],
        out_shape=[jax.ShapeDtypeStruct(x.shape, F32),
                   jax.ShapeDtypeStruct(w_up.shape, BF16),
                   jax.ShapeDtypeStruct(w_down.shape, BF16)],
        scratch_shapes=[
            pltpu.VMEM((T_ROWS, D_MODEL), BF16),
            pltpu.VMEM((TS_MIX, D_MODEL), BF16),
            pltpu.VMEM((D_MODEL, TS_MIX), BF16),
            pltpu.VMEM((D_MODEL, TS_MIX), F32),
            pltpu.VMEM((D_MODEL, TS_MIX), BF16),
            pltpu.VMEM((D_MODEL, TS_MIX), BF16),
            pltpu.VMEM((D_MODEL, TS_MIX), BF16),
            pltpu.VMEM((BLK + TS_MIX, KV_WIDTH), BF16),
            pltpu.VMEM((N_KV_HEADS * VT_ROWS, BLK + TS_MIX), BF16),
            pltpu.VMEM((D_MODEL, TS_MIX), BF16),
            pltpu.VMEM((TS_MIX, D_MODEL), F32),
            pltpu.VMEM((TS_MIX, D_MODEL), BF16),
        ],
        compiler_params=pltpu.CompilerParams(
            dimension_semantics=("arbitrary", "arbitrary"),
            vmem_limit_bytes=VMEM_LIMIT_BYTES),
        name="token_mixer",
    )(x, mod, rows, w_in, b_t, ln_g, ln_b, ws_t, wa, wb, wo, w_up, w_down)


def _mlp_kernel(x_ref, mod_ref, gmlp_ref, wup_ref, wdn_ref, gfin_ref, o_ref, hb_ref, act_ref):
    x = x_ref[...]
    h = _rms_modulate(x, gmlp_ref[...], mod_ref[4:5, :], mod_ref[3:4, :])
    hb_ref[...] = h.astype(BF16)
    for c in range(D_FF // D_MODEL):
        cs = slice(c * D_MODEL, (c + 1) * D_MODEL)
        a = jnp.maximum(_dot(hb_ref[...], wup_ref[:, cs]), 0.0)
        act_ref[:, cs] = (a * a).astype(BF16)
    x2 = x_ref[...] + mod_ref[5:6, :] * _dot(act_ref[...], wdn_ref[...])
    ms = jnp.mean(x2 * x2, axis=-1, keepdims=True)
    o_ref[...] = x2 * lax.rsqrt(ms + EPS) * gfin_ref[...]


def _mlp_call(x, mod, g_mlp, w_up, w_down, g_final):
    batch, seq, _ = x.shape
    tile = pl.BlockSpec((None, TM_MLP, D_MODEL), lambda b, t: (b, t, 0))
    return pl.pallas_call(
        _mlp_kernel,
        grid=(batch, seq // TM_MLP),
        in_specs=[
            tile,
            pl.BlockSpec((None, N_MOD, D_MODEL), lambda b, t: (b, 0, 0)),
            _resident((1, D_MODEL)),
            _resident((D_MODEL, D_FF)),
            _resident((D_FF, D_MODEL)),
            _resident((1, D_MODEL)),
        ],
        out_specs=tile,
        out_shape=jax.ShapeDtypeStruct(x.shape, F32),
        scratch_shapes=[
            pltpu.VMEM((TM_MLP, D_MODEL), BF16),
            pltpu.VMEM((TM_MLP, D_FF), BF16),
        ],
        compiler_params=pltpu.CompilerParams(
            dimension_semantics=("arbitrary", "arbitrary"),
            vmem_limit_bytes=VMEM_LIMIT_BYTES),
        name="channel_mlp",
    )(x, mod, g_mlp, w_up, w_down, g_final)


def kernel(x, c, w_ada, b_ada, g_norm_mix, w_in, b_in, gm_ln_g, gm_ln_b, gm_ws, gm_bs, attn_sinks,
           w_branch_a, w_branch_b, w_out, g_norm_mlp, w_up, w_down, g_final):
    batch = x.shape[0]
    depth = w_in.shape[0]
    row = lambda v: v.reshape(1, -1)
    lane_col = lambda v: jnp.broadcast_to(v[:, None], (v.shape[0], LANES))
    for l in range(depth):
        mod = _ada_call(c, w_ada[l], row(b_ada[l])).reshape(batch, N_MOD, D_MODEL)
        bl = b_in[l]
        b_t = lane_col(jnp.concatenate([bl[O_U:O_K], bl[O_VA:O_GA]]))
        pad = lambda v: jnp.pad(v.reshape(-1), (0, ROWS_WIDTH - v.size))
        rows = jnp.zeros((ROWS_SUBLANES, ROWS_WIDTH), F32)
        rows = rows.at[ROW_GMIX].set(pad(g_norm_mix[l])).at[ROW_BIN].set(pad(bl))
        rows = rows.at[ROW_BS].set(pad(gm_bs[l])).at[ROW_SINK].set(pad(jnp.repeat(attn_sinks[l], BLK)))
        x, w_up_b, w_down_b = _mixer_call(
            x, mod, rows, w_in[l].astype(BF16), b_t,
            lane_col(gm_ln_g[l]), lane_col(gm_ln_b[l]), jnp.swapaxes(gm_ws[l], 1, 2),
            w_branch_a[l].astype(BF16), w_branch_b[l].astype(BF16), w_out[l].astype(BF16),
            w_up[l], w_down[l])
        assert depth == 1
        x = _mlp_call(x, mod, row(g_norm_mlp[l]), w_up_b, w_down_b, row(g_final))
    return x
```

```python
import jax
import jax.numpy as jnp
from jax import lax
from jax.experimental import pallas as pl
from jax.experimental.pallas import tpu as pltpu

D_MODEL = 1024
BLK = 128
GM_GROUPS = 8
GM_GROUP_DIM = D_MODEL // GM_GROUPS
N_Q_HEADS = 16
N_KV_HEADS = 4
HEAD_DIM = 64
Q_REP = N_Q_HEADS // N_KV_HEADS
KV_WIDTH = N_KV_HEADS * HEAD_DIM
D_FF = 4 * D_MODEL
N_MOD = 6
EPS = 1e-6

O_U = 0
O_V = O_U + D_MODEL
O_Q = O_V + D_MODEL
O_K = O_Q + D_MODEL
O_VA = O_K + KV_WIDTH
O_GA = O_VA + KV_WIDTH
O_GB = O_GA + D_MODEL
IN_WIDTH = O_GB + D_MODEL

T_U = 0
T_V = T_U + D_MODEL
T_Q = T_V + D_MODEL
T_VA = T_Q + D_MODEL
T_ROWS = T_VA + KV_WIDTH

LANES = 128
BF16_ROWS = 16
HEADS_PER_ITEM = 2
VT_ROWS = HEAD_DIM + BF16_ROWS

ROW_GMIX, ROW_BIN, ROW_BS, ROW_SINK = 0, 1, 2, 3
ROWS_SUBLANES = 8
ROWS_WIDTH = 8192
ATT_LAG = 6
TS_MIX = 512
TM_MLP = 1024
VMEM_LIMIT_BYTES = 56 * 1024 * 1024

F32 = jnp.float32
BF16 = jnp.bfloat16
NT_DIMS = (((1,), (1,)), ((), ()))
TN_DIMS = (((0,), (0,)), ((), ()))

GELU_C0 = 0.7978845608028654
GELU_C1 = GELU_C0 * 0.044715


def _sigmoid(z):
    return 0.5 * jnp.tanh(0.5 * z) + 0.5


def _gelu_tanh(z):
    hz = 0.5 * z
    return hz + hz * jnp.tanh(z * (GELU_C0 + GELU_C1 * (z * z)))


def _dot(a, b):
    return jnp.dot(a, b, preferred_element_type=F32)


def _rms_modulate(x, g, scale, shift):
    ms = jnp.mean(x * x, axis=-1, keepdims=True)
    h = x * lax.rsqrt(ms + EPS) * g
    return h * (1.0 + scale) + shift


def _lane_tile(col, n):
    return jnp.concatenate([col] * n, axis=1)


def _ada_kernel(c_ref, w_ref, b_ref, o_ref):
    c = c_ref[...]
    o_ref[...] = jnp.dot(c * _sigmoid(c), w_ref[...], preferred_element_type=F32) + b_ref[...]


def _ada_call(c, w_ada, b_ada):
    batch = c.shape[0]
    return pl.pallas_call(
        _ada_kernel,
        grid=(N_MOD,),
        in_specs=[
            pl.BlockSpec((batch, D_MODEL), lambda n: (0, 0)),
            pl.BlockSpec((D_MODEL, D_MODEL), lambda n: (0, n)),
            pl.BlockSpec((1, D_MODEL), lambda n: (0, n)),
        ],
        out_specs=pl.BlockSpec((batch, D_MODEL), lambda n: (0, n)),
        out_shape=jax.ShapeDtypeStruct((batch, N_MOD * D_MODEL), F32),
        name="ada_mod",
    )(c, w_ada, b_ada)


def _mixer_kernel(x_ref, mod_ref, rows_ref, win_ref, bt_ref, lng_ref, lnb_ref,
                  wst_ref, wa_ref, wb_ref, wo_ref, wup_ref, wdn_ref,
                  o_ref, wupb_ref, wdnb_ref,
                  wt_ref, hb_ref, vnt_ref, yat_ref, qt_ref, kb_ref, vt_ref, ybt_ref, mg_ref, sgb_ref):
    t = pl.program_id(1)
    n_blk = TS_MIX // BLK

    @pl.when(t == 0)
    def _():
        kb_ref[0:BLK, :] = jnp.zeros((BLK, KV_WIDTH), BF16)
        r = lax.broadcasted_iota(jnp.int32, vt_ref.shape, 0)
        ones_row = r == HEAD_DIM
        for g in range(1, N_KV_HEADS):
            ones_row = ones_row | (r == g * VT_ROWS + HEAD_DIM)
        vt_ref[...] = jnp.where(ones_row, 1.0, 0.0).astype(BF16)

    @pl.when((pl.program_id(0) == 0) & (t == 0))
    def _():
        for lo in range(O_U, O_K, D_MODEL):
            wt_ref[T_U + lo:T_U + lo + D_MODEL, :] = win_ref[:, lo:lo + D_MODEL].T
        wt_ref[T_VA:T_ROWS, :] = win_ref[:, O_VA:O_GA].T

    wupb_ref[...] = wup_ref[...].astype(BF16)
    wdnb_ref[...] = wdn_ref[...].astype(BF16)

    x = x_ref[...]
    h = _rms_modulate(x, rows_ref[ROW_GMIX:ROW_GMIX + 1, :D_MODEL], mod_ref[1:2, :], mod_ref[0:1, :])
    hb_ref[...] = h.astype(BF16)

    def proj_t(lo, hi):
        z = lax.dot_general(wt_ref[lo:hi, :], hb_ref[...], NT_DIMS, preferred_element_type=F32)
        return z + _lane_tile(bt_ref[lo:hi, :], n_blk)

    def proj_n(lo, hi):
        z = jnp.dot(hb_ref[...], win_ref[:, lo:hi], preferred_element_type=F32)
        return z + rows_ref[ROW_BIN:ROW_BIN + 1, lo:hi]

    half = D_MODEL // 2
    zv0 = proj_t(T_V, T_V + half)
    zv1 = proj_t(T_V + half, T_Q)
    zu0 = proj_n(O_U, O_U + half)
    zu1 = proj_n(O_U + half, O_V)
    gv0 = _gelu_tanh(zv0)
    zq0 = proj_t(T_Q, T_Q + half)
    gv1 = _gelu_tanh(zv1)
    zq1 = proj_t(T_Q + half, T_VA)
    gv = jnp.concatenate([gv0, gv1], axis=0)
    mu = jnp.mean(gv, axis=0, keepdims=True)
    gc = gv - mu
    var = jnp.mean(gc * gc, axis=0, keepdims=True)
    vn = gc * lax.rsqrt(var + EPS) * _lane_tile(lng_ref[...], n_blk) + _lane_tile(lnb_ref[...], n_blk)
    vnt_ref[...] = vn.astype(BF16)
    vat = proj_t(T_VA, T_ROWS)
    kb_ref[BLK:, :] = proj_n(O_K, O_VA).astype(BF16)
    mg_ref[:, :half] = _sigmoid(proj_n(O_GA, O_GA + half))
    mg_ref[:, half:] = _sigmoid(proj_n(O_GA + half, O_GB))
    sgb_ref[...] = _sigmoid(proj_n(O_GB, IN_WIDTH)).astype(BF16)
    qt_ref[:half, :] = (zq0 * (HEAD_DIM ** -0.5)).astype(BF16)
    qt_ref[half:, :] = (zq1 * (HEAD_DIM ** -0.5)).astype(BF16)
    for g in range(N_KV_HEADS):
        vt_ref[g * VT_ROWS:g * VT_ROWS + HEAD_DIM, BLK:] = (
            vat[g * HEAD_DIM:(g + 1) * HEAD_DIM].astype(BF16))

    src = lax.broadcasted_iota(jnp.int32, (BLK, BLK), 0)
    dst = lax.broadcasted_iota(jnp.int32, (BLK, BLK), 1)
    causal_t = src <= dst
    zero_blk = jnp.zeros((BLK, BLK), BF16)
    gatings = []
    for pg in range(GM_GROUPS // 2):
        g0, g1 = 2 * pg, 2 * pg + 1
        r0 = slice(g0 * GM_GROUP_DIM, (g0 + 1) * GM_GROUP_DIM)
        r1 = slice(g1 * GM_GROUP_DIM, (g1 + 1) * GM_GROUP_DIM)
        w0 = jnp.where(causal_t, wst_ref[g0], 0.0).astype(BF16)
        w1 = jnp.where(causal_t, wst_ref[g1], 0.0).astype(BF16)
        bd = jnp.concatenate([jnp.concatenate([w0, zero_blk], axis=1),
                              jnp.concatenate([zero_blk, w1], axis=1)], axis=0)
        lhs = jnp.concatenate(
            [jnp.concatenate([vnt_ref[r0, j * BLK:(j + 1) * BLK], vnt_ref[r1, j * BLK:(j + 1) * BLK]],
                             axis=1) for j in range(n_blk)], axis=0)
        gatings.append(
            jnp.dot(lhs, bd, preferred_element_type=F32) + rows_ref[ROW_BS:ROW_BS + 1, g0 * BLK:(g1 + 1) * BLK])

    zu = (zu0, zu1)

    def gated_u(half_idx):
        for pg, sg in enumerate(gatings):
            for e in range(2):
                g = 2 * pg + e
                cols = slice(g * GM_GROUP_DIM, (g + 1) * GM_GROUP_DIM)
                zpart = zu[g * GM_GROUP_DIM // half]
                zcols = slice(g * GM_GROUP_DIM % half, g * GM_GROUP_DIM % half + GM_GROUP_DIM)
                for j in range(half_idx * (n_blk // 2), (half_idx + 1) * (n_blk // 2)):
                    cs = slice(j * BLK, (j + 1) * BLK)
                    yat_ref[cs, cols] = (
                        _gelu_tanh(zpart[cs, zcols]) * sg[cs, e * BLK:(e + 1) * BLK].T).astype(BF16)

    width = HEADS_PER_ITEM * BLK
    key = lax.broadcasted_iota(jnp.int32, (BLK, width), 0)
    qry = lax.broadcasted_iota(jnp.int32, (BLK, width), 1) & (BLK - 1)
    cur = key <= qry
    no_prev = jnp.where(t == 0, -jnp.inf, 0.0).astype(F32)
    zero_half = jnp.zeros((HEAD_DIM, width), BF16)

    items = [(j, g, p) for j in range(n_blk) for g in range(N_KV_HEADS)
             for p in range(Q_REP // HEADS_PER_ITEM)]

    def scores(item):
        j, g, p = item
        h0 = g * Q_REP + p * HEADS_PER_ITEM
        cs = slice(j * BLK, (j + 1) * BLK)
        kband = kb_ref[j * BLK:(j + 2) * BLK, (g // 2) * LANES:(g // 2 + 1) * LANES]
        qg = jnp.concatenate(
            [qt_ref[(h0 + e) * HEAD_DIM:(h0 + e + 1) * HEAD_DIM, cs] for e in range(HEADS_PER_ITEM)],
            axis=1)
        rhs = jnp.concatenate([qg, zero_half] if g % 2 == 0 else [zero_half, qg], axis=0)
        return jnp.dot(kband, rhs, preferred_element_type=F32)

    def softmax(item, st):
        j, g, p = item
        h0 = g * Q_REP + p * HEADS_PER_ITEM
        s_prev = st[:BLK]
        if j == 0:
            s_prev = s_prev + no_prev
        live = jnp.where(cur, st[BLK:], s_prev)
        sink = rows_ref[ROW_SINK:ROW_SINK + 1, h0 * BLK:h0 * BLK + width]
        m = jnp.maximum(jnp.max(live, axis=0, keepdims=True), sink)
        pr = jnp.exp(live - m)
        p_sink = jnp.exp(sink - m)
        pcat = jnp.concatenate([jnp.where(cur, 0.0, pr), jnp.where(cur, pr, 0.0)], axis=0)
        return pcat.astype(BF16), p_sink

    def attend(item, pcat, p_sink):
        j, g, p = item
        h0 = g * Q_REP + p * HEADS_PER_ITEM
        ot = jnp.dot(vt_ref[g * VT_ROWS:(g + 1) * VT_ROWS, j * BLK:(j + 2) * BLK], pcat,
                     preferred_element_type=F32)
        inv = 1.0 / (ot[HEAD_DIM:HEAD_DIM + 1] + p_sink)
        o = (ot[:HEAD_DIM] * inv).astype(BF16)
        for e in range(HEADS_PER_ITEM):
            ybt_ref[(h0 + e) * HEAD_DIM:(h0 + e + 1) * HEAD_DIM, j * BLK:(j + 1) * BLK] = (
                o[:, e * BLK:(e + 1) * BLK])

    def gate_a(half_idx):
        rs = slice(half_idx * (TS_MIX // 2), (half_idx + 1) * (TS_MIX // 2))
        gated_u(half_idx)
        acc = jnp.dot(yat_ref[rs, :], wa_ref[...], preferred_element_type=F32)
        mg_ref[rs, :] = mg_ref[rs, :] * acc

    n_items = len(items)
    fill = {n_items // 4 + ATT_LAG: 0, (3 * n_items) // 4 + ATT_LAG: 1}
    st, pc = {}, {}
    for step in range(n_items + 2 * ATT_LAG):
        if step < n_items:
            st[step] = scores(items[step])
        if step in fill:
            gate_a(fill[step])
        i = step - ATT_LAG
        if 0 <= i < n_items:
            pc[i] = softmax(items[i], st.pop(i))
        i = step - 2 * ATT_LAG
        if 0 <= i < n_items:
            attend(items[i], *pc.pop(i))

    kb_ref[0:BLK, :] = kb_ref[TS_MIX:TS_MIX + BLK, :]
    vt_ref[:, 0:BLK] = vt_ref[:, TS_MIX:TS_MIX + BLK]

    acc_b = lax.dot_general(ybt_ref[...], wb_ref[...], TN_DIMS, preferred_element_type=F32)
    merged = mg_ref[...] + sgb_ref[...].astype(F32) * acc_b
    y = jnp.dot(merged.astype(BF16), wo_ref[...], preferred_element_type=F32)
    o_ref[...] = x_ref[...] + mod_ref[2:3, :] * y


def _resident(shape):
    nd = len(shape)
    return pl.BlockSpec(shape, lambda *_: (0,) * nd, pipeline_mode=pl.Buffered(1))


def _mixer_call(x, mod, rows, w_in, b_t, ln_g, ln_b, ws_t, wa, wb, wo, w_up, w_down):
    batch, seq, _ = x.shape
    n_tiles = seq // TS_MIX
    n_steps = batch * n_tiles
    tile = pl.BlockSpec((None, TS_MIX, D_MODEL), lambda b, t: (b, t, 0))
    up_rows = pl.BlockSpec((D_MODEL // n_steps, D_FF), lambda b, t: (b * n_tiles + t, 0))
    dn_rows = pl.BlockSpec((D_FF // n_steps, D_MODEL), lambda b, t: (b * n_tiles + t, 0))
    return pl.pallas_call(
        _mixer_kernel,
        grid=(batch, n_tiles),
        in_specs=[
            tile,
            pl.BlockSpec((None, N_MOD, D_MODEL), lambda b, t: (b, 0, 0)),
            _resident((ROWS_SUBLANES, ROWS_WIDTH)),
            _resident((D_MODEL, IN_WIDTH)),
            _resident((T_ROWS, LANES)),
            _resident((D_MODEL, LANES)),
            _resident((D_MODEL, LANES)),
            _resident((GM_GROUPS, BLK, BLK)),
            _resident((D_MODEL, D_MODEL)),
            _resident((D_MODEL, D_MODEL)),
            _resident((D_MODEL, D_MODEL)),
            up_rows,
            dn_rows,
        ],
        out_specs=[tile, up_rows, dn_rows],
        out_shape=[jax.ShapeDtypeStruct(x.shape, F32),
                   jax.ShapeDtypeStruct(w_up.shape, BF16),
                   jax.ShapeDtypeStruct(w_down.shape, BF16)],
        scratch_shapes=[
            pltpu.VMEM((T_ROWS, D_MODEL), BF16),
            pltpu.VMEM((TS_MIX, D_MODEL), BF16),
            pltpu.VMEM((D_MODEL, TS_MIX), BF16),
            pltpu.VMEM((TS_MIX, D_MODEL), BF16),
            pltpu.VMEM((D_MODEL, TS_MIX), BF16),
            pltpu.VMEM((BLK + TS_MIX, KV_WIDTH), BF16),
            pltpu.VMEM((N_KV_HEADS * VT_ROWS, BLK + TS_MIX), BF16),
            pltpu.VMEM((D_MODEL, TS_MIX), BF16),
            pltpu.VMEM((TS_MIX, D_MODEL), F32),
            pltpu.VMEM((TS_MIX, D_MODEL), BF16),
        ],
        compiler_params=pltpu.CompilerParams(
            dimension_semantics=("arbitrary", "arbitrary"),
            vmem_limit_bytes=VMEM_LIMIT_BYTES),
        name="token_mixer",
    )(x, mod, rows, w_in, b_t, ln_g, ln_b, ws_t, wa, wb, wo, w_up, w_down)


def _mlp_kernel(x_ref, mod_ref, gmlp_ref, wup_ref, wdn_ref, gfin_ref, o_ref, hb_ref, act_ref):
    x = x_ref[...]
    h = _rms_modulate(x, gmlp_ref[...], mod_ref[4:5, :], mod_ref[3:4, :])
    hb_ref[...] = h.astype(BF16)
    for c in range(D_FF // D_MODEL):
        cs = slice(c * D_MODEL, (c + 1) * D_MODEL)
        a = jnp.maximum(_dot(hb_ref[...], wup_ref[:, cs]), 0.0)
        act_ref[:, cs] = (a * a).astype(BF16)
    x2 = x_ref[...] + mod_ref[5:6, :] * _dot(act_ref[...], wdn_ref[...])
    ms = jnp.mean(x2 * x2, axis=-1, keepdims=True)
    o_ref[...] = x2 * lax.rsqrt(ms + EPS) * gfin_ref[...]


def _mlp_call(x, mod, g_mlp, w_up, w_down, g_final):
    batch, seq, _ = x.shape
    tile = pl.BlockSpec((None, TM_MLP, D_MODEL), lambda b, t: (b, t, 0))
    return pl.pallas_call(
        _mlp_kernel,
        grid=(batch, seq // TM_MLP),
        in_specs=[
            tile,
            pl.BlockSpec((None, N_MOD, D_MODEL), lambda b, t: (b, 0, 0)),
            _resident((1, D_MODEL)),
            _resident((D_MODEL, D_FF)),
            _resident((D_FF, D_MODEL)),
            _resident((1, D_MODEL)),
        ],
        out_specs=tile,
        out_shape=jax.ShapeDtypeStruct(x.shape, F32),
        scratch_shapes=[
            pltpu.VMEM((TM_MLP, D_MODEL), BF16),
            pltpu.VMEM((TM_MLP, D_FF), BF16),
        ],
        compiler_params=pltpu.CompilerParams(
            dimension_semantics=("arbitrary", "arbitrary"),
            vmem_limit_bytes=VMEM_LIMIT_BYTES),
        name="channel_mlp",
    )(x, mod, g_mlp, w_up, w_down, g_final)


def kernel(x, c, w_ada, b_ada, g_norm_mix, w_in, b_in, gm_ln_g, gm_ln_b, gm_ws, gm_bs, attn_sinks,
           w_branch_a, w_branch_b, w_out, g_norm_mlp, w_up, w_down, g_final):
    batch = x.shape[0]
    depth = w_in.shape[0]
    row = lambda v: v.reshape(1, -1)
    lane_col = lambda v: jnp.broadcast_to(v[:, None], (v.shape[0], LANES))
    for l in range(depth):
        mod = _ada_call(c, w_ada[l], row(b_ada[l])).reshape(batch, N_MOD, D_MODEL)
        bl = b_in[l]
        b_t = lane_col(jnp.concatenate([bl[O_U:O_K], bl[O_VA:O_GA]]))
        pad = lambda v: jnp.pad(v.reshape(-1), (0, ROWS_WIDTH - v.size))
        rows = jnp.zeros((ROWS_SUBLANES, ROWS_WIDTH), F32)
        rows = rows.at[ROW_GMIX].set(pad(g_norm_mix[l])).at[ROW_BIN].set(pad(bl))
        rows = rows.at[ROW_BS].set(pad(gm_bs[l])).at[ROW_SINK].set(pad(jnp.repeat(attn_sinks[l], BLK)))
        x, w_up_b, w_down_b = _mixer_call(
            x, mod, rows, w_in[l].astype(BF16), b_t,
            lane_col(gm_ln_g[l]), lane_col(gm_ln_b[l]), jnp.swapaxes(gm_ws[l], 1, 2),
            w_branch_a[l].astype(BF16), w_branch_b[l].astype(BF16), w_out[l].astype(BF16),
            w_up[l], w_down[l])
        assert depth == 1
        x = _mlp_call(x, mod, row(g_norm_mlp[l]), w_up_b, w_down_b, row(g_final))
    return x
```

```python
import jax
import jax.numpy as jnp
from jax import lax
from jax.experimental import pallas as pl
from jax.experimental.pallas import tpu as pltpu

D_MODEL = 1024
BLK = 128
GM_GROUPS = 8
GM_GROUP_DIM = D_MODEL // GM_GROUPS
N_Q_HEADS = 16
N_KV_HEADS = 4
HEAD_DIM = 64
Q_REP = N_Q_HEADS // N_KV_HEADS
KV_WIDTH = N_KV_HEADS * HEAD_DIM
D_FF = 4 * D_MODEL
N_MOD = 6
EPS = 1e-6

O_U = 0
O_V = O_U + D_MODEL
O_Q = O_V + D_MODEL
O_K = O_Q + D_MODEL
O_VA = O_K + KV_WIDTH
O_GA = O_VA + KV_WIDTH
O_GB = O_GA + D_MODEL
IN_WIDTH = O_GB + D_MODEL

T_U = 0
T_V = T_U + D_MODEL
T_Q = T_V + D_MODEL
T_VA = T_Q + D_MODEL
T_ROWS = T_VA + KV_WIDTH

LANES = 128
BF16_ROWS = 16
HEADS_PER_ITEM = 2
VT_ROWS = HEAD_DIM + BF16_ROWS

ROW_GMIX, ROW_BIN, ROW_BS, ROW_SINK, ROW_LNG, ROW_LNB = 0, 1, 2, 3, 4, 5
ROWS_SUBLANES = 8
ROWS_WIDTH = 8192
ATT_LAG = 6
TS_MIX = 512
TM_MLP = 1024
VMEM_LIMIT_BYTES = 56 * 1024 * 1024

F32 = jnp.float32
BF16 = jnp.bfloat16
NT_DIMS = (((1,), (1,)), ((), ()))
TN_DIMS = (((0,), (0,)), ((), ()))

GELU_C0 = 0.7978845608028654
GELU_C1 = GELU_C0 * 0.044715


def _sigmoid(z):
    return 0.5 * jnp.tanh(0.5 * z) + 0.5


def _gelu_tanh(z):
    hz = 0.5 * z
    return hz + hz * jnp.tanh(z * (GELU_C0 + GELU_C1 * (z * z)))


def _dot(a, b):
    return jnp.dot(a, b, preferred_element_type=F32)


def _rms_modulate(x, g, scale, shift):
    ms = jnp.mean(x * x, axis=-1, keepdims=True)
    h = x * lax.rsqrt(ms + EPS) * g
    return h * (1.0 + scale) + shift


def _lane_tile(col, n):
    return jnp.concatenate([col] * n, axis=1)


def _ada_kernel(c_ref, w_ref, b_ref, o_ref):
    c = c_ref[...]
    o_ref[...] = jnp.dot(c * _sigmoid(c), w_ref[...], preferred_element_type=F32) + b_ref[...]


def _ada_call(c, w_ada, b_ada):
    batch = c.shape[0]
    return pl.pallas_call(
        _ada_kernel,
        grid=(N_MOD,),
        in_specs=[
            pl.BlockSpec((batch, D_MODEL), lambda n: (0, 0)),
            pl.BlockSpec((D_MODEL, D_MODEL), lambda n: (0, n)),
            pl.BlockSpec((1, D_MODEL), lambda n: (0, n)),
        ],
        out_specs=pl.BlockSpec((batch, D_MODEL), lambda n: (0, n)),
        out_shape=jax.ShapeDtypeStruct((batch, N_MOD * D_MODEL), F32),
        name="ada_mod",
    )(c, w_ada, b_ada)


def _mixer_kernel(x_ref, mod_ref, rows_ref, win_ref, bt_ref,
                  wst_ref, wa_ref, wb_ref, wo_ref, wup_ref, wdn_ref,
                  o_ref, wupb_ref, wdnb_ref,
                  wt_ref, hb_ref, vn_ref, yat_ref, qt_ref, kb_ref, vt_ref, ybt_ref, mg_ref, sgb_ref):
    t = pl.program_id(1)
    n_blk = TS_MIX // BLK

    @pl.when(t == 0)
    def _():
        kb_ref[0:BLK, :] = jnp.zeros((BLK, KV_WIDTH), BF16)
        r = lax.broadcasted_iota(jnp.int32, vt_ref.shape, 0)
        ones_row = r == HEAD_DIM
        for g in range(1, N_KV_HEADS):
            ones_row = ones_row | (r == g * VT_ROWS + HEAD_DIM)
        vt_ref[...] = jnp.where(ones_row, 1.0, 0.0).astype(BF16)

    @pl.when((pl.program_id(0) == 0) & (t == 0))
    def _():
        for lo in range(O_U, O_K, D_MODEL):
            wt_ref[T_U + lo:T_U + lo + D_MODEL, :] = win_ref[:, lo:lo + D_MODEL].T
        wt_ref[T_VA:T_ROWS, :] = win_ref[:, O_VA:O_GA].T

    wupb_ref[...] = wup_ref[...].astype(BF16)
    wdnb_ref[...] = wdn_ref[...].astype(BF16)

    x = x_ref[...]
    h = _rms_modulate(x, rows_ref[ROW_GMIX:ROW_GMIX + 1, :D_MODEL], mod_ref[1:2, :], mod_ref[0:1, :])
    hb_ref[...] = h.astype(BF16)

    def proj_t(lo, hi):
        z = lax.dot_general(wt_ref[lo:hi, :], hb_ref[...], NT_DIMS, preferred_element_type=F32)
        return z + _lane_tile(bt_ref[lo:hi, :], n_blk)

    def proj_n(lo, hi):
        z = jnp.dot(hb_ref[...], win_ref[:, lo:hi], preferred_element_type=F32)
        return z + rows_ref[ROW_BIN:ROW_BIN + 1, lo:hi]

    half = D_MODEL // 2
    zv0 = proj_n(O_V, O_V + half)
    zv1 = proj_n(O_V + half, O_Q)
    zu0 = proj_n(O_U, O_U + half)
    zu1 = proj_n(O_U + half, O_V)
    gv0 = _gelu_tanh(zv0)
    zq0 = proj_t(T_Q, T_Q + half)
    gv1 = _gelu_tanh(zv1)
    zq1 = proj_t(T_Q + half, T_VA)
    gv = jnp.concatenate([gv0, gv1], axis=1)
    mu = jnp.mean(gv, axis=-1, keepdims=True)
    gc = gv - mu
    var = jnp.mean(gc * gc, axis=-1, keepdims=True)
    vn = (gc * lax.rsqrt(var + EPS) * rows_ref[ROW_LNG:ROW_LNG + 1, :D_MODEL]
          + rows_ref[ROW_LNB:ROW_LNB + 1, :D_MODEL])
    vn_ref[...] = vn.astype(BF16)
    vat = proj_t(T_VA, T_ROWS)
    kb_ref[BLK:, :] = proj_n(O_K, O_VA).astype(BF16)
    mg_ref[:, :half] = _sigmoid(proj_n(O_GA, O_GA + half))
    mg_ref[:, half:] = _sigmoid(proj_n(O_GA + half, O_GB))
    sgb_ref[...] = _sigmoid(proj_n(O_GB, IN_WIDTH)).astype(BF16)
    qt_ref[:half, :] = (zq0 * (HEAD_DIM ** -0.5)).astype(BF16)
    qt_ref[half:, :] = (zq1 * (HEAD_DIM ** -0.5)).astype(BF16)
    for g in range(N_KV_HEADS):
        vt_ref[g * VT_ROWS:g * VT_ROWS + HEAD_DIM, BLK:] = (
            vat[g * HEAD_DIM:(g + 1) * HEAD_DIM].astype(BF16))

    src = lax.broadcasted_iota(jnp.int32, (BLK, BLK), 0)
    dst = lax.broadcasted_iota(jnp.int32, (BLK, BLK), 1)
    causal_t = src <= dst
    zero_blk = jnp.zeros((BLK, BLK), BF16)
    gatings = []
    for pg in range(GM_GROUPS // 2):
        g0, g1 = 2 * pg, 2 * pg + 1
        r0 = slice(g0 * GM_GROUP_DIM, (g0 + 1) * GM_GROUP_DIM)
        r1 = slice(g1 * GM_GROUP_DIM, (g1 + 1) * GM_GROUP_DIM)
        w0 = jnp.where(causal_t, wst_ref[g0], 0.0).astype(BF16)
        w1 = jnp.where(causal_t, wst_ref[g1], 0.0).astype(BF16)
        bd = jnp.concatenate([jnp.concatenate([w0, zero_blk], axis=1),
                              jnp.concatenate([zero_blk, w1], axis=1)], axis=0)
        lhs = jnp.concatenate(
            [jnp.concatenate([vn_ref[j * BLK:(j + 1) * BLK, r0].T, vn_ref[j * BLK:(j + 1) * BLK, r1].T],
                             axis=1) for j in range(n_blk)], axis=0)
        gatings.append(
            jnp.dot(lhs, bd, preferred_element_type=F32) + rows_ref[ROW_BS:ROW_BS + 1, g0 * BLK:(g1 + 1) * BLK])

    zu = (zu0, zu1)

    def gated_u(half_idx):
        for pg, sg in enumerate(gatings):
            for e in range(2):
                g = 2 * pg + e
                cols = slice(g * GM_GROUP_DIM, (g + 1) * GM_GROUP_DIM)
                zpart = zu[g * GM_GROUP_DIM // half]
                zcols = slice(g * GM_GROUP_DIM % half, g * GM_GROUP_DIM % half + GM_GROUP_DIM)
                for j in range(half_idx * (n_blk // 2), (half_idx + 1) * (n_blk // 2)):
                    cs = slice(j * BLK, (j + 1) * BLK)
                    yat_ref[cs, cols] = (
                        _gelu_tanh(zpart[cs, zcols]) * sg[cs, e * BLK:(e + 1) * BLK].T).astype(BF16)

    width = HEADS_PER_ITEM * BLK
    key = lax.broadcasted_iota(jnp.int32, (BLK, width), 0)
    qry = lax.broadcasted_iota(jnp.int32, (BLK, width), 1) & (BLK - 1)
    cur = key <= qry
    no_prev = jnp.where(t == 0, -jnp.inf, 0.0).astype(F32)
    zero_half = jnp.zeros((HEAD_DIM, width), BF16)

    items = [(j, g, p) for j in range(n_blk) for g in range(N_KV_HEADS)
             for p in range(Q_REP // HEADS_PER_ITEM)]

    def scores(item):
        j, g, p = item
        h0 = g * Q_REP + p * HEADS_PER_ITEM
        cs = slice(j * BLK, (j + 1) * BLK)
        kband = kb_ref[j * BLK:(j + 2) * BLK, (g // 2) * LANES:(g // 2 + 1) * LANES]
        qg = jnp.concatenate(
            [qt_ref[(h0 + e) * HEAD_DIM:(h0 + e + 1) * HEAD_DIM, cs] for e in range(HEADS_PER_ITEM)],
            axis=1)
        rhs = jnp.concatenate([qg, zero_half] if g % 2 == 0 else [zero_half, qg], axis=0)
        return jnp.dot(kband, rhs, preferred_element_type=F32)

    def softmax(item, st):
        j, g, p = item
        h0 = g * Q_REP + p * HEADS_PER_ITEM
        s_prev = st[:BLK]
        if j == 0:
            s_prev = s_prev + no_prev
        live = jnp.where(cur, st[BLK:], s_prev)
        sink = rows_ref[ROW_SINK:ROW_SINK + 1, h0 * BLK:h0 * BLK + width]
        m = jnp.maximum(jnp.max(live, axis=0, keepdims=True), sink)
        pr = jnp.exp(live - m)
        p_sink = jnp.exp(sink - m)
        pcat = jnp.concatenate([jnp.where(cur, 0.0, pr), jnp.where(cur, pr, 0.0)], axis=0)
        return pcat.astype(BF16), p_sink

    def attend(item, pcat, p_sink):
        j, g, p = item
        h0 = g * Q_REP + p * HEADS_PER_ITEM
        ot = jnp.dot(vt_ref[g * VT_ROWS:(g + 1) * VT_ROWS, j * BLK:(j + 2) * BLK], pcat,
                     preferred_element_type=F32)
        inv = 1.0 / (ot[HEAD_DIM:HEAD_DIM + 1] + p_sink)
        o = (ot[:HEAD_DIM] * inv).astype(BF16)
        for e in range(HEADS_PER_ITEM):
            ybt_ref[(h0 + e) * HEAD_DIM:(h0 + e + 1) * HEAD_DIM, j * BLK:(j + 1) * BLK] = (
                o[:, e * BLK:(e + 1) * BLK])

    def gate_a(half_idx):
        rs = slice(half_idx * (TS_MIX // 2), (half_idx + 1) * (TS_MIX // 2))
        gated_u(half_idx)
        acc = jnp.dot(yat_ref[rs, :], wa_ref[...], preferred_element_type=F32)
        mg_ref[rs, :] = mg_ref[rs, :] * acc

    n_items = len(items)
    fill = {n_items // 4 + ATT_LAG: 0, (3 * n_items) // 4 + ATT_LAG: 1}
    st, pc = {}, {}
    for step in range(n_items + 2 * ATT_LAG):
        if step < n_items:
            st[step] = scores(items[step])
        if step in fill:
            gate_a(fill[step])
        i = step - ATT_LAG
        if 0 <= i < n_items:
            pc[i] = softmax(items[i], st.pop(i))
        i = step - 2 * ATT_LAG
        if 0 <= i < n_items:
            attend(items[i], *pc.pop(i))

    kb_ref[0:BLK, :] = kb_ref[TS_MIX:TS_MIX + BLK, :]
    vt_ref[:, 0:BLK] = vt_ref[:, TS_MIX:TS_MIX + BLK]

    acc_b = lax.dot_general(ybt_ref[...], wb_ref[...], TN_DIMS, preferred_element_type=F32)
    merged = mg_ref[...] + sgb_ref[...].astype(F32) * acc_b
    y = jnp.dot(merged.astype(BF16), wo_ref[...], preferred_element_type=F32)
    o_ref[...] = x_ref[...] + mod_ref[2:3, :] * y


def _resident(shape):
    nd = len(shape)
    return pl.BlockSpec(shape, lambda *_: (0,) * nd, pipeline_mode=pl.Buffered(1))


def _mixer_call(x, mod, rows, w_in, b_t, ws_t, wa, wb, wo, w_up, w_down):
    batch, seq, _ = x.shape
    n_tiles = seq // TS_MIX
    n_steps = batch * n_tiles
    tile = pl.BlockSpec((None, TS_MIX, D_MODEL), lambda b, t: (b, t, 0))
    up_rows = pl.BlockSpec((D_MODEL // n_steps, D_FF), lambda b, t: (b * n_tiles + t, 0))
    dn_rows = pl.BlockSpec((D_FF // n_steps, D_MODEL), lambda b, t: (b * n_tiles + t, 0))
    return pl.pallas_call(
        _mixer_kernel,
        grid=(batch, n_tiles),
        in_specs=[
            tile,
            pl.BlockSpec((None, N_MOD, D_MODEL), lambda b, t: (b, 0, 0)),
            _resident((ROWS_SUBLANES, ROWS_WIDTH)),
            _resident((D_MODEL, IN_WIDTH)),
            _resident((T_ROWS, LANES)),
            _resident((GM_GROUPS, BLK, BLK)),
            _resident((D_MODEL, D_MODEL)),
            _resident((D_MODEL, D_MODEL)),
            _resident((D_MODEL, D_MODEL)),
            up_rows,
            dn_rows,
        ],
        out_specs=[tile, up_rows, dn_rows],
        out_shape=[jax.ShapeDtypeStruct(x.shape, F32),
                   jax.ShapeDtypeStruct(w_up.shape, BF16),
                   jax.ShapeDtypeStruct(w_down.shape, BF16)],
        scratch_shapes=[
            pltpu.VMEM((T_ROWS, D_MODEL), BF16),
            pltpu.VMEM((TS_MIX, D_MODEL), BF16),
            pltpu.VMEM((TS_MIX, D_MODEL), BF16),
            pltpu.VMEM((TS_MIX, D_MODEL), BF16),
            pltpu.VMEM((D_MODEL, TS_MIX), BF16),
            pltpu.VMEM((BLK + TS_MIX, KV_WIDTH), BF16),
            pltpu.VMEM((N_KV_HEADS * VT_ROWS, BLK + TS_MIX), BF16),
            pltpu.VMEM((D_MODEL, TS_MIX), BF16),
            pltpu.VMEM((TS_MIX, D_MODEL), F32),
            pltpu.VMEM((TS_MIX, D_MODEL), BF16),
        ],
        compiler_params=pltpu.CompilerParams(
            dimension_semantics=("arbitrary", "arbitrary"),
            vmem_limit_bytes=VMEM_LIMIT_BYTES),
        name="token_mixer",
    )(x, mod, rows, w_in, b_t, ws_t, wa, wb, wo, w_up, w_down)


def _mlp_kernel(x_ref, mod_ref, gmlp_ref, wup_ref, wdn_ref, gfin_ref, o_ref, hb_ref, act_ref):
    x = x_ref[...]
    h = _rms_modulate(x, gmlp_ref[...], mod_ref[4:5, :], mod_ref[3:4, :])
    hb_ref[...] = h.astype(BF16)
    for c in range(D_FF // D_MODEL):
        cs = slice(c * D_MODEL, (c + 1) * D_MODEL)
        a = jnp.maximum(_dot(hb_ref[...], wup_ref[:, cs]), 0.0)
        act_ref[:, cs] = (a * a).astype(BF16)
    x2 = x_ref[...] + mod_ref[5:6, :] * _dot(act_ref[...], wdn_ref[...])
    ms = jnp.mean(x2 * x2, axis=-1, keepdims=True)
    o_ref[...] = x2 * lax.rsqrt(ms + EPS) * gfin_ref[...]


def _mlp_call(x, mod, g_mlp, w_up, w_down, g_final):
    batch, seq, _ = x.shape
    tile = pl.BlockSpec((None, TM_MLP, D_MODEL), lambda b, t: (b, t, 0))
    return pl.pallas_call(
        _mlp_kernel,
        grid=(batch, seq // TM_MLP),
        in_specs=[
            tile,
            pl.BlockSpec((None, N_MOD, D_MODEL), lambda b, t: (b, 0, 0)),
            _resident((1, D_MODEL)),
            _resident((D_MODEL, D_FF)),
            _resident((D_FF, D_MODEL)),
            _resident((1, D_MODEL)),
        ],
        out_specs=tile,
        out_shape=jax.ShapeDtypeStruct(x.shape, F32),
        scratch_shapes=[
            pltpu.VMEM((TM_MLP, D_MODEL), BF16),
            pltpu.VMEM((TM_MLP, D_FF), BF16),
        ],
        compiler_params=pltpu.CompilerParams(
            dimension_semantics=("arbitrary", "arbitrary"),
            vmem_limit_bytes=VMEM_LIMIT_BYTES),
        name="channel_mlp",
    )(x, mod, g_mlp, w_up, w_down, g_final)


def kernel(x, c, w_ada, b_ada, g_norm_mix, w_in, b_in, gm_ln_g, gm_ln_b, gm_ws, gm_bs, attn_sinks,
           w_branch_a, w_branch_b, w_out, g_norm_mlp, w_up, w_down, g_final):
    batch = x.shape[0]
    depth = w_in.shape[0]
    row = lambda v: v.reshape(1, -1)
    lane_col = lambda v: jnp.broadcast_to(v[:, None], (v.shape[0], LANES))
    for l in range(depth):
        mod = _ada_call(c, w_ada[l], row(b_ada[l])).reshape(batch, N_MOD, D_MODEL)
        bl = b_in[l]
        b_t = lane_col(jnp.concatenate([bl[O_U:O_K], bl[O_VA:O_GA]]))
        pad = lambda v: jnp.pad(v.reshape(-1), (0, ROWS_WIDTH - v.size))
        rows = jnp.zeros((ROWS_SUBLANES, ROWS_WIDTH), F32)
        rows = rows.at[ROW_GMIX].set(pad(g_norm_mix[l])).at[ROW_BIN].set(pad(bl))
        rows = rows.at[ROW_BS].set(pad(gm_bs[l])).at[ROW_SINK].set(pad(jnp.repeat(attn_sinks[l], BLK)))
        rows = rows.at[ROW_LNG].set(pad(gm_ln_g[l])).at[ROW_LNB].set(pad(gm_ln_b[l]))
        x, w_up_b, w_down_b = _mixer_call(
            x, mod, rows, w_in[l].astype(BF16), b_t,
            jnp.swapaxes(gm_ws[l], 1, 2),
            w_branch_a[l].astype(BF16), w_branch_b[l].astype(BF16), w_out[l].astype(BF16),
            w_up[l], w_down[l])
        assert depth == 1
        x = _mlp_call(x, mod, row(g_norm_mlp[l]), w_up_b, w_down_b, row(g_final))
    return x
```

```python
import jax
import jax.numpy as jnp
from jax import lax
from jax.experimental import pallas as pl
from jax.experimental.pallas import tpu as pltpu

D_MODEL = 1024
BLK = 128
GM_GROUPS = 8
GM_GROUP_DIM = D_MODEL // GM_GROUPS
N_Q_HEADS = 16
N_KV_HEADS = 4
HEAD_DIM = 64
Q_REP = N_Q_HEADS // N_KV_HEADS
KV_WIDTH = N_KV_HEADS * HEAD_DIM
D_FF = 4 * D_MODEL
N_MOD = 6
EPS = 1e-6

O_U = 0
O_V = O_U + D_MODEL
O_Q = O_V + D_MODEL
O_K = O_Q + D_MODEL
O_VA = O_K + KV_WIDTH
O_GA = O_VA + KV_WIDTH
O_GB = O_GA + D_MODEL
IN_WIDTH = O_GB + D_MODEL

LANES = 128
BF16_ROWS = 16
HEADS_PER_ITEM = 2
VT_ROWS = HEAD_DIM + BF16_ROWS

ROW_GMIX, ROW_BIN, ROW_BS, ROW_SINK, ROW_LNG, ROW_LNB = 0, 1, 2, 3, 4, 5
ROWS_SUBLANES = 8
ROWS_WIDTH = 8192
ATT_LAG = 6
TS_MIX = 512
TM_MLP = 1024
VMEM_LIMIT_BYTES = 56 * 1024 * 1024

F32 = jnp.float32
BF16 = jnp.bfloat16
TN_DIMS = (((0,), (0,)), ((), ()))

GELU_C0 = 0.7978845608028654
GELU_C1 = GELU_C0 * 0.044715


def _sigmoid(z):
    return 0.5 * jnp.tanh(0.5 * z) + 0.5


def _gelu_tanh(z):
    hz = 0.5 * z
    return hz + hz * jnp.tanh(z * (GELU_C0 + GELU_C1 * (z * z)))


def _dot(a, b):
    return jnp.dot(a, b, preferred_element_type=F32)


def _rms_modulate(x, g, scale, shift):
    ms = jnp.mean(x * x, axis=-1, keepdims=True)
    h = x * lax.rsqrt(ms + EPS) * g
    return h * (1.0 + scale) + shift


def _ada_kernel(c_ref, w_ref, b_ref, o_ref):
    c = c_ref[...]
    o_ref[...] = jnp.dot(c * _sigmoid(c), w_ref[...], preferred_element_type=F32) + b_ref[...]


def _ada_call(c, w_ada, b_ada):
    batch = c.shape[0]
    return pl.pallas_call(
        _ada_kernel,
        grid=(N_MOD,),
        in_specs=[
            pl.BlockSpec((batch, D_MODEL), lambda n: (0, 0)),
            pl.BlockSpec((D_MODEL, D_MODEL), lambda n: (0, n)),
            pl.BlockSpec((1, D_MODEL), lambda n: (0, n)),
        ],
        out_specs=pl.BlockSpec((batch, D_MODEL), lambda n: (0, n)),
        out_shape=jax.ShapeDtypeStruct((batch, N_MOD * D_MODEL), F32),
        name="ada_mod",
    )(c, w_ada, b_ada)


def _mixer_kernel(x_ref, mod_ref, rows_ref, win_ref,
                  wst_ref, wa_ref, wb_ref, wo_ref, wup_ref, wdn_ref,
                  o_ref, wupb_ref, wdnb_ref,
                  hb_ref, vn_ref, yat_ref, qt_ref, kb_ref, vt_ref, ybt_ref, mg_ref, sgb_ref):
    t = pl.program_id(1)
    n_blk = TS_MIX // BLK

    @pl.when(t == 0)
    def _():
        kb_ref[0:BLK, :] = jnp.zeros((BLK, KV_WIDTH), BF16)
        r = lax.broadcasted_iota(jnp.int32, vt_ref.shape, 0)
        ones_row = r == HEAD_DIM
        for g in range(1, N_KV_HEADS):
            ones_row = ones_row | (r == g * VT_ROWS + HEAD_DIM)
        vt_ref[...] = jnp.where(ones_row, 1.0, 0.0).astype(BF16)

    wupb_ref[...] = wup_ref[...].astype(BF16)
    wdnb_ref[...] = wdn_ref[...].astype(BF16)

    x = x_ref[...]
    h = _rms_modulate(x, rows_ref[ROW_GMIX:ROW_GMIX + 1, :D_MODEL], mod_ref[1:2, :], mod_ref[0:1, :])
    hb_ref[...] = h.astype(BF16)

    def proj_n(lo, hi):
        z = jnp.dot(hb_ref[...], win_ref[:, lo:hi], preferred_element_type=F32)
        return z + rows_ref[ROW_BIN:ROW_BIN + 1, lo:hi]

    half = D_MODEL // 2
    zv0 = proj_n(O_V, O_V + half)
    zv1 = proj_n(O_V + half, O_Q)
    zu0 = proj_n(O_U, O_U + half)
    zu1 = proj_n(O_U + half, O_V)
    gv0 = _gelu_tanh(zv0)
    zq0 = proj_n(O_Q, O_Q + half)
    gv1 = _gelu_tanh(zv1)
    zq1 = proj_n(O_Q + half, O_K)
    gv = jnp.concatenate([gv0, gv1], axis=1)
    mu = jnp.mean(gv, axis=-1, keepdims=True)
    gc = gv - mu
    var = jnp.mean(gc * gc, axis=-1, keepdims=True)
    vn = (gc * lax.rsqrt(var + EPS) * rows_ref[ROW_LNG:ROW_LNG + 1, :D_MODEL]
          + rows_ref[ROW_LNB:ROW_LNB + 1, :D_MODEL])
    vn_ref[...] = vn.astype(BF16)
    va = proj_n(O_VA, O_GA)
    kb_ref[BLK:, :] = proj_n(O_K, O_VA).astype(BF16)
    mg_ref[:, :half] = _sigmoid(proj_n(O_GA, O_GA + half))
    mg_ref[:, half:] = _sigmoid(proj_n(O_GA + half, O_GB))
    sgb_ref[...] = _sigmoid(proj_n(O_GB, IN_WIDTH)).astype(BF16)
    qt_ref[:half, :] = (zq0 * (HEAD_DIM ** -0.5)).astype(BF16).T
    qt_ref[half:, :] = (zq1 * (HEAD_DIM ** -0.5)).astype(BF16).T
    vat = va.astype(BF16).T
    for g in range(N_KV_HEADS):
        vt_ref[g * VT_ROWS:g * VT_ROWS + HEAD_DIM, BLK:] = vat[g * HEAD_DIM:(g + 1) * HEAD_DIM]

    src = lax.broadcasted_iota(jnp.int32, (BLK, BLK), 0)
    dst = lax.broadcasted_iota(jnp.int32, (BLK, BLK), 1)
    causal_t = src <= dst
    zero_blk = jnp.zeros((BLK, BLK), BF16)
    gatings = []
    for pg in range(GM_GROUPS // 2):
        g0, g1 = 2 * pg, 2 * pg + 1
        r0 = slice(g0 * GM_GROUP_DIM, (g0 + 1) * GM_GROUP_DIM)
        r1 = slice(g1 * GM_GROUP_DIM, (g1 + 1) * GM_GROUP_DIM)
        w0 = jnp.where(causal_t, wst_ref[g0], 0.0).astype(BF16)
        w1 = jnp.where(causal_t, wst_ref[g1], 0.0).astype(BF16)
        bd = jnp.concatenate([jnp.concatenate([w0, zero_blk], axis=1),
                              jnp.concatenate([zero_blk, w1], axis=1)], axis=0)
        lhs = jnp.concatenate(
            [jnp.concatenate([vn_ref[j * BLK:(j + 1) * BLK, r0].T, vn_ref[j * BLK:(j + 1) * BLK, r1].T],
                             axis=1) for j in range(n_blk)], axis=0)
        gatings.append(
            jnp.dot(lhs, bd, preferred_element_type=F32) + rows_ref[ROW_BS:ROW_BS + 1, g0 * BLK:(g1 + 1) * BLK])

    zu = (zu0, zu1)

    def gated_u(half_idx):
        for pg, sg in enumerate(gatings):
            for e in range(2):
                g = 2 * pg + e
                cols = slice(g * GM_GROUP_DIM, (g + 1) * GM_GROUP_DIM)
                zpart = zu[g * GM_GROUP_DIM // half]
                zcols = slice(g * GM_GROUP_DIM % half, g * GM_GROUP_DIM % half + GM_GROUP_DIM)
                for j in range(half_idx * (n_blk // 2), (half_idx + 1) * (n_blk // 2)):
                    cs = slice(j * BLK, (j + 1) * BLK)
                    yat_ref[cs, cols] = (
                        _gelu_tanh(zpart[cs, zcols]) * sg[cs, e * BLK:(e + 1) * BLK].T).astype(BF16)

    width = HEADS_PER_ITEM * BLK
    key = lax.broadcasted_iota(jnp.int32, (BLK, width), 0)
    qry = lax.broadcasted_iota(jnp.int32, (BLK, width), 1) & (BLK - 1)
    cur = key <= qry
    no_prev = jnp.where(t == 0, -jnp.inf, 0.0).astype(F32)
    zero_half = jnp.zeros((HEAD_DIM, width), BF16)

    items = [(j, g, p) for j in range(n_blk) for g in range(N_KV_HEADS)
             for p in range(Q_REP // HEADS_PER_ITEM)]

    def scores(item):
        j, g, p = item
        h0 = g * Q_REP + p * HEADS_PER_ITEM
        cs = slice(j * BLK, (j + 1) * BLK)
        kband = kb_ref[j * BLK:(j + 2) * BLK, (g // 2) * LANES:(g // 2 + 1) * LANES]
        qg = jnp.concatenate(
            [qt_ref[(h0 + e) * HEAD_DIM:(h0 + e + 1) * HEAD_DIM, cs] for e in range(HEADS_PER_ITEM)],
            axis=1)
        rhs = jnp.concatenate([qg, zero_half] if g % 2 == 0 else [zero_half, qg], axis=0)
        return jnp.dot(kband, rhs, preferred_element_type=F32)

    def softmax(item, st):
        j, g, p = item
        h0 = g * Q_REP + p * HEADS_PER_ITEM
        s_prev = st[:BLK]
        if j == 0:
            s_prev = s_prev + no_prev
        live = jnp.where(cur, st[BLK:], s_prev)
        sink = rows_ref[ROW_SINK:ROW_SINK + 1, h0 * BLK:h0 * BLK + width]
        m = jnp.maximum(jnp.max(live, axis=0, keepdims=True), sink)
        pr = jnp.exp(live - m)
        p_sink = jnp.exp(sink - m)
        pcat = jnp.concatenate([jnp.where(cur, 0.0, pr), jnp.where(cur, pr, 0.0)], axis=0)
        return pcat.astype(BF16), p_sink

    def attend(item, pcat, p_sink):
        j, g, p = item
        h0 = g * Q_REP + p * HEADS_PER_ITEM
        ot = jnp.dot(vt_ref[g * VT_ROWS:(g + 1) * VT_ROWS, j * BLK:(j + 2) * BLK], pcat,
                     preferred_element_type=F32)
        inv = 1.0 / (ot[HEAD_DIM:HEAD_DIM + 1] + p_sink)
        o = (ot[:HEAD_DIM] * inv).astype(BF16)
        for e in range(HEADS_PER_ITEM):
            ybt_ref[(h0 + e) * HEAD_DIM:(h0 + e + 1) * HEAD_DIM, j * BLK:(j + 1) * BLK] = (
                o[:, e * BLK:(e + 1) * BLK])

    def gate_a(half_idx):
        rs = slice(half_idx * (TS_MIX // 2), (half_idx + 1) * (TS_MIX // 2))
        gated_u(half_idx)
        acc = jnp.dot(yat_ref[rs, :], wa_ref[...], preferred_element_type=F32)
        mg_ref[rs, :] = mg_ref[rs, :] * acc

    n_items = len(items)
    fill = {n_items // 4 + ATT_LAG: 0, (3 * n_items) // 4 + ATT_LAG: 1}
    st, pc = {}, {}
    for step in range(n_items + 2 * ATT_LAG):
        if step < n_items:
            st[step] = scores(items[step])
        if step in fill:
            gate_a(fill[step])
        i = step - ATT_LAG
        if 0 <= i < n_items:
            pc[i] = softmax(items[i], st.pop(i))
        i = step - 2 * ATT_LAG
        if 0 <= i < n_items:
            attend(items[i], *pc.pop(i))

    kb_ref[0:BLK, :] = kb_ref[TS_MIX:TS_MIX + BLK, :]
    vt_ref[:, 0:BLK] = vt_ref[:, TS_MIX:TS_MIX + BLK]

    acc_b = lax.dot_general(ybt_ref[...], wb_ref[...], TN_DIMS, preferred_element_type=F32)
    merged = mg_ref[...] + sgb_ref[...].astype(F32) * acc_b
    y = jnp.dot(merged.astype(BF16), wo_ref[...], preferred_element_type=F32)
    o_ref[...] = x_ref[...] + mod_ref[2:3, :] * y


def _resident(shape):
    nd = len(shape)
    return pl.BlockSpec(shape, lambda *_: (0,) * nd, pipeline_mode=pl.Buffered(1))


def _mixer_call(x, mod, rows, w_in, ws_t, wa, wb, wo, w_up, w_down):
    batch, seq, _ = x.shape
    n_tiles = seq // TS_MIX
    n_steps = batch * n_tiles
    tile = pl.BlockSpec((None, TS_MIX, D_MODEL), lambda b, t: (b, t, 0))
    up_rows = pl.BlockSpec((D_MODEL // n_steps, D_FF), lambda b, t: (b * n_tiles + t, 0))
    dn_rows = pl.BlockSpec((D_FF // n_steps, D_MODEL), lambda b, t: (b * n_tiles + t, 0))
    return pl.pallas_call(
        _mixer_kernel,
        grid=(batch, n_tiles),
        in_specs=[
            tile,
            pl.BlockSpec((None, N_MOD, D_MODEL), lambda b, t: (b, 0, 0)),
            _resident((ROWS_SUBLANES, ROWS_WIDTH)),
            _resident((D_MODEL, IN_WIDTH)),
            _resident((GM_GROUPS, BLK, BLK)),
            _resident((D_MODEL, D_MODEL)),
            _resident((D_MODEL, D_MODEL)),
            _resident((D_MODEL, D_MODEL)),
            up_rows,
            dn_rows,
        ],
        out_specs=[tile, up_rows, dn_rows],
        out_shape=[jax.ShapeDtypeStruct(x.shape, F32),
                   jax.ShapeDtypeStruct(w_up.shape, BF16),
                   jax.ShapeDtypeStruct(w_down.shape, BF16)],
        scratch_shapes=[
            pltpu.VMEM((TS_MIX, D_MODEL), BF16),
            pltpu.VMEM((TS_MIX, D_MODEL), BF16),
            pltpu.VMEM((TS_MIX, D_MODEL), BF16),
            pltpu.VMEM((D_MODEL, TS_MIX), BF16),
            pltpu.VMEM((BLK + TS_MIX, KV_WIDTH), BF16),
            pltpu.VMEM((N_KV_HEADS * VT_ROWS, BLK + TS_MIX), BF16),
            pltpu.VMEM((D_MODEL, TS_MIX), BF16),
            pltpu.VMEM((TS_MIX, D_MODEL), F32),
            pltpu.VMEM((TS_MIX, D_MODEL), BF16),
        ],
        compiler_params=pltpu.CompilerParams(
            dimension_semantics=("arbitrary", "arbitrary"),
            vmem_limit_bytes=VMEM_LIMIT_BYTES),
        name="token_mixer",
    )(x, mod, rows, w_in, ws_t, wa, wb, wo, w_up, w_down)


def _mlp_kernel(x_ref, mod_ref, gmlp_ref, wup_ref, wdn_ref, gfin_ref, o_ref, hb_ref, act_ref):
    x = x_ref[...]
    h = _rms_modulate(x, gmlp_ref[...], mod_ref[4:5, :], mod_ref[3:4, :])
    hb_ref[...] = h.astype(BF16)
    for c in range(D_FF // D_MODEL):
        cs = slice(c * D_MODEL, (c + 1) * D_MODEL)
        a = jnp.maximum(_dot(hb_ref[...], wup_ref[:, cs]), 0.0)
        act_ref[:, cs] = (a * a).astype(BF16)
    x2 = x_ref[...] + mod_ref[5:6, :] * _dot(act_ref[...], wdn_ref[...])
    ms = jnp.mean(x2 * x2, axis=-1, keepdims=True)
    o_ref[...] = x2 * lax.rsqrt(ms + EPS) * gfin_ref[...]


def _mlp_call(x, mod, g_mlp, w_up, w_down, g_final):
    batch, seq, _ = x.shape
    tile = pl.BlockSpec((None, TM_MLP, D_MODEL), lambda b, t: (b, t, 0))
    return pl.pallas_call(
        _mlp_kernel,
        grid=(batch, seq // TM_MLP),
        in_specs=[
            tile,
            pl.BlockSpec((None, N_MOD, D_MODEL), lambda b, t: (b, 0, 0)),
            _resident((1, D_MODEL)),
            _resident((D_MODEL, D_FF)),
            _resident((D_FF, D_MODEL)),
            _resident((1, D_MODEL)),
        ],
        out_specs=tile,
        out_shape=jax.ShapeDtypeStruct(x.shape, F32),
        scratch_shapes=[
            pltpu.VMEM((TM_MLP, D_MODEL), BF16),
            pltpu.VMEM((TM_MLP, D_FF), BF16),
        ],
        compiler_params=pltpu.CompilerParams(
            dimension_semantics=("arbitrary", "arbitrary"),
            vmem_limit_bytes=VMEM_LIMIT_BYTES),
        name="channel_mlp",
    )(x, mod, g_mlp, w_up, w_down, g_final)


def kernel(x, c, w_ada, b_ada, g_norm_mix, w_in, b_in, gm_ln_g, gm_ln_b, gm_ws, gm_bs, attn_sinks,
           w_branch_a, w_branch_b, w_out, g_norm_mlp, w_up, w_down, g_final):
    batch = x.shape[0]
    depth = w_in.shape[0]
    row = lambda v: v.reshape(1, -1)
    for l in range(depth):
        mod = _ada_call(c, w_ada[l], row(b_ada[l])).reshape(batch, N_MOD, D_MODEL)
        bl = b_in[l]
        pad = lambda v: jnp.pad(v.reshape(-1), (0, ROWS_WIDTH - v.size))
        rows = jnp.zeros((ROWS_SUBLANES, ROWS_WIDTH), F32)
        rows = rows.at[ROW_GMIX].set(pad(g_norm_mix[l])).at[ROW_BIN].set(pad(bl))
        rows = rows.at[ROW_BS].set(pad(gm_bs[l])).at[ROW_SINK].set(pad(jnp.repeat(attn_sinks[l], BLK)))
        rows = rows.at[ROW_LNG].set(pad(gm_ln_g[l])).at[ROW_LNB].set(pad(gm_ln_b[l]))
        x, w_up_b, w_down_b = _mixer_call(
            x, mod, rows, w_in[l].astype(BF16), jnp.swapaxes(gm_ws[l], 1, 2),
            w_branch_a[l].astype(BF16), w_branch_b[l].astype(BF16), w_out[l].astype(BF16),
            w_up[l], w_down[l])
        assert depth == 1
        x = _mlp_call(x, mod, row(g_norm_mlp[l]), w_up_b, w_down_b, row(g_final))
    return x
```

```python
import jax
import jax.numpy as jnp
from jax import lax
from jax.experimental import pallas as pl
from jax.experimental.pallas import tpu as pltpu

D_MODEL = 1024
BLK = 128
GM_GROUPS = 8
GM_GROUP_DIM = D_MODEL // GM_GROUPS
N_Q_HEADS = 16
N_KV_HEADS = 4
HEAD_DIM = 64
Q_REP = N_Q_HEADS // N_KV_HEADS
KV_WIDTH = N_KV_HEADS * HEAD_DIM
D_FF = 4 * D_MODEL
N_MOD = 6
EPS = 1e-6

O_U = 0
O_V = O_U + D_MODEL
O_Q = O_V + D_MODEL
O_K = O_Q + D_MODEL
O_VA = O_K + KV_WIDTH
O_GA = O_VA + KV_WIDTH
O_GB = O_GA + D_MODEL
IN_WIDTH = O_GB + D_MODEL

T_Q = 0
T_VA = T_Q + D_MODEL
T_ROWS = T_VA + KV_WIDTH

LANES = 128
BF16_ROWS = 16
HEADS_PER_ITEM = 2
VT_ROWS = HEAD_DIM + BF16_ROWS

ROW_GMIX, ROW_BIN, ROW_BS, ROW_SINK, ROW_LNG, ROW_LNB = 0, 1, 2, 3, 4, 5
ROWS_SUBLANES = 8
ROWS_WIDTH = 8192
ATT_LAG = 6
TS_MIX = 512
TM_MLP = 1024
VMEM_LIMIT_BYTES = 56 * 1024 * 1024

F32 = jnp.float32
BF16 = jnp.bfloat16
NT_DIMS = (((1,), (1,)), ((), ()))
TN_DIMS = (((0,), (0,)), ((), ()))

GELU_C0 = 0.7978845608028654
GELU_C1 = GELU_C0 * 0.044715


def _sigmoid(z):
    return 0.5 * jnp.tanh(0.5 * z) + 0.5


def _gelu_tanh(z):
    hz = 0.5 * z
    return hz + hz * jnp.tanh(z * (GELU_C0 + GELU_C1 * (z * z)))


def _dot(a, b):
    return jnp.dot(a, b, preferred_element_type=F32)


def _rms_modulate(x, g, scale, shift):
    ms = jnp.mean(x * x, axis=-1, keepdims=True)
    h = x * lax.rsqrt(ms + EPS) * g
    return h * (1.0 + scale) + shift


def _lane_tile(col, n):
    return jnp.concatenate([col] * n, axis=1)


def _ada_kernel(c_ref, w_ref, b_ref, o_ref):
    c = c_ref[...]
    o_ref[...] = jnp.dot(c * _sigmoid(c), w_ref[...], preferred_element_type=F32) + b_ref[...]


def _ada_call(c, w_ada, b_ada):
    batch = c.shape[0]
    return pl.pallas_call(
        _ada_kernel,
        grid=(N_MOD,),
        in_specs=[
            pl.BlockSpec((batch, D_MODEL), lambda n: (0, 0)),
            pl.BlockSpec((D_MODEL, D_MODEL), lambda n: (0, n)),
            pl.BlockSpec((1, D_MODEL), lambda n: (0, n)),
        ],
        out_specs=pl.BlockSpec((batch, D_MODEL), lambda n: (0, n)),
        out_shape=jax.ShapeDtypeStruct((batch, N_MOD * D_MODEL), F32),
        name="ada_mod",
    )(c, w_ada, b_ada)


def _mixer_kernel(x_ref, mod_ref, rows_ref, win_ref, bt_ref,
                  wst_ref, wa_ref, wb_ref, wo_ref, wup_ref, wdn_ref,
                  o_ref, wupb_ref, wdnb_ref,
                  wt_ref, hb_ref, vn_ref, ya_ref, qt_ref, kb_ref, vt_ref, ybt_ref, mg_ref, sgb_ref):
    t = pl.program_id(1)
    n_blk = TS_MIX // BLK

    @pl.when(t == 0)
    def _():
        kb_ref[0:BLK, :] = jnp.zeros((BLK, KV_WIDTH), BF16)
        r = lax.broadcasted_iota(jnp.int32, vt_ref.shape, 0)
        ones_row = r == HEAD_DIM
        for g in range(1, N_KV_HEADS):
            ones_row = ones_row | (r == g * VT_ROWS + HEAD_DIM)
        vt_ref[...] = jnp.where(ones_row, 1.0, 0.0).astype(BF16)

    @pl.when((pl.program_id(0) == 0) & (t == 0))
    def _():
        for lo in range(0, D_MODEL, BLK * 4):
            wt_ref[T_Q + lo:T_Q + lo + BLK * 4, :] = win_ref[:, O_Q + lo:O_Q + lo + BLK * 4].T
        wt_ref[T_VA:T_ROWS, :] = win_ref[:, O_VA:O_GA].T

    wupb_ref[...] = wup_ref[...].astype(BF16)
    wdnb_ref[...] = wdn_ref[...].astype(BF16)

    x = x_ref[...]
    h = _rms_modulate(x, rows_ref[ROW_GMIX:ROW_GMIX + 1, :D_MODEL], mod_ref[1:2, :], mod_ref[0:1, :])
    hb_ref[...] = h.astype(BF16)

    def proj_t(lo, hi):
        z = lax.dot_general(wt_ref[lo:hi, :], hb_ref[...], NT_DIMS, preferred_element_type=F32)
        return z + _lane_tile(bt_ref[lo:hi, :], n_blk)

    def proj_n(lo, hi):
        z = jnp.dot(hb_ref[...], win_ref[:, lo:hi], preferred_element_type=F32)
        return z + rows_ref[ROW_BIN:ROW_BIN + 1, lo:hi]

    half = D_MODEL // 2
    zv0 = proj_n(O_V, O_V + half)
    zv1 = proj_n(O_V + half, O_Q)
    zu0 = proj_n(O_U, O_U + half)
    zu1 = proj_n(O_U + half, O_V)
    gv0 = _gelu_tanh(zv0)
    zq0 = proj_t(T_Q, T_Q + half)
    gv1 = _gelu_tanh(zv1)
    zq1 = proj_t(T_Q + half, T_VA)
    gv = jnp.concatenate([gv0, gv1], axis=1)
    mu = jnp.mean(gv, axis=-1, keepdims=True)
    gc = gv - mu
    var = jnp.mean(gc * gc, axis=-1, keepdims=True)
    vn = (gc * lax.rsqrt(var + EPS) * rows_ref[ROW_LNG:ROW_LNG + 1, :D_MODEL]
          + rows_ref[ROW_LNB:ROW_LNB + 1, :D_MODEL])
    vn_ref[...] = vn.astype(BF16)
    vat = proj_t(T_VA, T_ROWS)
    kb_ref[BLK:, :] = proj_n(O_K, O_VA).astype(BF16)
    mg_ref[:, :half] = _sigmoid(proj_n(O_GA, O_GA + half))
    mg_ref[:, half:] = _sigmoid(proj_n(O_GA + half, O_GB))
    sgb_ref[...] = _sigmoid(proj_n(O_GB, IN_WIDTH)).astype(BF16)
    qt_ref[:half, :] = (zq0 * (HEAD_DIM ** -0.5)).astype(BF16)
    qt_ref[half:, :] = (zq1 * (HEAD_DIM ** -0.5)).astype(BF16)
    for g in range(N_KV_HEADS):
        vt_ref[g * VT_ROWS:g * VT_ROWS + HEAD_DIM, BLK:] = (
            vat[g * HEAD_DIM:(g + 1) * HEAD_DIM].astype(BF16))

    src = lax.broadcasted_iota(jnp.int32, (BLK, BLK), 0)
    dst = lax.broadcasted_iota(jnp.int32, (BLK, BLK), 1)
    causal_t = src <= dst
    zero_blk = jnp.zeros((BLK, BLK), BF16)
    gatings = []
    for pg in range(GM_GROUPS // 2):
        g0, g1 = 2 * pg, 2 * pg + 1
        r0 = slice(g0 * GM_GROUP_DIM, (g0 + 1) * GM_GROUP_DIM)
        r1 = slice(g1 * GM_GROUP_DIM, (g1 + 1) * GM_GROUP_DIM)
        w0 = jnp.where(causal_t, wst_ref[g0], 0.0).astype(BF16)
        w1 = jnp.where(causal_t, wst_ref[g1], 0.0).astype(BF16)
        bd = jnp.concatenate([jnp.concatenate([w0, zero_blk], axis=1),
                              jnp.concatenate([zero_blk, w1], axis=1)], axis=0)
        lhs = jnp.concatenate(
            [jnp.concatenate([vn_ref[j * BLK:(j + 1) * BLK, r0].T, vn_ref[j * BLK:(j + 1) * BLK, r1].T],
                             axis=1) for j in range(n_blk)], axis=0)
        gatings.append(
            jnp.dot(lhs, bd, preferred_element_type=F32) + rows_ref[ROW_BS:ROW_BS + 1, g0 * BLK:(g1 + 1) * BLK])

    zu = (zu0, zu1)

    def gated_u(half_idx):
        for pg, sg in enumerate(gatings):
            for e in range(2):
                g = 2 * pg + e
                cols = slice(g * GM_GROUP_DIM, (g + 1) * GM_GROUP_DIM)
                zpart = zu[g * GM_GROUP_DIM // half]
                zcols = slice(g * GM_GROUP_DIM % half, g * GM_GROUP_DIM % half + GM_GROUP_DIM)
                for j in range(half_idx * (n_blk // 2), (half_idx + 1) * (n_blk // 2)):
                    cs = slice(j * BLK, (j + 1) * BLK)
                    ya_ref[cs, cols] = (
                        _gelu_tanh(zpart[cs, zcols]) * sg[cs, e * BLK:(e + 1) * BLK].T).astype(BF16)

    width = HEADS_PER_ITEM * BLK
    key = lax.broadcasted_iota(jnp.int32, (BLK, width), 0)
    qry = lax.broadcasted_iota(jnp.int32, (BLK, width), 1) & (BLK - 1)
    cur = key <= qry
    no_prev = jnp.where(t == 0, -jnp.inf, 0.0).astype(F32)
    zero_half = jnp.zeros((HEAD_DIM, width), BF16)

    items = [(j, g, p) for j in range(n_blk) for g in range(N_KV_HEADS)
             for p in range(Q_REP // HEADS_PER_ITEM)]

    def scores(item):
        j, g, p = item
        h0 = g * Q_REP + p * HEADS_PER_ITEM
        cs = slice(j * BLK, (j + 1) * BLK)
        kband = kb_ref[j * BLK:(j + 2) * BLK, (g // 2) * LANES:(g // 2 + 1) * LANES]
        qg = jnp.concatenate(
            [qt_ref[(h0 + e) * HEAD_DIM:(h0 + e + 1) * HEAD_DIM, cs] for e in range(HEADS_PER_ITEM)],
            axis=1)
        rhs = jnp.concatenate([qg, zero_half] if g % 2 == 0 else [zero_half, qg], axis=0)
        return jnp.dot(kband, rhs, preferred_element_type=F32)

    def softmax(item, st):
        j, g, p = item
        h0 = g * Q_REP + p * HEADS_PER_ITEM
        s_prev = st[:BLK]
        if j == 0:
            s_prev = s_prev + no_prev
        live = jnp.where(cur, st[BLK:], s_prev)
        sink = rows_ref[ROW_SINK:ROW_SINK + 1, h0 * BLK:h0 * BLK + width]
        m = jnp.maximum(jnp.max(live, axis=0, keepdims=True), sink)
        pr = jnp.exp(live - m)
        p_sink = jnp.exp(sink - m)
        pcat = jnp.concatenate([jnp.where(cur, 0.0, pr), jnp.where(cur, pr, 0.0)], axis=0)
        return pcat.astype(BF16), p_sink

    def attend(item, pcat, p_sink):
        j, g, p = item
        h0 = g * Q_REP + p * HEADS_PER_ITEM
        ot = jnp.dot(vt_ref[g * VT_ROWS:(g + 1) * VT_ROWS, j * BLK:(j + 2) * BLK], pcat,
                     preferred_element_type=F32)
        inv = 1.0 / (ot[HEAD_DIM:HEAD_DIM + 1] + p_sink)
        o = (ot[:HEAD_DIM] * inv).astype(BF16)
        for e in range(HEADS_PER_ITEM):
            ybt_ref[(h0 + e) * HEAD_DIM:(h0 + e + 1) * HEAD_DIM, j * BLK:(j + 1) * BLK] = (
                o[:, e * BLK:(e + 1) * BLK])

    def gate_a(half_idx):
        rs = slice(half_idx * (TS_MIX // 2), (half_idx + 1) * (TS_MIX // 2))
        gated_u(half_idx)
        acc = jnp.dot(ya_ref[rs, :], wa_ref[...], preferred_element_type=F32)
        mg_ref[rs, :] = mg_ref[rs, :] * acc

    n_items = len(items)
    fill = {n_items // 4 + ATT_LAG: 0, (3 * n_items) // 4 + ATT_LAG: 1}
    st, pc = {}, {}
    for step in range(n_items + 2 * ATT_LAG):
        if step < n_items:
            st[step] = scores(items[step])
        if step in fill:
            gate_a(fill[step])
        i = step - ATT_LAG
        if 0 <= i < n_items:
            pc[i] = softmax(items[i], st.pop(i))
        i = step - 2 * ATT_LAG
        if 0 <= i < n_items:
            attend(items[i], *pc.pop(i))

    kb_ref[0:BLK, :] = kb_ref[TS_MIX:TS_MIX + BLK, :]
    vt_ref[:, 0:BLK] = vt_ref[:, TS_MIX:TS_MIX + BLK]

    acc_b = lax.dot_general(ybt_ref[...], wb_ref[...], TN_DIMS, preferred_element_type=F32)
    merged = mg_ref[...] + sgb_ref[...].astype(F32) * acc_b
    y = jnp.dot(merged.astype(BF16), wo_ref[...], preferred_element_type=F32)
    o_ref[...] = x_ref[...] + mod_ref[2:3, :] * y


def _resident(shape):
    nd = len(shape)
    return pl.BlockSpec(shape, lambda *_: (0,) * nd, pipeline_mode=pl.Buffered(1))


def _mixer_call(x, mod, rows, w_in, b_t, ws_t, wa, wb, wo, w_up, w_down):
    batch, seq, _ = x.shape
    n_tiles = seq // TS_MIX
    n_steps = batch * n_tiles
    tile = pl.BlockSpec((None, TS_MIX, D_MODEL), lambda b, t: (b, t, 0))
    up_rows = pl.BlockSpec((D_MODEL // n_steps, D_FF), lambda b, t: (b * n_tiles + t, 0))
    dn_rows = pl.BlockSpec((D_FF // n_steps, D_MODEL), lambda b, t: (b * n_tiles + t, 0))
    return pl.pallas_call(
        _mixer_kernel,
        grid=(batch, n_tiles),
        in_specs=[
            tile,
            pl.BlockSpec((None, N_MOD, D_MODEL), lambda b, t: (b, 0, 0)),
            _resident((ROWS_SUBLANES, ROWS_WIDTH)),
            _resident((D_MODEL, IN_WIDTH)),
            _resident((T_ROWS, LANES)),
            _resident((GM_GROUPS, BLK, BLK)),
            _resident((D_MODEL, D_MODEL)),
            _resident((D_MODEL, D_MODEL)),
            _resident((D_MODEL, D_MODEL)),
            up_rows,
            dn_rows,
        ],
        out_specs=[tile, up_rows, dn_rows],
        out_shape=[jax.ShapeDtypeStruct(x.shape, F32),
                   jax.ShapeDtypeStruct(w_up.shape, BF16),
                   jax.ShapeDtypeStruct(w_down.shape, BF16)],
        scratch_shapes=[
            pltpu.VMEM((T_ROWS, D_MODEL), BF16),
            pltpu.VMEM((TS_MIX, D_MODEL), BF16),
            pltpu.VMEM((TS_MIX, D_MODEL), BF16),
            pltpu.VMEM((TS_MIX, D_MODEL), BF16),
            pltpu.VMEM((D_MODEL, TS_MIX), BF16),
            pltpu.VMEM((BLK + TS_MIX, KV_WIDTH), BF16),
            pltpu.VMEM((N_KV_HEADS * VT_ROWS, BLK + TS_MIX), BF16),
            pltpu.VMEM((D_MODEL, TS_MIX), BF16),
            pltpu.VMEM((TS_MIX, D_MODEL), F32),
            pltpu.VMEM((TS_MIX, D_MODEL), BF16),
        ],
        compiler_params=pltpu.CompilerParams(
            dimension_semantics=("arbitrary", "arbitrary"),
            vmem_limit_bytes=VMEM_LIMIT_BYTES),
        name="token_mixer",
    )(x, mod, rows, w_in, b_t, ws_t, wa, wb, wo, w_up, w_down)


def _mlp_kernel(x_ref, mod_ref, gmlp_ref, wup_ref, wdn_ref, gfin_ref, o_ref, hb_ref, act_ref):
    x = x_ref[...]
    h = _rms_modulate(x, gmlp_ref[...], mod_ref[4:5, :], mod_ref[3:4, :])
    hb_ref[...] = h.astype(BF16)
    for c in range(D_FF // D_MODEL):
        cs = slice(c * D_MODEL, (c + 1) * D_MODEL)
        a = jnp.maximum(_dot(hb_ref[...], wup_ref[:, cs]), 0.0)
        act_ref[:, cs] = (a * a).astype(BF16)
    x2 = x_ref[...] + mod_ref[5:6, :] * _dot(act_ref[...], wdn_ref[...])
    ms = jnp.mean(x2 * x2, axis=-1, keepdims=True)
    o_ref[...] = x2 * lax.rsqrt(ms + EPS) * gfin_ref[...]


def _mlp_call(x, mod, g_mlp, w_up, w_down, g_final):
    batch, seq, _ = x.shape
    tile = pl.BlockSpec((None, TM_MLP, D_MODEL), lambda b, t: (b, t, 0))
    return pl.pallas_call(
        _mlp_kernel,
        grid=(batch, seq // TM_MLP),
        in_specs=[
            tile,
            pl.BlockSpec((None, N_MOD, D_MODEL), lambda b, t: (b, 0, 0)),
            _resident((1, D_MODEL)),
            _resident((D_MODEL, D_FF)),
            _resident((D_FF, D_MODEL)),
            _resident((1, D_MODEL)),
        ],
        out_specs=tile,
        out_shape=jax.ShapeDtypeStruct(x.shape, F32),
        scratch_shapes=[
            pltpu.VMEM((TM_MLP, D_MODEL), BF16),
            pltpu.VMEM((TM_MLP, D_FF), BF16),
        ],
        compiler_params=pltpu.CompilerParams(
            dimension_semantics=("arbitrary", "arbitrary"),
            vmem_limit_bytes=VMEM_LIMIT_BYTES),
        name="channel_mlp",
    )(x, mod, g_mlp, w_up, w_down, g_final)


def kernel(x, c, w_ada, b_ada, g_norm_mix, w_in, b_in, gm_ln_g, gm_ln_b, gm_ws, gm_bs, attn_sinks,
           w_branch_a, w_branch_b, w_out, g_norm_mlp, w_up, w_down, g_final):
    batch = x.shape[0]
    depth = w_in.shape[0]
    row = lambda v: v.reshape(1, -1)
    lane_col = lambda v: jnp.broadcast_to(v[:, None], (v.shape[0], LANES))
    for l in range(depth):
        mod = _ada_call(c, w_ada[l], row(b_ada[l])).reshape(batch, N_MOD, D_MODEL)
        bl = b_in[l]
        b_t = lane_col(jnp.concatenate([bl[O_Q:O_K], bl[O_VA:O_GA]]))
        pad = lambda v: jnp.pad(v.reshape(-1), (0, ROWS_WIDTH - v.size))
        rows = jnp.zeros((ROWS_SUBLANES, ROWS_WIDTH), F32)
        rows = rows.at[ROW_GMIX].set(pad(g_norm_mix[l])).at[ROW_BIN].set(pad(bl))
        rows = rows.at[ROW_BS].set(pad(gm_bs[l])).at[ROW_SINK].set(pad(jnp.repeat(attn_sinks[l], BLK)))
        rows = rows.at[ROW_LNG].set(pad(gm_ln_g[l])).at[ROW_LNB].set(pad(gm_ln_b[l]))
        x, w_up_b, w_down_b = _mixer_call(
            x, mod, rows, w_in[l].astype(BF16), b_t,
            jnp.swapaxes(gm_ws[l], 1, 2),
            w_branch_a[l].astype(BF16), w_branch_b[l].astype(BF16), w_out[l].astype(BF16),
            w_up[l], w_down[l])
        assert depth == 1
        x = _mlp_call(x, mod, row(g_norm_mlp[l]), w_up_b, w_down_b, row(g_final))
    return x
```

```python
import jax
import jax.numpy as jnp
from jax import lax
from jax.experimental import pallas as pl
from jax.experimental.pallas import tpu as pltpu

D_MODEL = 1024
BLK = 128
GM_GROUPS = 8
GM_GROUP_DIM = D_MODEL // GM_GROUPS
N_Q_HEADS = 16
N_KV_HEADS = 4
HEAD_DIM = 64
Q_REP = N_Q_HEADS // N_KV_HEADS
KV_WIDTH = N_KV_HEADS * HEAD_DIM
D_FF = 4 * D_MODEL
N_MOD = 6
EPS = 1e-6

O_U = 0
O_V = O_U + D_MODEL
O_Q = O_V + D_MODEL
O_K = O_Q + D_MODEL
O_VA = O_K + KV_WIDTH
O_GA = O_VA + KV_WIDTH
O_GB = O_GA + D_MODEL
IN_WIDTH = O_GB + D_MODEL

T_Q = 0
T_VA = T_Q + D_MODEL
T_ROWS = T_VA + KV_WIDTH

LANES = 128
BF16_ROWS = 16
HEADS_PER_ITEM = 2
VT_ROWS = HEAD_DIM + BF16_ROWS

ROW_GMIX, ROW_BIN, ROW_SINK, ROW_LNG, ROW_LNB = 0, 1, 2, 3, 4
ROWS_SUBLANES = 8
ROWS_WIDTH = 8192
ATT_LAG = 6
TS_MIX = 512
TM_MLP = 1024
VMEM_LIMIT_BYTES = 56 * 1024 * 1024

F32 = jnp.float32
BF16 = jnp.bfloat16
NT_DIMS = (((1,), (1,)), ((), ()))
TN_DIMS = (((0,), (0,)), ((), ()))

GELU_C0 = 0.7978845608028654
GELU_C1 = GELU_C0 * 0.044715


def _sigmoid(z):
    return 0.5 * jnp.tanh(0.5 * z) + 0.5


def _gelu_tanh(z):
    hz = 0.5 * z
    return hz + hz * jnp.tanh(z * (GELU_C0 + GELU_C1 * (z * z)))


def _dot(a, b):
    return jnp.dot(a, b, preferred_element_type=F32)


def _rms_modulate(x, g, scale, shift):
    ms = jnp.mean(x * x, axis=-1, keepdims=True)
    h = x * lax.rsqrt(ms + EPS) * g
    return h * (1.0 + scale) + shift


def _lane_tile(col, n):
    return jnp.concatenate([col] * n, axis=1)


def _ada_kernel(c_ref, w_ref, b_ref, o_ref):
    c = c_ref[...]
    o_ref[...] = jnp.dot(c * _sigmoid(c), w_ref[...], preferred_element_type=F32) + b_ref[...]


def _ada_call(c, w_ada, b_ada):
    batch = c.shape[0]
    return pl.pallas_call(
        _ada_kernel,
        grid=(N_MOD,),
        in_specs=[
            pl.BlockSpec((batch, D_MODEL), lambda n: (0, 0)),
            pl.BlockSpec((D_MODEL, D_MODEL), lambda n: (0, n)),
            pl.BlockSpec((1, D_MODEL), lambda n: (0, n)),
        ],
        out_specs=pl.BlockSpec((batch, D_MODEL), lambda n: (0, n)),
        out_shape=jax.ShapeDtypeStruct((batch, N_MOD * D_MODEL), F32),
        name="ada_mod",
    )(c, w_ada, b_ada)


def _mixer_kernel(x_ref, mod_ref, rows_ref, win_ref, bt_ref,
                  ws_ref, bst_ref, wa_ref, wb_ref, wo_ref, wup_ref, wdn_ref,
                  o_ref, wupb_ref, wdnb_ref,
                  wt_ref, hb_ref, vn_ref, ya_ref, qt_ref, kb_ref, vt_ref, ybt_ref, mg_ref, sgb_ref):
    t = pl.program_id(1)
    n_blk = TS_MIX // BLK

    @pl.when(t == 0)
    def _():
        kb_ref[0:BLK, :] = jnp.zeros((BLK, KV_WIDTH), BF16)
        r = lax.broadcasted_iota(jnp.int32, vt_ref.shape, 0)
        ones_row = r == HEAD_DIM
        for g in range(1, N_KV_HEADS):
            ones_row = ones_row | (r == g * VT_ROWS + HEAD_DIM)
        vt_ref[...] = jnp.where(ones_row, 1.0, 0.0).astype(BF16)

    @pl.when((pl.program_id(0) == 0) & (t == 0))
    def _():
        for lo in range(0, D_MODEL, BLK * 4):
            wt_ref[T_Q + lo:T_Q + lo + BLK * 4, :] = win_ref[:, O_Q + lo:O_Q + lo + BLK * 4].T
        wt_ref[T_VA:T_ROWS, :] = win_ref[:, O_VA:O_GA].T

    wupb_ref[...] = wup_ref[...].astype(BF16)
    wdnb_ref[...] = wdn_ref[...].astype(BF16)

    x = x_ref[...]
    h = _rms_modulate(x, rows_ref[ROW_GMIX:ROW_GMIX + 1, :D_MODEL], mod_ref[1:2, :], mod_ref[0:1, :])
    hb_ref[...] = h.astype(BF16)

    def proj_t(lo, hi):
        z = lax.dot_general(wt_ref[lo:hi, :], hb_ref[...], NT_DIMS, preferred_element_type=F32)
        return z + _lane_tile(bt_ref[lo:hi, :], n_blk)

    def proj_n(lo, hi):
        z = jnp.dot(hb_ref[...], win_ref[:, lo:hi], preferred_element_type=F32)
        return z + rows_ref[ROW_BIN:ROW_BIN + 1, lo:hi]

    half = D_MODEL // 2
    zv0 = proj_n(O_V, O_V + half)
    zv1 = proj_n(O_V + half, O_Q)
    zu0 = proj_n(O_U, O_U + half)
    zu1 = proj_n(O_U + half, O_V)
    gv0 = _gelu_tanh(zv0)
    zq0 = proj_t(T_Q, T_Q + half)
    gv1 = _gelu_tanh(zv1)
    zq1 = proj_t(T_Q + half, T_VA)
    gv = jnp.concatenate([gv0, gv1], axis=1)
    mu = jnp.mean(gv, axis=-1, keepdims=True)
    gc = gv - mu
    var = jnp.mean(gc * gc, axis=-1, keepdims=True)
    vn = (gc * lax.rsqrt(var + EPS) * rows_ref[ROW_LNG:ROW_LNG + 1, :D_MODEL]
          + rows_ref[ROW_LNB:ROW_LNB + 1, :D_MODEL])
    vn_ref[...] = vn.astype(BF16)
    vat = proj_t(T_VA, T_ROWS)
    kb_ref[BLK:, :] = proj_n(O_K, O_VA).astype(BF16)
    mg_ref[:, :half] = _sigmoid(proj_n(O_GA, O_GA + half))
    mg_ref[:, half:] = _sigmoid(proj_n(O_GA + half, O_GB))
    sgb_ref[...] = _sigmoid(proj_n(O_GB, IN_WIDTH)).astype(BF16)
    qt_ref[:half, :] = (zq0 * (HEAD_DIM ** -0.5)).astype(BF16)
    qt_ref[half:, :] = (zq1 * (HEAD_DIM ** -0.5)).astype(BF16)
    for g in range(N_KV_HEADS):
        vt_ref[g * VT_ROWS:g * VT_ROWS + HEAD_DIM, BLK:] = (
            vat[g * HEAD_DIM:(g + 1) * HEAD_DIM].astype(BF16))

    dst = lax.broadcasted_iota(jnp.int32, (BLK, BLK), 0)
    src = lax.broadcasted_iota(jnp.int32, (BLK, BLK), 1)
    causal = dst >= src
    zero_blk = jnp.zeros((BLK, BLK), BF16)
    gatings = {}
    for pg in range(GM_GROUPS // 2):
        g0, g1 = 2 * pg, 2 * pg + 1
        w_cat = jnp.concatenate([jnp.where(causal, ws_ref[g0], 0.0), jnp.where(causal, ws_ref[g1], 0.0)],
                                axis=1).astype(BF16)
        b0 = bst_ref[:, g0:g0 + 1]
        b1 = bst_ref[:, g1:g1 + 1]
        for j in range(n_blk):
            rs = slice(j * BLK, (j + 1) * BLK)
            v0 = vn_ref[rs, g0 * GM_GROUP_DIM:(g0 + 1) * GM_GROUP_DIM]
            v1 = vn_ref[rs, g1 * GM_GROUP_DIM:(g1 + 1) * GM_GROUP_DIM]
            bd = jnp.concatenate([jnp.concatenate([v0, zero_blk], axis=1),
                                  jnp.concatenate([zero_blk, v1], axis=1)], axis=0)
            s = jnp.dot(w_cat, bd, preferred_element_type=F32)
            gatings[(pg, j)] = (s[:, :BLK] + b0, s[:, BLK:] + b1)

    zu = (zu0, zu1)

    def gated_u(half_idx):
        for pg in range(GM_GROUPS // 2):
            for e in range(2):
                g = 2 * pg + e
                cols = slice(g * GM_GROUP_DIM, (g + 1) * GM_GROUP_DIM)
                zpart = zu[g * GM_GROUP_DIM // half]
                zcols = slice(g * GM_GROUP_DIM % half, g * GM_GROUP_DIM % half + GM_GROUP_DIM)
                for j in range(half_idx * (n_blk // 2), (half_idx + 1) * (n_blk // 2)):
                    cs = slice(j * BLK, (j + 1) * BLK)
                    ya_ref[cs, cols] = (_gelu_tanh(zpart[cs, zcols]) * gatings[(pg, j)][e]).astype(BF16)

    width = HEADS_PER_ITEM * BLK
    key = lax.broadcasted_iota(jnp.int32, (BLK, width), 0)
    qry = lax.broadcasted_iota(jnp.int32, (BLK, width), 1) & (BLK - 1)
    cur = key <= qry
    no_prev = jnp.where(t == 0, -jnp.inf, 0.0).astype(F32)
    zero_half = jnp.zeros((HEAD_DIM, width), BF16)

    items = [(j, g, p) for j in range(n_blk) for g in range(N_KV_HEADS)
             for p in range(Q_REP // HEADS_PER_ITEM)]

    def scores(item):
        j, g, p = item
        h0 = g * Q_REP + p * HEADS_PER_ITEM
        cs = slice(j * BLK, (j + 1) * BLK)
        kband = kb_ref[j * BLK:(j + 2) * BLK, (g // 2) * LANES:(g // 2 + 1) * LANES]
        qg = jnp.concatenate(
            [qt_ref[(h0 + e) * HEAD_DIM:(h0 + e + 1) * HEAD_DIM, cs] for e in range(HEADS_PER_ITEM)],
            axis=1)
        rhs = jnp.concatenate([qg, zero_half] if g % 2 == 0 else [zero_half, qg], axis=0)
        return jnp.dot(kband, rhs, preferred_element_type=F32)

    def softmax(item, st):
        j, g, p = item
        h0 = g * Q_REP + p * HEADS_PER_ITEM
        s_prev = st[:BLK]
        if j == 0:
            s_prev = s_prev + no_prev
        live = jnp.where(cur, st[BLK:], s_prev)
        sink = rows_ref[ROW_SINK:ROW_SINK + 1, h0 * BLK:h0 * BLK + width]
        m = jnp.maximum(jnp.max(live, axis=0, keepdims=True), sink)
        pr = jnp.exp(live - m)
        p_sink = jnp.exp(sink - m)
        pcat = jnp.concatenate([jnp.where(cur, 0.0, pr), jnp.where(cur, pr, 0.0)], axis=0)
        return pcat.astype(BF16), p_sink

    def attend(item, pcat, p_sink):
        j, g, p = item
        h0 = g * Q_REP + p * HEADS_PER_ITEM
        ot = jnp.dot(vt_ref[g * VT_ROWS:(g + 1) * VT_ROWS, j * BLK:(j + 2) * BLK], pcat,
                     preferred_element_type=F32)
        inv = 1.0 / (ot[HEAD_DIM:HEAD_DIM + 1] + p_sink)
        o = (ot[:HEAD_DIM] * inv).astype(BF16)
        for e in range(HEADS_PER_ITEM):
            ybt_ref[(h0 + e) * HEAD_DIM:(h0 + e + 1) * HEAD_DIM, j * BLK:(j + 1) * BLK] = (
                o[:, e * BLK:(e + 1) * BLK])

    def gate_a(half_idx):
        rs = slice(half_idx * (TS_MIX // 2), (half_idx + 1) * (TS_MIX // 2))
        gated_u(half_idx)
        acc = jnp.dot(ya_ref[rs, :], wa_ref[...], preferred_element_type=F32)
        mg_ref[rs, :] = mg_ref[rs, :] * acc

    n_items = len(items)
    fill = {n_items // 4 + ATT_LAG: 0, (3 * n_items) // 4 + ATT_LAG: 1}
    st, pc = {}, {}
    for step in range(n_items + 2 * ATT_LAG):
        if step < n_items:
            st[step] = scores(items[step])
        if step in fill:
            gate_a(fill[step])
        i = step - ATT_LAG
        if 0 <= i < n_items:
            pc[i] = softmax(items[i], st.pop(i))
        i = step - 2 * ATT_LAG
        if 0 <= i < n_items:
            attend(items[i], *pc.pop(i))

    kb_ref[0:BLK, :] = kb_ref[TS_MIX:TS_MIX + BLK, :]
    vt_ref[:, 0:BLK] = vt_ref[:, TS_MIX:TS_MIX + BLK]

    acc_b = lax.dot_general(ybt_ref[...], wb_ref[...], TN_DIMS, preferred_element_type=F32)
    merged = mg_ref[...] + sgb_ref[...].astype(F32) * acc_b
    y = jnp.dot(merged.astype(BF16), wo_ref[...], preferred_element_type=F32)
    o_ref[...] = x_ref[...] + mod_ref[2:3, :] * y


def _resident(shape):
    nd = len(shape)
    return pl.BlockSpec(shape, lambda *_: (0,) * nd, pipeline_mode=pl.Buffered(1))


def _mixer_call(x, mod, rows, w_in, b_t, ws, bs_t, wa, wb, wo, w_up, w_down):
    batch, seq, _ = x.shape
    n_tiles = seq // TS_MIX
    n_steps = batch * n_tiles
    tile = pl.BlockSpec((None, TS_MIX, D_MODEL), lambda b, t: (b, t, 0))
    up_rows = pl.BlockSpec((D_MODEL // n_steps, D_FF), lambda b, t: (b * n_tiles + t, 0))
    dn_rows = pl.BlockSpec((D_FF // n_steps, D_MODEL), lambda b, t: (b * n_tiles + t, 0))
    return pl.pallas_call(
        _mixer_kernel,
        grid=(batch, n_tiles),
        in_specs=[
            tile,
            pl.BlockSpec((None, N_MOD, D_MODEL), lambda b, t: (b, 0, 0)),
            _resident((ROWS_SUBLANES, ROWS_WIDTH)),
            _resident((D_MODEL, IN_WIDTH)),
            _resident((T_ROWS, LANES)),
            _resident((GM_GROUPS, BLK, BLK)),
            _resident((BLK, GM_GROUPS)),
            _resident((D_MODEL, D_MODEL)),
            _resident((D_MODEL, D_MODEL)),
            _resident((D_MODEL, D_MODEL)),
            up_rows,
            dn_rows,
        ],
        out_specs=[tile, up_rows, dn_rows],
        out_shape=[jax.ShapeDtypeStruct(x.shape, F32),
                   jax.ShapeDtypeStruct(w_up.shape, BF16),
                   jax.ShapeDtypeStruct(w_down.shape, BF16)],
        scratch_shapes=[
            pltpu.VMEM((T_ROWS, D_MODEL), BF16),
            pltpu.VMEM((TS_MIX, D_MODEL), BF16),
            pltpu.VMEM((TS_MIX, D_MODEL), BF16),
            pltpu.VMEM((TS_MIX, D_MODEL), BF16),
            pltpu.VMEM((D_MODEL, TS_MIX), BF16),
            pltpu.VMEM((BLK + TS_MIX, KV_WIDTH), BF16),
            pltpu.VMEM((N_KV_HEADS * VT_ROWS, BLK + TS_MIX), BF16),
            pltpu.VMEM((D_MODEL, TS_MIX), BF16),
            pltpu.VMEM((TS_MIX, D_MODEL), F32),
            pltpu.VMEM((TS_MIX, D_MODEL), BF16),
        ],
        compiler_params=pltpu.CompilerParams(
            dimension_semantics=("arbitrary", "arbitrary"),
            vmem_limit_bytes=VMEM_LIMIT_BYTES),
        name="token_mixer",
    )(x, mod, rows, w_in, b_t, ws, bs_t, wa, wb, wo, w_up, w_down)


def _mlp_kernel(x_ref, mod_ref, gmlp_ref, wup_ref, wdn_ref, gfin_ref, o_ref, hb_ref, act_ref):
    x = x_ref[...]
    h = _rms_modulate(x, gmlp_ref[...], mod_ref[4:5, :], mod_ref[3:4, :])
    hb_ref[...] = h.astype(BF16)
    for c in range(D_FF // D_MODEL):
        cs = slice(c * D_MODEL, (c + 1) * D_MODEL)
        a = jnp.maximum(_dot(hb_ref[...], wup_ref[:, cs]), 0.0)
        act_ref[:, cs] = (a * a).astype(BF16)
    x2 = x_ref[...] + mod_ref[5:6, :] * _dot(act_ref[...], wdn_ref[...])
    ms = jnp.mean(x2 * x2, axis=-1, keepdims=True)
    o_ref[...] = x2 * lax.rsqrt(ms + EPS) * gfin_ref[...]


def _mlp_call(x, mod, g_mlp, w_up, w_down, g_final):
    batch, seq, _ = x.shape
    tile = pl.BlockSpec((None, TM_MLP, D_MODEL), lambda b, t: (b, t, 0))
    return pl.pallas_call(
        _mlp_kernel,
        grid=(batch, seq // TM_MLP),
        in_specs=[
            tile,
            pl.BlockSpec((None, N_MOD, D_MODEL), lambda b, t: (b, 0, 0)),
            _resident((1, D_MODEL)),
            _resident((D_MODEL, D_FF)),
            _resident((D_FF, D_MODEL)),
            _resident((1, D_MODEL)),
        ],
        out_specs=tile,
        out_shape=jax.ShapeDtypeStruct(x.shape, F32),
        scratch_shapes=[
            pltpu.VMEM((TM_MLP, D_MODEL), BF16),
            pltpu.VMEM((TM_MLP, D_FF), BF16),
        ],
        compiler_params=pltpu.CompilerParams(
            dimension_semantics=("arbitrary", "arbitrary"),
            vmem_limit_bytes=VMEM_LIMIT_BYTES),
        name="channel_mlp",
    )(x, mod, g_mlp, w_up, w_down, g_final)


def kernel(x, c, w_ada, b_ada, g_norm_mix, w_in, b_in, gm_ln_g, gm_ln_b, gm_ws, gm_bs, attn_sinks,
           w_branch_a, w_branch_b, w_out, g_norm_mlp, w_up, w_down, g_final):
    batch = x.shape[0]
    depth = w_in.shape[0]
    row = lambda v: v.reshape(1, -1)
    lane_col = lambda v: jnp.broadcast_to(v[:, None], (v.shape[0], LANES))
    for l in range(depth):
        mod = _ada_call(c, w_ada[l], row(b_ada[l])).reshape(batch, N_MOD, D_MODEL)
        bl = b_in[l]
        b_t = lane_col(jnp.concatenate([bl[O_Q:O_K], bl[O_VA:O_GA]]))
        pad = lambda v: jnp.pad(v.reshape(-1), (0, ROWS_WIDTH - v.size))
        rows = jnp.zeros((ROWS_SUBLANES, ROWS_WIDTH), F32)
        rows = rows.at[ROW_GMIX].set(pad(g_norm_mix[l])).at[ROW_BIN].set(pad(bl))
        rows = rows.at[ROW_SINK].set(pad(jnp.repeat(attn_sinks[l], BLK)))
        rows = rows.at[ROW_LNG].set(pad(gm_ln_g[l])).at[ROW_LNB].set(pad(gm_ln_b[l]))
        x, w_up_b, w_down_b = _mixer_call(
            x, mod, rows, w_in[l].astype(BF16), b_t,
            gm_ws[l], gm_bs[l].T,
            w_branch_a[l].astype(BF16), w_branch_b[l].astype(BF16), w_out[l].astype(BF16),
            w_up[l], w_down[l])
        assert depth == 1
        x = _mlp_call(x, mod, row(g_norm_mlp[l]), w_up_b, w_down_b, row(g_final))
    return x
```

```python
import jax
import jax.numpy as jnp
from jax import lax
from jax.experimental import pallas as pl
from jax.experimental.pallas import tpu as pltpu

D_MODEL = 1024
BLK = 128
GM_GROUPS = 8
GM_GROUP_DIM = D_MODEL // GM_GROUPS
N_Q_HEADS = 16
N_KV_HEADS = 4
HEAD_DIM = 64
Q_REP = N_Q_HEADS // N_KV_HEADS
KV_WIDTH = N_KV_HEADS * HEAD_DIM
D_FF = 4 * D_MODEL
N_MOD = 6
EPS = 1e-6

O_U = 0
O_V = O_U + D_MODEL
O_Q = O_V + D_MODEL
O_K = O_Q + D_MODEL
O_VA = O_K + KV_WIDTH
O_GA = O_VA + KV_WIDTH
O_GB = O_GA + D_MODEL
IN_WIDTH = O_GB + D_MODEL

T_Q = 0
T_VA = T_Q + D_MODEL
T_ROWS = T_VA + KV_WIDTH

LANES = 128
BF16_ROWS = 16
HEADS_PER_ITEM = 2
VT_ROWS = HEAD_DIM + BF16_ROWS

ROW_GMIX, ROW_BIN, ROW_BS, ROW_SINK, ROW_LNG, ROW_LNB = 0, 1, 2, 3, 4, 5
ROWS_SUBLANES = 8
ROWS_WIDTH = 8192
ATT_LAG = 6
TS_MIX = 512
TM_MLP = 1024
VMEM_LIMIT_BYTES = 56 * 1024 * 1024

F32 = jnp.float32
BF16 = jnp.bfloat16
NT_DIMS = (((1,), (1,)), ((), ()))
TN_DIMS = (((0,), (0,)), ((), ()))

GELU_C0 = 0.7978845608028654
GELU_C1 = GELU_C0 * 0.044715


def _sigmoid(z):
    return 0.5 * jnp.tanh(0.5 * z) + 0.5


def _gelu_tanh(z):
    hz = 0.5 * z
    return hz + hz * jnp.tanh(z * (GELU_C0 + GELU_C1 * (z * z)))


def _dot(a, b):
    return jnp.dot(a, b, preferred_element_type=F32)


def _rms_modulate(x, g, scale, shift):
    ms = jnp.mean(x * x, axis=-1, keepdims=True)
    h = x * lax.rsqrt(ms + EPS) * g
    return h * (1.0 + scale) + shift


def _lane_tile(col, n):
    return jnp.concatenate([col] * n, axis=1)


def _ada_kernel(c_ref, w_ref, b_ref, o_ref):
    c = c_ref[...]
    o_ref[...] = jnp.dot(c * _sigmoid(c), w_ref[...], preferred_element_type=F32) + b_ref[...]


def _ada_call(c, w_ada, b_ada):
    batch = c.shape[0]
    return pl.pallas_call(
        _ada_kernel,
        grid=(N_MOD,),
        in_specs=[
            pl.BlockSpec((batch, D_MODEL), lambda n: (0, 0)),
            pl.BlockSpec((D_MODEL, D_MODEL), lambda n: (0, n)),
            pl.BlockSpec((1, D_MODEL), lambda n: (0, n)),
        ],
        out_specs=pl.BlockSpec((batch, D_MODEL), lambda n: (0, n)),
        out_shape=jax.ShapeDtypeStruct((batch, N_MOD * D_MODEL), F32),
        name="ada_mod",
    )(c, w_ada, b_ada)


def _mixer_kernel(x_ref, mod_ref, rows_ref, win_ref, bt_ref,
                  wst_ref, wa_ref, wb_ref, wo_ref, wup_ref, wdn_ref,
                  o_ref, wupb_ref, wdnb_ref,
                  wt_ref, hb_ref, vn_ref, ya_ref, qt_ref, kb_ref, vt_ref, ybt_ref, mg_ref, sgb_ref):
    t = pl.program_id(1)
    n_blk = TS_MIX // BLK

    @pl.when(t == 0)
    def _():
        kb_ref[0:BLK, :] = jnp.zeros((BLK, KV_WIDTH), BF16)
        r = lax.broadcasted_iota(jnp.int32, vt_ref.shape, 0)
        ones_row = r == HEAD_DIM
        for g in range(1, N_KV_HEADS):
            ones_row = ones_row | (r == g * VT_ROWS + HEAD_DIM)
        vt_ref[...] = jnp.where(ones_row, 1.0, 0.0).astype(BF16)

    @pl.when((pl.program_id(0) == 0) & (t == 0))
    def _():
        for lo in range(0, D_MODEL, BLK * 4):
            wt_ref[T_Q + lo:T_Q + lo + BLK * 4, :] = (
                win_ref[:, O_Q + lo:O_Q + lo + BLK * 4] * (HEAD_DIM ** -0.5)).astype(BF16).T
        wt_ref[T_VA:T_ROWS, :] = win_ref[:, O_VA:O_GA].T

    wupb_ref[...] = wup_ref[...].astype(BF16)
    wdnb_ref[...] = wdn_ref[...].astype(BF16)

    x = x_ref[...]
    h = _rms_modulate(x, rows_ref[ROW_GMIX:ROW_GMIX + 1, :D_MODEL], mod_ref[1:2, :], mod_ref[0:1, :])
    hb_ref[...] = h.astype(BF16)

    def proj_t(lo, hi):
        z = lax.dot_general(wt_ref[lo:hi, :], hb_ref[...], NT_DIMS, preferred_element_type=F32)
        return z + _lane_tile(bt_ref[lo:hi, :], n_blk)

    def proj_n(lo, hi):
        z = jnp.dot(hb_ref[...], win_ref[:, lo:hi], preferred_element_type=F32)
        return z + rows_ref[ROW_BIN:ROW_BIN + 1, lo:hi]

    half = D_MODEL // 2
    zv0 = proj_n(O_V, O_V + half)
    zv1 = proj_n(O_V + half, O_Q)
    zu0 = proj_n(O_U, O_U + half)
    zu1 = proj_n(O_U + half, O_V)
    gv0 = _gelu_tanh(zv0)
    zq0 = proj_t(T_Q, T_Q + half)
    gv1 = _gelu_tanh(zv1)
    zq1 = proj_t(T_Q + half, T_VA)
    gv = jnp.concatenate([gv0, gv1], axis=1)
    mu = jnp.mean(gv, axis=-1, keepdims=True)
    gc = gv - mu
    var = jnp.mean(gc * gc, axis=-1, keepdims=True)
    vn = (gc * lax.rsqrt(var + EPS) * rows_ref[ROW_LNG:ROW_LNG + 1, :D_MODEL]
          + rows_ref[ROW_LNB:ROW_LNB + 1, :D_MODEL])
    vn_ref[...] = vn.astype(BF16)
    vat = proj_t(T_VA, T_ROWS)
    kb_ref[BLK:, :] = proj_n(O_K, O_VA).astype(BF16)
    mg_ref[:, :half] = _sigmoid(proj_n(O_GA, O_GA + half))
    mg_ref[:, half:] = _sigmoid(proj_n(O_GA + half, O_GB))
    sgb_ref[...] = _sigmoid(proj_n(O_GB, IN_WIDTH)).astype(BF16)
    qt_ref[:half, :] = zq0.astype(BF16)
    qt_ref[half:, :] = zq1.astype(BF16)
    for g in range(N_KV_HEADS):
        vt_ref[g * VT_ROWS:g * VT_ROWS + HEAD_DIM, BLK:] = (
            vat[g * HEAD_DIM:(g + 1) * HEAD_DIM].astype(BF16))

    src = lax.broadcasted_iota(jnp.int32, (BLK, BLK), 0)
    dst = lax.broadcasted_iota(jnp.int32, (BLK, BLK), 1)
    causal_t = src <= dst
    zero_blk = jnp.zeros((BLK, BLK), BF16)
    gatings = []
    for pg in range(GM_GROUPS // 2):
        g0, g1 = 2 * pg, 2 * pg + 1
        r0 = slice(g0 * GM_GROUP_DIM, (g0 + 1) * GM_GROUP_DIM)
        r1 = slice(g1 * GM_GROUP_DIM, (g1 + 1) * GM_GROUP_DIM)
        w0 = jnp.where(causal_t, wst_ref[g0], 0.0).astype(BF16)
        w1 = jnp.where(causal_t, wst_ref[g1], 0.0).astype(BF16)
        bd = jnp.concatenate([jnp.concatenate([w0, zero_blk], axis=1),
                              jnp.concatenate([zero_blk, w1], axis=1)], axis=0)
        lhs = jnp.concatenate(
            [jnp.concatenate([vn_ref[j * BLK:(j + 1) * BLK, r0].T, vn_ref[j * BLK:(j + 1) * BLK, r1].T],
                             axis=1) for j in range(n_blk)], axis=0)
        gatings.append(
            jnp.dot(lhs, bd, preferred_element_type=F32) + rows_ref[ROW_BS:ROW_BS + 1, g0 * BLK:(g1 + 1) * BLK])

    zu = (zu0, zu1)

    def gated_u(half_idx):
        for pg, sg in enumerate(gatings):
            for e in range(2):
                g = 2 * pg + e
                cols = slice(g * GM_GROUP_DIM, (g + 1) * GM_GROUP_DIM)
                zpart = zu[g * GM_GROUP_DIM // half]
                zcols = slice(g * GM_GROUP_DIM % half, g * GM_GROUP_DIM % half + GM_GROUP_DIM)
                for j in range(half_idx * (n_blk // 2), (half_idx + 1) * (n_blk // 2)):
                    cs = slice(j * BLK, (j + 1) * BLK)
                    ya_ref[cs, cols] = (
                        _gelu_tanh(zpart[cs, zcols]) * sg[cs, e * BLK:(e + 1) * BLK].T).astype(BF16)

    width = HEADS_PER_ITEM * BLK
    key = lax.broadcasted_iota(jnp.int32, (BLK, width), 0)
    qry = lax.broadcasted_iota(jnp.int32, (BLK, width), 1) & (BLK - 1)
    cur = key <= qry
    no_prev = jnp.where(t == 0, -jnp.inf, 0.0).astype(F32)
    zero_half = jnp.zeros((HEAD_DIM, width), BF16)

    items = [(j, g, p) for j in range(n_blk) for g in range(N_KV_HEADS)
             for p in range(Q_REP // HEADS_PER_ITEM)]

    def scores(item):
        j, g, p = item
        h0 = g * Q_REP + p * HEADS_PER_ITEM
        cs = slice(j * BLK, (j + 1) * BLK)
        kband = kb_ref[j * BLK:(j + 2) * BLK, (g // 2) * LANES:(g // 2 + 1) * LANES]
        qg = jnp.concatenate(
            [qt_ref[(h0 + e) * HEAD_DIM:(h0 + e + 1) * HEAD_DIM, cs] for e in range(HEADS_PER_ITEM)],
            axis=1)
        rhs = jnp.concatenate([qg, zero_half] if g % 2 == 0 else [zero_half, qg], axis=0)
        return jnp.dot(kband, rhs, preferred_element_type=F32)

    def softmax(item, st):
        j, g, p = item
        h0 = g * Q_REP + p * HEADS_PER_ITEM
        s_prev = st[:BLK]
        if j == 0:
            s_prev = s_prev + no_prev
        live = jnp.where(cur, st[BLK:], s_prev)
        sink = rows_ref[ROW_SINK:ROW_SINK + 1, h0 * BLK:h0 * BLK + width]
        m = jnp.maximum(jnp.max(live, axis=0, keepdims=True), sink)
        pr = jnp.exp(live - m)
        p_sink = jnp.exp(sink - m)
        pcat = jnp.concatenate([jnp.where(cur, 0.0, pr), jnp.where(cur, pr, 0.0)], axis=0)
        return pcat.astype(BF16), p_sink

    def attend(item, pcat, p_sink):
        j, g, p = item
        h0 = g * Q_REP + p * HEADS_PER_ITEM
        ot = jnp.dot(vt_ref[g * VT_ROWS:(g + 1) * VT_ROWS, j * BLK:(j + 2) * BLK], pcat,
                     preferred_element_type=F32)
        inv = 1.0 / (ot[HEAD_DIM:HEAD_DIM + 1] + p_sink)
        o = (ot[:HEAD_DIM] * inv).astype(BF16)
        for e in range(HEADS_PER_ITEM):
            ybt_ref[(h0 + e) * HEAD_DIM:(h0 + e + 1) * HEAD_DIM, j * BLK:(j + 1) * BLK] = (
                o[:, e * BLK:(e + 1) * BLK])

    def gate_a(half_idx):
        rs = slice(half_idx * (TS_MIX // 2), (half_idx + 1) * (TS_MIX // 2))
        gated_u(half_idx)
        acc = jnp.dot(ya_ref[rs, :], wa_ref[...], preferred_element_type=F32)
        mg_ref[rs, :] = mg_ref[rs, :] * acc

    n_items = len(items)
    fill = {n_items // 4 + ATT_LAG: 0, (3 * n_items) // 4 + ATT_LAG: 1}
    st, pc = {}, {}
    for step in range(n_items + 2 * ATT_LAG):
        if step < n_items:
            st[step] = scores(items[step])
        if step in fill:
            gate_a(fill[step])
        i = step - ATT_LAG
        if 0 <= i < n_items:
            pc[i] = softmax(items[i], st.pop(i))
        i = step - 2 * ATT_LAG
        if 0 <= i < n_items:
            attend(items[i], *pc.pop(i))

    kb_ref[0:BLK, :] = kb_ref[TS_MIX:TS_MIX + BLK, :]
    vt_ref[:, 0:BLK] = vt_ref[:, TS_MIX:TS_MIX + BLK]

    acc_b = lax.dot_general(ybt_ref[...], wb_ref[...], TN_DIMS, preferred_element_type=F32)
    merged = mg_ref[...] + sgb_ref[...].astype(F32) * acc_b
    y = jnp.dot(merged.astype(BF16), wo_ref[...], preferred_element_type=F32)
    o_ref[...] = x_ref[...] + mod_ref[2:3, :] * y


def _resident(shape):
    nd = len(shape)
    return pl.BlockSpec(shape, lambda *_: (0,) * nd, pipeline_mode=pl.Buffered(1))


def _mixer_call(x, mod, rows, w_in, b_t, ws_t, wa, wb, wo, w_up, w_down):
    batch, seq, _ = x.shape
    n_tiles = seq // TS_MIX
    n_steps = batch * n_tiles
    tile = pl.BlockSpec((None, TS_MIX, D_MODEL), lambda b, t: (b, t, 0))
    up_rows = pl.BlockSpec((D_MODEL // n_steps, D_FF), lambda b, t: (b * n_tiles + t, 0))
    dn_rows = pl.BlockSpec((D_FF // n_steps, D_MODEL), lambda b, t: (b * n_tiles + t, 0))
    return pl.pallas_call(
        _mixer_kernel,
        grid=(batch, n_tiles),
        in_specs=[
            tile,
            pl.BlockSpec((None, N_MOD, D_MODEL), lambda b, t: (b, 0, 0)),
            _resident((ROWS_SUBLANES, ROWS_WIDTH)),
            _resident((D_MODEL, IN_WIDTH)),
            _resident((T_ROWS, LANES)),
            _resident((GM_GROUPS, BLK, BLK)),
            _resident((D_MODEL, D_MODEL)),
            _resident((D_MODEL, D_MODEL)),
            _resident((D_MODEL, D_MODEL)),
            up_rows,
            dn_rows,
        ],
        out_specs=[tile, up_rows, dn_rows],
        out_shape=[jax.ShapeDtypeStruct(x.shape, F32),
                   jax.ShapeDtypeStruct(w_up.shape, BF16),
                   jax.ShapeDtypeStruct(w_down.shape, BF16)],
        scratch_shapes=[
            pltpu.VMEM((T_ROWS, D_MODEL), BF16),
            pltpu.VMEM((TS_MIX, D_MODEL), BF16),
            pltpu.VMEM((TS_MIX, D_MODEL), BF16),
            pltpu.VMEM((TS_MIX, D_MODEL), BF16),
            pltpu.VMEM((D_MODEL, TS_MIX), BF16),
            pltpu.VMEM((BLK + TS_MIX, KV_WIDTH), BF16),
            pltpu.VMEM((N_KV_HEADS * VT_ROWS, BLK + TS_MIX), BF16),
            pltpu.VMEM((D_MODEL, TS_MIX), BF16),
            pltpu.VMEM((TS_MIX, D_MODEL), F32),
            pltpu.VMEM((TS_MIX, D_MODEL), BF16),
        ],
        compiler_params=pltpu.CompilerParams(
            dimension_semantics=("arbitrary", "arbitrary"),
            vmem_limit_bytes=VMEM_LIMIT_BYTES),
        name="token_mixer",
    )(x, mod, rows, w_in, b_t, ws_t, wa, wb, wo, w_up, w_down)


def _mlp_kernel(x_ref, mod_ref, gmlp_ref, wup_ref, wdn_ref, gfin_ref, o_ref, hb_ref, act_ref):
    x = x_ref[...]
    h = _rms_modulate(x, gmlp_ref[...], mod_ref[4:5, :], mod_ref[3:4, :])
    hb_ref[...] = h.astype(BF16)
    for c in range(D_FF // D_MODEL):
        cs = slice(c * D_MODEL, (c + 1) * D_MODEL)
        a = jnp.maximum(_dot(hb_ref[...], wup_ref[:, cs]), 0.0)
        act_ref[:, cs] = (a * a).astype(BF16)
    x2 = x_ref[...] + mod_ref[5:6, :] * _dot(act_ref[...], wdn_ref[...])
    ms = jnp.mean(x2 * x2, axis=-1, keepdims=True)
    o_ref[...] = x2 * lax.rsqrt(ms + EPS) * gfin_ref[...]


def _mlp_call(x, mod, g_mlp, w_up, w_down, g_final):
    batch, seq, _ = x.shape
    tile = pl.BlockSpec((None, TM_MLP, D_MODEL), lambda b, t: (b, t, 0))
    return pl.pallas_call(
        _mlp_kernel,
        grid=(batch, seq // TM_MLP),
        in_specs=[
            tile,
            pl.BlockSpec((None, N_MOD, D_MODEL), lambda b, t: (b, 0, 0)),
            _resident((1, D_MODEL)),
            _resident((D_MODEL, D_FF)),
            _resident((D_FF, D_MODEL)),
            _resident((1, D_MODEL)),
        ],
        out_specs=tile,
        out_shape=jax.ShapeDtypeStruct(x.shape, F32),
        scratch_shapes=[
            pltpu.VMEM((TM_MLP, D_MODEL), BF16),
            pltpu.VMEM((TM_MLP, D_FF), BF16),
        ],
        compiler_params=pltpu.CompilerParams(
            dimension_semantics=("arbitrary", "arbitrary"),
            vmem_limit_bytes=VMEM_LIMIT_BYTES),
        name="channel_mlp",
    )(x, mod, g_mlp, w_up, w_down, g_final)


def kernel(x, c, w_ada, b_ada, g_norm_mix, w_in, b_in, gm_ln_g, gm_ln_b, gm_ws, gm_bs, attn_sinks,
           w_branch_a, w_branch_b, w_out, g_norm_mlp, w_up, w_down, g_final):
    batch = x.shape[0]
    depth = w_in.shape[0]
    row = lambda v: v.reshape(1, -1)
    lane_col = lambda v: jnp.broadcast_to(v[:, None], (v.shape[0], LANES))
    for l in range(depth):
        mod = _ada_call(c, w_ada[l], row(b_ada[l])).reshape(batch, N_MOD, D_MODEL)
        bl = b_in[l]
        b_t = lane_col(jnp.concatenate([bl[O_Q:O_K] * (HEAD_DIM ** -0.5), bl[O_VA:O_GA]]))
        pad = lambda v: jnp.pad(v.reshape(-1), (0, ROWS_WIDTH - v.size))
        rows = jnp.zeros((ROWS_SUBLANES, ROWS_WIDTH), F32)
        rows = rows.at[ROW_GMIX].set(pad(g_norm_mix[l])).at[ROW_BIN].set(pad(bl))
        rows = rows.at[ROW_BS].set(pad(gm_bs[l])).at[ROW_SINK].set(pad(jnp.repeat(attn_sinks[l], BLK)))
        rows = rows.at[ROW_LNG].set(pad(gm_ln_g[l])).at[ROW_LNB].set(pad(gm_ln_b[l]))
        x, w_up_b, w_down_b = _mixer_call(
            x, mod, rows, w_in[l].astype(BF16), b_t,
            jnp.swapaxes(gm_ws[l], 1, 2),
            w_branch_a[l].astype(BF16), w_branch_b[l].astype(BF16), w_out[l].astype(BF16),
            w_up[l], w_down[l])
        assert depth == 1
        x = _mlp_call(x, mod, row(g_norm_mlp[l]), w_up_b, w_down_b, row(g_final))
    return x
```

```python
import jax
import jax.numpy as jnp
from jax import lax
from jax.experimental import pallas as pl
from jax.experimental.pallas import tpu as pltpu

D_MODEL = 1024
BLK = 128
GM_GROUPS = 8
GM_GROUP_DIM = D_MODEL // GM_GROUPS
N_Q_HEADS = 16
N_KV_HEADS = 4
HEAD_DIM = 64
Q_REP = N_Q_HEADS // N_KV_HEADS
KV_WIDTH = N_KV_HEADS * HEAD_DIM
D_FF = 4 * D_MODEL
N_MOD = 6
EPS = 1e-6

O_U = 0
O_V = O_U + D_MODEL
O_Q = O_V + D_MODEL
O_K = O_Q + D_MODEL
O_VA = O_K + KV_WIDTH
O_GA = O_VA + KV_WIDTH
O_GB = O_GA + D_MODEL
IN_WIDTH = O_GB + D_MODEL

T_Q = 0
T_VA = T_Q + D_MODEL
T_ROWS = T_VA + KV_WIDTH

LANES = 128
BF16_ROWS = 16
HEADS_PER_ITEM = 2
VT_ROWS = HEAD_DIM + BF16_ROWS

ROW_GMIX, ROW_BIN, ROW_SINK, ROW_LNG, ROW_LNB = 0, 1, 2, 3, 4
ROWS_SUBLANES = 8
ROWS_WIDTH = 8192
ATT_LAG = 6
TS_MIX = 512
TM_MLP = 1024
VMEM_LIMIT_BYTES = 56 * 1024 * 1024

F32 = jnp.float32
BF16 = jnp.bfloat16
NT_DIMS = (((1,), (1,)), ((), ()))
TN_DIMS = (((0,), (0,)), ((), ()))

GELU_C0 = 0.7978845608028654
GELU_C1 = GELU_C0 * 0.044715


def _sigmoid(z):
    return 0.5 * jnp.tanh(0.5 * z) + 0.5


def _gelu_tanh(z):
    hz = 0.5 * z
    return hz + hz * jnp.tanh(z * (GELU_C0 + GELU_C1 * (z * z)))


def _dot(a, b):
    return jnp.dot(a, b, preferred_element_type=F32)


def _rms_modulate(x, g, scale, shift):
    ms = jnp.mean(x * x, axis=-1, keepdims=True)
    h = x * lax.rsqrt(ms + EPS) * g
    return h * (1.0 + scale) + shift


def _lane_tile(col, n):
    return jnp.concatenate([col] * n, axis=1)


def _ada_kernel(c_ref, w_ref, b_ref, o_ref):
    c = c_ref[...]
    o_ref[...] = jnp.dot(c * _sigmoid(c), w_ref[...], preferred_element_type=F32) + b_ref[...]


def _ada_call(c, w_ada, b_ada):
    batch = c.shape[0]
    return pl.pallas_call(
        _ada_kernel,
        grid=(N_MOD,),
        in_specs=[
            pl.BlockSpec((batch, D_MODEL), lambda n: (0, 0)),
            pl.BlockSpec((D_MODEL, D_MODEL), lambda n: (0, n)),
            pl.BlockSpec((1, D_MODEL), lambda n: (0, n)),
        ],
        out_specs=pl.BlockSpec((batch, D_MODEL), lambda n: (0, n)),
        out_shape=jax.ShapeDtypeStruct((batch, N_MOD * D_MODEL), F32),
        name="ada_mod",
    )(c, w_ada, b_ada)


def _mixer_kernel(x_ref, mod_ref, rows_ref, win_ref, bt_ref,
                  wst_ref, bst_ref, wa_ref, wb_ref, wo_ref, wup_ref, wdn_ref,
                  o_ref, wupb_ref, wdnb_ref,
                  wt_ref, hb_ref, vn_ref, ya_ref, qt_ref, kb_ref, vt_ref, ybt_ref, mg_ref, sgb_ref):
    t = pl.program_id(1)
    n_blk = TS_MIX // BLK

    @pl.when(t == 0)
    def _():
        kb_ref[0:BLK, :] = jnp.zeros((BLK, KV_WIDTH), BF16)
        r = lax.broadcasted_iota(jnp.int32, vt_ref.shape, 0)
        ones_row = r == HEAD_DIM
        for g in range(1, N_KV_HEADS):
            ones_row = ones_row | (r == g * VT_ROWS + HEAD_DIM)
        vt_ref[...] = jnp.where(ones_row, 1.0, 0.0).astype(BF16)

    @pl.when((pl.program_id(0) == 0) & (t == 0))
    def _():
        for lo in range(0, D_MODEL, BLK * 4):
            wt_ref[T_Q + lo:T_Q + lo + BLK * 4, :] = (
                win_ref[:, O_Q + lo:O_Q + lo + BLK * 4] * (HEAD_DIM ** -0.5)).astype(BF16).T
        wt_ref[T_VA:T_ROWS, :] = win_ref[:, O_VA:O_GA].T

    wupb_ref[...] = wup_ref[...].astype(BF16)
    wdnb_ref[...] = wdn_ref[...].astype(BF16)

    x = x_ref[...]
    h = _rms_modulate(x, rows_ref[ROW_GMIX:ROW_GMIX + 1, :D_MODEL], mod_ref[1:2, :], mod_ref[0:1, :])
    hb_ref[...] = h.astype(BF16)

    def proj_t(lo, hi):
        z = lax.dot_general(wt_ref[lo:hi, :], hb_ref[...], NT_DIMS, preferred_element_type=F32)
        return z + _lane_tile(bt_ref[lo:hi, :], n_blk)

    def proj_n(lo, hi):
        z = jnp.dot(hb_ref[...], win_ref[:, lo:hi], preferred_element_type=F32)
        return z + rows_ref[ROW_BIN:ROW_BIN + 1, lo:hi]

    half = D_MODEL // 2
    zv0 = proj_n(O_V, O_V + half)
    zv1 = proj_n(O_V + half, O_Q)
    zu0 = proj_n(O_U, O_U + half)
    zu1 = proj_n(O_U + half, O_V)
    gv0 = _gelu_tanh(zv0)
    zq0 = proj_t(T_Q, T_Q + half)
    gv1 = _gelu_tanh(zv1)
    zq1 = proj_t(T_Q + half, T_VA)
    gv = jnp.concatenate([gv0, gv1], axis=1)
    mu = jnp.mean(gv, axis=-1, keepdims=True)
    gc = gv - mu
    var = jnp.mean(gc * gc, axis=-1, keepdims=True)
    vn = (gc * lax.rsqrt(var + EPS) * rows_ref[ROW_LNG:ROW_LNG + 1, :D_MODEL]
          + rows_ref[ROW_LNB:ROW_LNB + 1, :D_MODEL])
    vn_ref[...] = vn.astype(BF16)
    vat = proj_t(T_VA, T_ROWS)
    kb_ref[BLK:, :] = proj_n(O_K, O_VA).astype(BF16)
    mg_ref[:, :half] = _sigmoid(proj_n(O_GA, O_GA + half))
    mg_ref[:, half:] = _sigmoid(proj_n(O_GA + half, O_GB))
    sgb_ref[...] = _sigmoid(proj_n(O_GB, IN_WIDTH)).astype(BF16)
    qt_ref[:half, :] = zq0.astype(BF16)
    qt_ref[half:, :] = zq1.astype(BF16)
    for g in range(N_KV_HEADS):
        vt_ref[g * VT_ROWS:g * VT_ROWS + HEAD_DIM, BLK:] = (
            vat[g * HEAD_DIM:(g + 1) * HEAD_DIM].astype(BF16))

    src = lax.broadcasted_iota(jnp.int32, (BLK, BLK), 0)
    dst = lax.broadcasted_iota(jnp.int32, (BLK, BLK), 1)
    causal_t = src <= dst
    zero_blk = jnp.zeros((BLK, BLK), BF16)
    gatings = []
    for pg in range(GM_GROUPS // 2):
        g0, g1 = 2 * pg, 2 * pg + 1
        r0 = slice(g0 * GM_GROUP_DIM, (g0 + 1) * GM_GROUP_DIM)
        r1 = slice(g1 * GM_GROUP_DIM, (g1 + 1) * GM_GROUP_DIM)
        w0 = jnp.where(causal_t, wst_ref[g0], 0.0).astype(BF16)
        w1 = jnp.where(causal_t, wst_ref[g1], 0.0).astype(BF16)
        bd = jnp.concatenate([jnp.concatenate([w0, zero_blk], axis=1),
                              jnp.concatenate([zero_blk, w1], axis=1)], axis=0)
        lhs = jnp.concatenate(
            [jnp.concatenate([vn_ref[j * BLK:(j + 1) * BLK, r0].T, vn_ref[j * BLK:(j + 1) * BLK, r1].T],
                             axis=1) for j in range(n_blk)], axis=0)
        gatings.append(jnp.dot(lhs, bd, preferred_element_type=F32))

    zu = (zu0, zu1)

    def gated_u(half_idx):
        for pg, sg in enumerate(gatings):
            for e in range(2):
                g = 2 * pg + e
                cols = slice(g * GM_GROUP_DIM, (g + 1) * GM_GROUP_DIM)
                bias = bst_ref[:, g:g + 1]
                zpart = zu[g * GM_GROUP_DIM // half]
                zcols = slice(g * GM_GROUP_DIM % half, g * GM_GROUP_DIM % half + GM_GROUP_DIM)
                for j in range(half_idx * (n_blk // 2), (half_idx + 1) * (n_blk // 2)):
                    cs = slice(j * BLK, (j + 1) * BLK)
                    ya_ref[cs, cols] = (
                        _gelu_tanh(zpart[cs, zcols]) * (sg[cs, e * BLK:(e + 1) * BLK].T + bias)).astype(BF16)

    width = HEADS_PER_ITEM * BLK
    key = lax.broadcasted_iota(jnp.int32, (BLK, width), 0)
    qry = lax.broadcasted_iota(jnp.int32, (BLK, width), 1) & (BLK - 1)
    cur = key <= qry
    no_prev = jnp.where(t == 0, -jnp.inf, 0.0).astype(F32)
    zero_half = jnp.zeros((HEAD_DIM, width), BF16)

    items = [(j, g, p) for j in range(n_blk) for g in range(N_KV_HEADS)
             for p in range(Q_REP // HEADS_PER_ITEM)]

    def scores(item):
        j, g, p = item
        h0 = g * Q_REP + p * HEADS_PER_ITEM
        cs = slice(j * BLK, (j + 1) * BLK)
        kband = kb_ref[j * BLK:(j + 2) * BLK, (g // 2) * LANES:(g // 2 + 1) * LANES]
        qg = jnp.concatenate(
            [qt_ref[(h0 + e) * HEAD_DIM:(h0 + e + 1) * HEAD_DIM, cs] for e in range(HEADS_PER_ITEM)],
            axis=1)
        rhs = jnp.concatenate([qg, zero_half] if g % 2 == 0 else [zero_half, qg], axis=0)
        return jnp.dot(kband, rhs, preferred_element_type=F32)

    def softmax(item, st):
        j, g, p = item
        h0 = g * Q_REP + p * HEADS_PER_ITEM
        s_prev = st[:BLK]
        if j == 0:
            s_prev = s_prev + no_prev
        live = jnp.where(cur, st[BLK:], s_prev)
        sink = rows_ref[ROW_SINK:ROW_SINK + 1, h0 * BLK:h0 * BLK + width]
        m = jnp.maximum(jnp.max(live, axis=0, keepdims=True), sink)
        pr = jnp.exp(live - m)
        p_sink = jnp.exp(sink - m)
        pcat = jnp.concatenate([jnp.where(cur, 0.0, pr), jnp.where(cur, pr, 0.0)], axis=0)
        return pcat.astype(BF16), p_sink

    def attend(item, pcat, p_sink):
        j, g, p = item
        h0 = g * Q_REP + p * HEADS_PER_ITEM
        ot = jnp.dot(vt_ref[g * VT_ROWS:(g + 1) * VT_ROWS, j * BLK:(j + 2) * BLK], pcat,
                     preferred_element_type=F32)
        inv = 1.0 / (ot[HEAD_DIM:HEAD_DIM + 1] + p_sink)
        o = (ot[:HEAD_DIM] * inv).astype(BF16)
        for e in range(HEADS_PER_ITEM):
            ybt_ref[(h0 + e) * HEAD_DIM:(h0 + e + 1) * HEAD_DIM, j * BLK:(j + 1) * BLK] = (
                o[:, e * BLK:(e + 1) * BLK])

    def gate_a(half_idx):
        rs = slice(half_idx * (TS_MIX // 2), (half_idx + 1) * (TS_MIX // 2))
        gated_u(half_idx)
        acc = jnp.dot(ya_ref[rs, :], wa_ref[...], preferred_element_type=F32)
        mg_ref[rs, :] = mg_ref[rs, :] * acc

    n_items = len(items)
    fill = {n_items // 4 + ATT_LAG: 0, (3 * n_items) // 4 + ATT_LAG: 1}
    st, pc = {}, {}
    for step in range(n_items + 2 * ATT_LAG):
        if step < n_items:
            st[step] = scores(items[step])
        if step in fill:
            gate_a(fill[step])
        i = step - ATT_LAG
        if 0 <= i < n_items:
            pc[i] = softmax(items[i], st.pop(i))
        i = step - 2 * ATT_LAG
        if 0 <= i < n_items:
            attend(items[i], *pc.pop(i))

    kb_ref[0:BLK, :] = kb_ref[TS_MIX:TS_MIX + BLK, :]
    vt_ref[:, 0:BLK] = vt_ref[:, TS_MIX:TS_MIX + BLK]

    acc_b = lax.dot_general(ybt_ref[...], wb_ref[...], TN_DIMS, preferred_element_type=F32)
    merged = mg_ref[...] + sgb_ref[...].astype(F32) * acc_b
    y = jnp.dot(merged.astype(BF16), wo_ref[...], preferred_element_type=F32)
    o_ref[...] = x_ref[...] + mod_ref[2:3, :] * y


def _resident(shape):
    nd = len(shape)
    return pl.BlockSpec(shape, lambda *_: (0,) * nd, pipeline_mode=pl.Buffered(1))


def _mixer_call(x, mod, rows, w_in, b_t, ws_t, bs_t, wa, wb, wo, w_up, w_down):
    batch, seq, _ = x.shape
    n_tiles = seq // TS_MIX
    n_steps = batch * n_tiles
    tile = pl.BlockSpec((None, TS_MIX, D_MODEL), lambda b, t: (b, t, 0))
    up_rows = pl.BlockSpec((D_MODEL // n_steps, D_FF), lambda b, t: (b * n_tiles + t, 0))
    dn_rows = pl.BlockSpec((D_FF // n_steps, D_MODEL), lambda b, t: (b * n_tiles + t, 0))
    return pl.pallas_call(
        _mixer_kernel,
        grid=(batch, n_tiles),
        in_specs=[
            tile,
            pl.BlockSpec((None, N_MOD, D_MODEL), lambda b, t: (b, 0, 0)),
            _resident((ROWS_SUBLANES, ROWS_WIDTH)),
            _resident((D_MODEL, IN_WIDTH)),
            _resident((T_ROWS, LANES)),
            _resident((GM_GROUPS, BLK, BLK)),
            _resident((BLK, GM_GROUPS)),
            _resident((D_MODEL, D_MODEL)),
            _resident((D_MODEL, D_MODEL)),
            _resident((D_MODEL, D_MODEL)),
            up_rows,
            dn_rows,
        ],
        out_specs=[tile, up_rows, dn_rows],
        out_shape=[jax.ShapeDtypeStruct(x.shape, F32),
                   jax.ShapeDtypeStruct(w_up.shape, BF16),
                   jax.ShapeDtypeStruct(w_down.shape, BF16)],
        scratch_shapes=[
            pltpu.VMEM((T_ROWS, D_MODEL), BF16),
            pltpu.VMEM((TS_MIX, D_MODEL), BF16),
            pltpu.VMEM((TS_MIX, D_MODEL), BF16),
            pltpu.VMEM((TS_MIX, D_MODEL), BF16),
            pltpu.VMEM((D_MODEL, TS_MIX), BF16),
            pltpu.VMEM((BLK + TS_MIX, KV_WIDTH), BF16),
            pltpu.VMEM((N_KV_HEADS * VT_ROWS, BLK + TS_MIX), BF16),
            pltpu.VMEM((D_MODEL, TS_MIX), BF16),
            pltpu.VMEM((TS_MIX, D_MODEL), F32),
            pltpu.VMEM((TS_MIX, D_MODEL), BF16),
        ],
        compiler_params=pltpu.CompilerParams(
            dimension_semantics=("arbitrary", "arbitrary"),
            vmem_limit_bytes=VMEM_LIMIT_BYTES),
        name="token_mixer",
    )(x, mod, rows, w_in, b_t, ws_t, bs_t, wa, wb, wo, w_up, w_down)


def _mlp_kernel(x_ref, mod_ref, gmlp_ref, wup_ref, wdn_ref, gfin_ref, o_ref, hb_ref, act_ref):
    x = x_ref[...]
    h = _rms_modulate(x, gmlp_ref[...], mod_ref[4:5, :], mod_ref[3:4, :])
    hb_ref[...] = h.astype(BF16)
    for c in range(D_FF // D_MODEL):
        cs = slice(c * D_MODEL, (c + 1) * D_MODEL)
        a = jnp.maximum(_dot(hb_ref[...], wup_ref[:, cs]), 0.0)
        act_ref[:, cs] = (a * a).astype(BF16)
    x2 = x_ref[...] + mod_ref[5:6, :] * _dot(act_ref[...], wdn_ref[...])
    ms = jnp.mean(x2 * x2, axis=-1, keepdims=True)
    o_ref[...] = x2 * lax.rsqrt(ms + EPS) * gfin_ref[...]


def _mlp_call(x, mod, g_mlp, w_up, w_down, g_final):
    batch, seq, _ = x.shape
    tile = pl.BlockSpec((None, TM_MLP, D_MODEL), lambda b, t: (b, t, 0))
    return pl.pallas_call(
        _mlp_kernel,
        grid=(batch, seq // TM_MLP),
        in_specs=[
            tile,
            pl.BlockSpec((None, N_MOD, D_MODEL), lambda b, t: (b, 0, 0)),
            _resident((1, D_MODEL)),
            _resident((D_MODEL, D_FF)),
            _resident((D_FF, D_MODEL)),
            _resident((1, D_MODEL)),
        ],
        out_specs=tile,
        out_shape=jax.ShapeDtypeStruct(x.shape, F32),
        scratch_shapes=[
            pltpu.VMEM((TM_MLP, D_MODEL), BF16),
            pltpu.VMEM((TM_MLP, D_FF), BF16),
        ],
        compiler_params=pltpu.CompilerParams(
            dimension_semantics=("arbitrary", "arbitrary"),
            vmem_limit_bytes=VMEM_LIMIT_BYTES),
        name="channel_mlp",
    )(x, mod, g_mlp, w_up, w_down, g_final)


def kernel(x, c, w_ada, b_ada, g_norm_mix, w_in, b_in, gm_ln_g, gm_ln_b, gm_ws, gm_bs, attn_sinks,
           w_branch_a, w_branch_b, w_out, g_norm_mlp, w_up, w_down, g_final):
    batch = x.shape[0]
    depth = w_in.shape[0]
    row = lambda v: v.reshape(1, -1)
    lane_col = lambda v: jnp.broadcast_to(v[:, None], (v.shape[0], LANES))
    for l in range(depth):
        mod = _ada_call(c, w_ada[l], row(b_ada[l])).reshape(batch, N_MOD, D_MODEL)
        bl = b_in[l]
        b_t = lane_col(jnp.concatenate([bl[O_Q:O_K] * (HEAD_DIM ** -0.5), bl[O_VA:O_GA]]))
        pad = lambda v: jnp.pad(v.reshape(-1), (0, ROWS_WIDTH - v.size))
        rows = jnp.zeros((ROWS_SUBLANES, ROWS_WIDTH), F32)
        rows = rows.at[ROW_GMIX].set(pad(g_norm_mix[l])).at[ROW_BIN].set(pad(bl))
        rows = rows.at[ROW_SINK].set(pad(jnp.repeat(attn_sinks[l], BLK)))
        rows = rows.at[ROW_LNG].set(pad(gm_ln_g[l])).at[ROW_LNB].set(pad(gm_ln_b[l]))
        x, w_up_b, w_down_b = _mixer_call(
            x, mod, rows, w_in[l].astype(BF16), b_t,
            jnp.swapaxes(gm_ws[l], 1, 2), gm_bs[l].T,
            w_branch_a[l].astype(BF16), w_branch_b[l].astype(BF16), w_out[l].astype(BF16),
            w_up[l], w_down[l])
        assert depth == 1
        x = _mlp_call(x, mod, row(g_norm_mlp[l]), w_up_b, w_down_b, row(g_final))
    return x
```

```python
import jax
import jax.numpy as jnp
from jax import lax
from jax.experimental import pallas as pl
from jax.experimental.pallas import tpu as pltpu

D_MODEL = 1024
BLK = 128
GM_GROUPS = 8
GM_GROUP_DIM = D_MODEL // GM_GROUPS
N_Q_HEADS = 16
N_KV_HEADS = 4
HEAD_DIM = 64
Q_REP = N_Q_HEADS // N_KV_HEADS
KV_WIDTH = N_KV_HEADS * HEAD_DIM
D_FF = 4 * D_MODEL
N_MOD = 6
EPS = 1e-6

O_U = 0
O_V = O_U + D_MODEL
O_Q = O_V + D_MODEL
O_K = O_Q + D_MODEL
O_VA = O_K + KV_WIDTH
O_GA = O_VA + KV_WIDTH
O_GB = O_GA + D_MODEL
IN_WIDTH = O_GB + D_MODEL

T_Q = 0
T_VA = T_Q + D_MODEL
T_ROWS = T_VA + KV_WIDTH

LANES = 128
BF16_ROWS = 16
HEADS_PER_ITEM = 2
VT_ROWS = HEAD_DIM + BF16_ROWS

ROW_GMIX, ROW_BIN, ROW_BS, ROW_SINK, ROW_LNG, ROW_LNB = 0, 1, 2, 3, 4, 5
ROWS_SUBLANES = 8
ROWS_WIDTH = 8192
ATT_LAG = 4
TS_MIX = 512
TM_MLP = 1024
VMEM_LIMIT_BYTES = 56 * 1024 * 1024

F32 = jnp.float32
BF16 = jnp.bfloat16
NT_DIMS = (((1,), (1,)), ((), ()))
TN_DIMS = (((0,), (0,)), ((), ()))

GELU_C0 = 0.7978845608028654
GELU_C1 = GELU_C0 * 0.044715


def _sigmoid(z):
    return 0.5 * jnp.tanh(0.5 * z) + 0.5


def _gelu_tanh(z):
    hz = 0.5 * z
    return hz + hz * jnp.tanh(z * (GELU_C0 + GELU_C1 * (z * z)))


def _dot(a, b):
    return jnp.dot(a, b, preferred_element_type=F32)


def _rms_modulate(x, g, scale, shift):
    ms = jnp.mean(x * x, axis=-1, keepdims=True)
    h = x * lax.rsqrt(ms + EPS) * g
    return h * (1.0 + scale) + shift


def _lane_tile(col, n):
    return jnp.concatenate([col] * n, axis=1)


def _ada_kernel(c_ref, w_ref, b_ref, o_ref):
    c = c_ref[...]
    o_ref[...] = jnp.dot(c * _sigmoid(c), w_ref[...], preferred_element_type=F32) + b_ref[...]


def _ada_call(c, w_ada, b_ada):
    batch = c.shape[0]
    return pl.pallas_call(
        _ada_kernel,
        grid=(N_MOD,),
        in_specs=[
            pl.BlockSpec((batch, D_MODEL), lambda n: (0, 0)),
            pl.BlockSpec((D_MODEL, D_MODEL), lambda n: (0, n)),
            pl.BlockSpec((1, D_MODEL), lambda n: (0, n)),
        ],
        out_specs=pl.BlockSpec((batch, D_MODEL), lambda n: (0, n)),
        out_shape=jax.ShapeDtypeStruct((batch, N_MOD * D_MODEL), F32),
        name="ada_mod",
    )(c, w_ada, b_ada)


def _mixer_kernel(x_ref, mod_ref, rows_ref, win_ref, bt_ref,
                  wst_ref, wa_ref, wb_ref, wo_ref, wup_ref, wdn_ref,
                  o_ref, wupb_ref, wdnb_ref,
                  wt_ref, hb_ref, vn_ref, ya_ref, qt_ref, kb_ref, vt_ref, ybt_ref, mg_ref, sgb_ref):
    t = pl.program_id(1)
    n_blk = TS_MIX // BLK

    @pl.when(t == 0)
    def _():
        kb_ref[0:BLK, :] = jnp.zeros((BLK, KV_WIDTH), BF16)
        r = lax.broadcasted_iota(jnp.int32, vt_ref.shape, 0)
        ones_row = r == HEAD_DIM
        for g in range(1, N_KV_HEADS):
            ones_row = ones_row | (r == g * VT_ROWS + HEAD_DIM)
        vt_ref[...] = jnp.where(ones_row, 1.0, 0.0).astype(BF16)

    @pl.when((pl.program_id(0) == 0) & (t == 0))
    def _():
        for lo in range(0, D_MODEL, BLK * 4):
            wt_ref[T_Q + lo:T_Q + lo + BLK * 4, :] = (
                win_ref[:, O_Q + lo:O_Q + lo + BLK * 4] * (HEAD_DIM ** -0.5)).astype(BF16).T
        wt_ref[T_VA:T_ROWS, :] = win_ref[:, O_VA:O_GA].T

    wupb_ref[...] = wup_ref[...].astype(BF16)
    wdnb_ref[...] = wdn_ref[...].astype(BF16)

    x = x_ref[...]
    h = _rms_modulate(x, rows_ref[ROW_GMIX:ROW_GMIX + 1, :D_MODEL], mod_ref[1:2, :], mod_ref[0:1, :])
    hb_ref[...] = h.astype(BF16)

    def proj_t(lo, hi):
        z = lax.dot_general(wt_ref[lo:hi, :], hb_ref[...], NT_DIMS, preferred_element_type=F32)
        return z + _lane_tile(bt_ref[lo:hi, :], n_blk)

    def proj_n(lo, hi):
        z = jnp.dot(hb_ref[...], win_ref[:, lo:hi], preferred_element_type=F32)
        return z + rows_ref[ROW_BIN:ROW_BIN + 1, lo:hi]

    half = D_MODEL // 2
    zv0 = proj_n(O_V, O_V + half)
    zv1 = proj_n(O_V + half, O_Q)
    zu0 = proj_n(O_U, O_U + half)
    zu1 = proj_n(O_U + half, O_V)
    gv0 = _gelu_tanh(zv0)
    zq0 = proj_t(T_Q, T_Q + half)
    gv1 = _gelu_tanh(zv1)
    zq1 = proj_t(T_Q + half, T_VA)
    gv = jnp.concatenate([gv0, gv1], axis=1)
    mu = jnp.mean(gv, axis=-1, keepdims=True)
    gc = gv - mu
    var = jnp.mean(gc * gc, axis=-1, keepdims=True)
    vn = (gc * lax.rsqrt(var + EPS) * rows_ref[ROW_LNG:ROW_LNG + 1, :D_MODEL]
          + rows_ref[ROW_LNB:ROW_LNB + 1, :D_MODEL])
    vn_ref[...] = vn.astype(BF16)
    vat = proj_t(T_VA, T_ROWS)
    kb_ref[BLK:, :] = proj_n(O_K, O_VA).astype(BF16)
    mg_ref[:, :half] = _sigmoid(proj_n(O_GA, O_GA + half))
    mg_ref[:, half:] = _sigmoid(proj_n(O_GA + half, O_GB))
    sgb_ref[...] = _sigmoid(proj_n(O_GB, IN_WIDTH)).astype(BF16)
    qt_ref[:half, :] = zq0.astype(BF16)
    qt_ref[half:, :] = zq1.astype(BF16)
    for g in range(N_KV_HEADS):
        vt_ref[g * VT_ROWS:g * VT_ROWS + HEAD_DIM, BLK:] = (
            vat[g * HEAD_DIM:(g + 1) * HEAD_DIM].astype(BF16))

    src = lax.broadcasted_iota(jnp.int32, (BLK, BLK), 0)
    dst = lax.broadcasted_iota(jnp.int32, (BLK, BLK), 1)
    causal_t = src <= dst
    zero_blk = jnp.zeros((BLK, BLK), BF16)
    gatings = []
    for pg in range(GM_GROUPS // 2):
        g0, g1 = 2 * pg, 2 * pg + 1
        r0 = slice(g0 * GM_GROUP_DIM, (g0 + 1) * GM_GROUP_DIM)
        r1 = slice(g1 * GM_GROUP_DIM, (g1 + 1) * GM_GROUP_DIM)
        w0 = jnp.where(causal_t, wst_ref[g0], 0.0).astype(BF16)
        w1 = jnp.where(causal_t, wst_ref[g1], 0.0).astype(BF16)
        bd = jnp.concatenate([jnp.concatenate([w0, zero_blk], axis=1),
                              jnp.concatenate([zero_blk, w1], axis=1)], axis=0)
        lhs = jnp.concatenate(
            [jnp.concatenate([vn_ref[j * BLK:(j + 1) * BLK, r0].T, vn_ref[j * BLK:(j + 1) * BLK, r1].T],
                             axis=1) for j in range(n_blk)], axis=0)
        gatings.append(
            jnp.dot(lhs, bd, preferred_element_type=F32) + rows_ref[ROW_BS:ROW_BS + 1, g0 * BLK:(g1 + 1) * BLK])

    zu = (zu0, zu1)

    def gated_u(half_idx):
        for pg, sg in enumerate(gatings):
            for e in range(2):
                g = 2 * pg + e
                cols = slice(g * GM_GROUP_DIM, (g + 1) * GM_GROUP_DIM)
                zpart = zu[g * GM_GROUP_DIM // half]
                zcols = slice(g * GM_GROUP_DIM % half, g * GM_GROUP_DIM % half + GM_GROUP_DIM)
                for j in range(half_idx * (n_blk // 2), (half_idx + 1) * (n_blk // 2)):
                    cs = slice(j * BLK, (j + 1) * BLK)
                    ya_ref[cs, cols] = (
                        _gelu_tanh(zpart[cs, zcols]) * sg[cs, e * BLK:(e + 1) * BLK].T).astype(BF16)

    width = HEADS_PER_ITEM * BLK
    key = lax.broadcasted_iota(jnp.int32, (BLK, width), 0)
    qry = lax.broadcasted_iota(jnp.int32, (BLK, width), 1) & (BLK - 1)
    cur = key <= qry
    no_prev = jnp.where(t == 0, -jnp.inf, 0.0).astype(F32)
    zero_half = jnp.zeros((HEAD_DIM, width), BF16)

    items = [(j, g, p) for j in range(n_blk) for g in range(N_KV_HEADS)
             for p in range(Q_REP // HEADS_PER_ITEM)]

    def scores(item):
        j, g, p = item
        h0 = g * Q_REP + p * HEADS_PER_ITEM
        cs = slice(j * BLK, (j + 1) * BLK)
        kband = kb_ref[j * BLK:(j + 2) * BLK, (g // 2) * LANES:(g // 2 + 1) * LANES]
        qg = jnp.concatenate(
            [qt_ref[(h0 + e) * HEAD_DIM:(h0 + e + 1) * HEAD_DIM, cs] for e in range(HEADS_PER_ITEM)],
            axis=1)
        rhs = jnp.concatenate([qg, zero_half] if g % 2 == 0 else [zero_half, qg], axis=0)
        return jnp.dot(kband, rhs, preferred_element_type=F32)

    def softmax(item, st):
        j, g, p = item
        h0 = g * Q_REP + p * HEADS_PER_ITEM
        s_prev = st[:BLK]
        if j == 0:
            s_prev = s_prev + no_prev
        live = jnp.where(cur, st[BLK:], s_prev)
        sink = rows_ref[ROW_SINK:ROW_SINK + 1, h0 * BLK:h0 * BLK + width]
        m = jnp.maximum(jnp.max(live, axis=0, keepdims=True), sink)
        pr = jnp.exp(live - m)
        p_sink = jnp.exp(sink - m)
        pcat = jnp.concatenate([jnp.where(cur, 0.0, pr), jnp.where(cur, pr, 0.0)], axis=0)
        return pcat.astype(BF16), p_sink

    def attend(item, pcat, p_sink):
        j, g, p = item
        h0 = g * Q_REP + p * HEADS_PER_ITEM
        ot = jnp.dot(vt_ref[g * VT_ROWS:(g + 1) * VT_ROWS, j * BLK:(j + 2) * BLK], pcat,
                     preferred_element_type=F32)
        inv = 1.0 / (ot[HEAD_DIM:HEAD_DIM + 1] + p_sink)
        o = (ot[:HEAD_DIM] * inv).astype(BF16)
        for e in range(HEADS_PER_ITEM):
            ybt_ref[(h0 + e) * HEAD_DIM:(h0 + e + 1) * HEAD_DIM, j * BLK:(j + 1) * BLK] = (
                o[:, e * BLK:(e + 1) * BLK])

    def gate_a(half_idx):
        rs = slice(half_idx * (TS_MIX // 2), (half_idx + 1) * (TS_MIX // 2))
        gated_u(half_idx)
        acc = jnp.dot(ya_ref[rs, :], wa_ref[...], preferred_element_type=F32)
        mg_ref[rs, :] = mg_ref[rs, :] * acc

    n_items = len(items)
    fill = {n_items // 4 + ATT_LAG: 0, (3 * n_items) // 4 + ATT_LAG: 1}
    st, pc = {}, {}
    for step in range(n_items + 2 * ATT_LAG):
        if step < n_items:
            st[step] = scores(items[step])
        if step in fill:
            gate_a(fill[step])
        i = step - ATT_LAG
        if 0 <= i < n_items:
            pc[i] = softmax(items[i], st.pop(i))
        i = step - 2 * ATT_LAG
        if 0 <= i < n_items:
            attend(items[i], *pc.pop(i))

    kb_ref[0:BLK, :] = kb_ref[TS_MIX:TS_MIX + BLK, :]
    vt_ref[:, 0:BLK] = vt_ref[:, TS_MIX:TS_MIX + BLK]

    acc_b = lax.dot_general(ybt_ref[...], wb_ref[...], TN_DIMS, preferred_element_type=F32)
    merged = mg_ref[...] + sgb_ref[...].astype(F32) * acc_b
    y = jnp.dot(merged.astype(BF16), wo_ref[...], preferred_element_type=F32)
    o_ref[...] = x_ref[...] + mod_ref[2:3, :] * y


def _resident(shape):
    nd = len(shape)
    return pl.BlockSpec(shape, lambda *_: (0,) * nd, pipeline_mode=pl.Buffered(1))


def _mixer_call(x, mod, rows, w_in, b_t, ws_t, wa, wb, wo, w_up, w_down):
    batch, seq, _ = x.shape
    n_tiles = seq // TS_MIX
    n_steps = batch * n_tiles
    tile = pl.BlockSpec((None, TS_MIX, D_MODEL), lambda b, t: (b, t, 0))
    up_rows = pl.BlockSpec((D_MODEL // n_steps, D_FF), lambda b, t: (b * n_tiles + t, 0))
    dn_rows = pl.BlockSpec((D_FF // n_steps, D_MODEL), lambda b, t: (b * n_tiles + t, 0))
    return pl.pallas_call(
        _mixer_kernel,
        grid=(batch, n_tiles),
        in_specs=[
            tile,
            pl.BlockSpec((None, N_MOD, D_MODEL), lambda b, t: (b, 0, 0)),
            _resident((ROWS_SUBLANES, ROWS_WIDTH)),
            _resident((D_MODEL, IN_WIDTH)),
            _resident((T_ROWS, LANES)),
            _resident((GM_GROUPS, BLK, BLK)),
            _resident((D_MODEL, D_MODEL)),
            _resident((D_MODEL, D_MODEL)),
            _resident((D_MODEL, D_MODEL)),
            up_rows,
            dn_rows,
        ],
        out_specs=[tile, up_rows, dn_rows],
        out_shape=[jax.ShapeDtypeStruct(x.shape, F32),
                   jax.ShapeDtypeStruct(w_up.shape, BF16),
                   jax.ShapeDtypeStruct(w_down.shape, BF16)],
        scratch_shapes=[
            pltpu.VMEM((T_ROWS, D_MODEL), BF16),
            pltpu.VMEM((TS_MIX, D_MODEL), BF16),
            pltpu.VMEM((TS_MIX, D_MODEL), BF16),
            pltpu.VMEM((TS_MIX, D_MODEL), BF16),
            pltpu.VMEM((D_MODEL, TS_MIX), BF16),
            pltpu.VMEM((BLK + TS_MIX, KV_WIDTH), BF16),
            pltpu.VMEM((N_KV_HEADS * VT_ROWS, BLK + TS_MIX), BF16),
            pltpu.VMEM((D_MODEL, TS_MIX), BF16),
            pltpu.VMEM((TS_MIX, D_MODEL), F32),
            pltpu.VMEM((TS_MIX, D_MODEL), BF16),
        ],
        compiler_params=pltpu.CompilerParams(
            dimension_semantics=("arbitrary", "arbitrary"),
            vmem_limit_bytes=VMEM_LIMIT_BYTES),
        name="token_mixer",
    )(x, mod, rows, w_in, b_t, ws_t, wa, wb, wo, w_up, w_down)


def _mlp_kernel(x_ref, mod_ref, gmlp_ref, wup_ref, wdn_ref, gfin_ref, o_ref, hb_ref, act_ref):
    x = x_ref[...]
    h = _rms_modulate(x, gmlp_ref[...], mod_ref[4:5, :], mod_ref[3:4, :])
    hb_ref[...] = h.astype(BF16)
    for c in range(D_FF // D_MODEL):
        cs = slice(c * D_MODEL, (c + 1) * D_MODEL)
        a = jnp.maximum(_dot(hb_ref[...], wup_ref[:, cs]), 0.0)
        act_ref[:, cs] = (a * a).astype(BF16)
    x2 = x_ref[...] + mod_ref[5:6, :] * _dot(act_ref[...], wdn_ref[...])
    ms = jnp.mean(x2 * x2, axis=-1, keepdims=True)
    o_ref[...] = x2 * lax.rsqrt(ms + EPS) * gfin_ref[...]


def _mlp_call(x, mod, g_mlp, w_up, w_down, g_final):
    batch, seq, _ = x.shape
    tile = pl.BlockSpec((None, TM_MLP, D_MODEL), lambda b, t: (b, t, 0))
    return pl.pallas_call(
        _mlp_kernel,
        grid=(batch, seq // TM_MLP),
        in_specs=[
            tile,
            pl.BlockSpec((None, N_MOD, D_MODEL), lambda b, t: (b, 0, 0)),
            _resident((1, D_MODEL)),
            _resident((D_MODEL, D_FF)),
            _resident((D_FF, D_MODEL)),
            _resident((1, D_MODEL)),
        ],
        out_specs=tile,
        out_shape=jax.ShapeDtypeStruct(x.shape, F32),
        scratch_shapes=[
            pltpu.VMEM((TM_MLP, D_MODEL), BF16),
            pltpu.VMEM((TM_MLP, D_FF), BF16),
        ],
        compiler_params=pltpu.CompilerParams(
            dimension_semantics=("arbitrary", "arbitrary"),
            vmem_limit_bytes=VMEM_LIMIT_BYTES),
        name="channel_mlp",
    )(x, mod, g_mlp, w_up, w_down, g_final)


def kernel(x, c, w_ada, b_ada, g_norm_mix, w_in, b_in, gm_ln_g, gm_ln_b, gm_ws, gm_bs, attn_sinks,
           w_branch_a, w_branch_b, w_out, g_norm_mlp, w_up, w_down, g_final):
    batch = x.shape[0]
    depth = w_in.shape[0]
    row = lambda v: v.reshape(1, -1)
    lane_col = lambda v: jnp.broadcast_to(v[:, None], (v.shape[0], LANES))
    for l in range(depth):
        mod = _ada_call(c, w_ada[l], row(b_ada[l])).reshape(batch, N_MOD, D_MODEL)
        bl = b_in[l]
        b_t = lane_col(jnp.concatenate([bl[O_Q:O_K] * (HEAD_DIM ** -0.5), bl[O_VA:O_GA]]))
        pad = lambda v: jnp.pad(v.reshape(-1), (0, ROWS_WIDTH - v.size))
        rows = jnp.zeros((ROWS_SUBLANES, ROWS_WIDTH), F32)
        rows = rows.at[ROW_GMIX].set(pad(g_norm_mix[l])).at[ROW_BIN].set(pad(bl))
        rows = rows.at[ROW_BS].set(pad(gm_bs[l])).at[ROW_SINK].set(pad(jnp.repeat(attn_sinks[l], BLK)))
        rows = rows.at[ROW_LNG].set(pad(gm_ln_g[l])).at[ROW_LNB].set(pad(gm_ln_b[l]))
        x, w_up_b, w_down_b = _mixer_call(
            x, mod, rows, w_in[l].astype(BF16), b_t,
            jnp.swapaxes(gm_ws[l], 1, 2),
            w_branch_a[l].astype(BF16), w_branch_b[l].astype(BF16), w_out[l].astype(BF16),
            w_up[l], w_down[l])
        assert depth == 1
        x = _mlp_call(x, mod, row(g_norm_mlp[l]), w_up_b, w_down_b, row(g_final))
    return x
```

```python
import jax
import jax.numpy as jnp
from jax import lax
from jax.experimental import pallas as pl
from jax.experimental.pallas import tpu as pltpu

D_MODEL = 1024
BLK = 128
GM_GROUPS = 8
GM_GROUP_DIM = D_MODEL // GM_GROUPS
N_Q_HEADS = 16
N_KV_HEADS = 4
HEAD_DIM = 64
Q_REP = N_Q_HEADS // N_KV_HEADS
KV_WIDTH = N_KV_HEADS * HEAD_DIM
D_FF = 4 * D_MODEL
N_MOD = 6
EPS = 1e-6

O_U = 0
O_V = O_U + D_MODEL
O_Q = O_V + D_MODEL
O_K = O_Q + D_MODEL
O_VA = O_K + KV_WIDTH
O_GA = O_VA + KV_WIDTH
O_GB = O_GA + D_MODEL
IN_WIDTH = O_GB + D_MODEL

T_Q = 0
T_VA = T_Q + D_MODEL
T_ROWS = T_VA + KV_WIDTH

LANES = 128
BF16_ROWS = 16
HEADS_PER_ITEM = 2
VT_ROWS = HEAD_DIM + BF16_ROWS

ROW_GMIX, ROW_BIN, ROW_BS, ROW_SINK, ROW_LNG, ROW_LNB = 0, 1, 2, 3, 4, 5
ROWS_SUBLANES = 8
ROWS_WIDTH = 8192
ATT_LAG = 3
TS_MIX = 512
TM_MLP = 1024
VMEM_LIMIT_BYTES = 56 * 1024 * 1024

F32 = jnp.float32
BF16 = jnp.bfloat16
NT_DIMS = (((1,), (1,)), ((), ()))
TN_DIMS = (((0,), (0,)), ((), ()))

GELU_C0 = 0.7978845608028654
GELU_C1 = GELU_C0 * 0.044715


def _sigmoid(z):
    return 0.5 * jnp.tanh(0.5 * z) + 0.5


def _gelu_tanh(z):
    hz = 0.5 * z
    return hz + hz * jnp.tanh(z * (GELU_C0 + GELU_C1 * (z * z)))


def _dot(a, b):
    return jnp.dot(a, b, preferred_element_type=F32)


def _rms_modulate(x, g, scale, shift):
    ms = jnp.mean(x * x, axis=-1, keepdims=True)
    h = x * lax.rsqrt(ms + EPS) * g
    return h * (1.0 + scale) + shift


def _lane_tile(col, n):
    return jnp.concatenate([col] * n, axis=1)


def _ada_kernel(c_ref, w_ref, b_ref, o_ref):
    c = c_ref[...]
    o_ref[...] = jnp.dot(c * _sigmoid(c), w_ref[...], preferred_element_type=F32) + b_ref[...]


def _ada_call(c, w_ada, b_ada):
    batch = c.shape[0]
    return pl.pallas_call(
        _ada_kernel,
        grid=(N_MOD,),
        in_specs=[
            pl.BlockSpec((batch, D_MODEL), lambda n: (0, 0)),
            pl.BlockSpec((D_MODEL, D_MODEL), lambda n: (0, n)),
            pl.BlockSpec((1, D_MODEL), lambda n: (0, n)),
        ],
        out_specs=pl.BlockSpec((batch, D_MODEL), lambda n: (0, n)),
        out_shape=jax.ShapeDtypeStruct((batch, N_MOD * D_MODEL), F32),
        name="ada_mod",
    )(c, w_ada, b_ada)


def _mixer_kernel(x_ref, mod_ref, rows_ref, win_ref, bt_ref,
                  wst_ref, wa_ref, wb_ref, wo_ref, wup_ref, wdn_ref,
                  o_ref, wupb_ref, wdnb_ref,
                  wt_ref, hb_ref, vn_ref, ya_ref, qt_ref, kb_ref, vt_ref, ybt_ref, mg_ref, sgb_ref):
    t = pl.program_id(1)
    n_blk = TS_MIX // BLK

    @pl.when(t == 0)
    def _():
        kb_ref[0:BLK, :] = jnp.zeros((BLK, KV_WIDTH), BF16)
        r = lax.broadcasted_iota(jnp.int32, vt_ref.shape, 0)
        ones_row = r == HEAD_DIM
        for g in range(1, N_KV_HEADS):
            ones_row = ones_row | (r == g * VT_ROWS + HEAD_DIM)
        vt_ref[...] = jnp.where(ones_row, 1.0, 0.0).astype(BF16)

    @pl.when((pl.program_id(0) == 0) & (t == 0))
    def _():
        for lo in range(0, D_MODEL, BLK * 4):
            wt_ref[T_Q + lo:T_Q + lo + BLK * 4, :] = (
                win_ref[:, O_Q + lo:O_Q + lo + BLK * 4] * (HEAD_DIM ** -0.5)).astype(BF16).T
        wt_ref[T_VA:T_ROWS, :] = win_ref[:, O_VA:O_GA].T

    wupb_ref[...] = wup_ref[...].astype(BF16)
    wdnb_ref[...] = wdn_ref[...].astype(BF16)

    x = x_ref[...]
    h = _rms_modulate(x, rows_ref[ROW_GMIX:ROW_GMIX + 1, :D_MODEL], mod_ref[1:2, :], mod_ref[0:1, :])
    hb_ref[...] = h.astype(BF16)

    def proj_t(lo, hi):
        z = lax.dot_general(wt_ref[lo:hi, :], hb_ref[...], NT_DIMS, preferred_element_type=F32)
        return z + _lane_tile(bt_ref[lo:hi, :], n_blk)

    def proj_n(lo, hi):
        z = jnp.dot(hb_ref[...], win_ref[:, lo:hi], preferred_element_type=F32)
        return z + rows_ref[ROW_BIN:ROW_BIN + 1, lo:hi]

    half = D_MODEL // 2
    zv0 = proj_n(O_V, O_V + half)
    zv1 = proj_n(O_V + half, O_Q)
    zu0 = proj_n(O_U, O_U + half)
    zu1 = proj_n(O_U + half, O_V)
    gv0 = _gelu_tanh(zv0)
    zq0 = proj_t(T_Q, T_Q + half)
    gv1 = _gelu_tanh(zv1)
    zq1 = proj_t(T_Q + half, T_VA)
    gv = jnp.concatenate([gv0, gv1], axis=1)
    mu = jnp.mean(gv, axis=-1, keepdims=True)
    gc = gv - mu
    var = jnp.mean(gc * gc, axis=-1, keepdims=True)
    vn = (gc * lax.rsqrt(var + EPS) * rows_ref[ROW_LNG:ROW_LNG + 1, :D_MODEL]
          + rows_ref[ROW_LNB:ROW_LNB + 1, :D_MODEL])
    vn_ref[...] = vn.astype(BF16)
    vat = proj_t(T_VA, T_ROWS)
    kb_ref[BLK:, :] = proj_n(O_K, O_VA).astype(BF16)
    mg_ref[:, :half] = _sigmoid(proj_n(O_GA, O_GA + half))
    mg_ref[:, half:] = _sigmoid(proj_n(O_GA + half, O_GB))
    sgb_ref[...] = _sigmoid(proj_n(O_GB, IN_WIDTH)).astype(BF16)
    qt_ref[:half, :] = zq0.astype(BF16)
    qt_ref[half:, :] = zq1.astype(BF16)
    for g in range(N_KV_HEADS):
        vt_ref[g * VT_ROWS:g * VT_ROWS + HEAD_DIM, BLK:] = (
            vat[g * HEAD_DIM:(g + 1) * HEAD_DIM].astype(BF16))

    src = lax.broadcasted_iota(jnp.int32, (BLK, BLK), 0)
    dst = lax.broadcasted_iota(jnp.int32, (BLK, BLK), 1)
    causal_t = src <= dst
    zero_blk = jnp.zeros((BLK, BLK), BF16)
    gatings = []
    for pg in range(GM_GROUPS // 2):
        g0, g1 = 2 * pg, 2 * pg + 1
        r0 = slice(g0 * GM_GROUP_DIM, (g0 + 1) * GM_GROUP_DIM)
        r1 = slice(g1 * GM_GROUP_DIM, (g1 + 1) * GM_GROUP_DIM)
        w0 = jnp.where(causal_t, wst_ref[g0], 0.0).astype(BF16)
        w1 = jnp.where(causal_t, wst_ref[g1], 0.0).astype(BF16)
        bd = jnp.concatenate([jnp.concatenate([w0, zero_blk], axis=1),
                              jnp.concatenate([zero_blk, w1], axis=1)], axis=0)
        lhs = jnp.concatenate(
            [jnp.concatenate([vn_ref[j * BLK:(j + 1) * BLK, r0].T, vn_ref[j * BLK:(j + 1) * BLK, r1].T],
                             axis=1) for j in range(n_blk)], axis=0)
        gatings.append(
            jnp.dot(lhs, bd, preferred_element_type=F32) + rows_ref[ROW_BS:ROW_BS + 1, g0 * BLK:(g1 + 1) * BLK])

    zu = (zu0, zu1)

    def gated_u(half_idx):
        for pg, sg in enumerate(gatings):
            for e in range(2):
                g = 2 * pg + e
                cols = slice(g * GM_GROUP_DIM, (g + 1) * GM_GROUP_DIM)
                zpart = zu[g * GM_GROUP_DIM // half]
                zcols = slice(g * GM_GROUP_DIM % half, g * GM_GROUP_DIM % half + GM_GROUP_DIM)
                for j in range(half_idx * (n_blk // 2), (half_idx + 1) * (n_blk // 2)):
                    cs = slice(j * BLK, (j + 1) * BLK)
                    ya_ref[cs, cols] = (
                        _gelu_tanh(zpart[cs, zcols]) * sg[cs, e * BLK:(e + 1) * BLK].T).astype(BF16)

    width = HEADS_PER_ITEM * BLK
    key = lax.broadcasted_iota(jnp.int32, (BLK, width), 0)
    qry = lax.broadcasted_iota(jnp.int32, (BLK, width), 1) & (BLK - 1)
    cur = key <= qry
    no_prev = jnp.where(t == 0, -jnp.inf, 0.0).astype(F32)
    zero_half = jnp.zeros((HEAD_DIM, width), BF16)

    items = [(j, g, p) for j in range(n_blk) for g in range(N_KV_HEADS)
             for p in range(Q_REP // HEADS_PER_ITEM)]

    def scores(item):
        j, g, p = item
        h0 = g * Q_REP + p * HEADS_PER_ITEM
        cs = slice(j * BLK, (j + 1) * BLK)
        kband = kb_ref[j * BLK:(j + 2) * BLK, (g // 2) * LANES:(g // 2 + 1) * LANES]
        qg = jnp.concatenate(
            [qt_ref[(h0 + e) * HEAD_DIM:(h0 + e + 1) * HEAD_DIM, cs] for e in range(HEADS_PER_ITEM)],
            axis=1)
        rhs = jnp.concatenate([qg, zero_half] if g % 2 == 0 else [zero_half, qg], axis=0)
        return jnp.dot(kband, rhs, preferred_element_type=F32)

    def softmax(item, st):
        j, g, p = item
        h0 = g * Q_REP + p * HEADS_PER_ITEM
        s_prev = st[:BLK]
        if j == 0:
            s_prev = s_prev + no_prev
        live = jnp.where(cur, st[BLK:], s_prev)
        sink = rows_ref[ROW_SINK:ROW_SINK + 1, h0 * BLK:h0 * BLK + width]
        m = jnp.maximum(jnp.max(live, axis=0, keepdims=True), sink)
        pr = jnp.exp(live - m)
        p_sink = jnp.exp(sink - m)
        pcat = jnp.concatenate([jnp.where(cur, 0.0, pr), jnp.where(cur, pr, 0.0)], axis=0)
        return pcat.astype(BF16), p_sink

    def attend(item, pcat, p_sink):
        j, g, p = item
        h0 = g * Q_REP + p * HEADS_PER_ITEM
        ot = jnp.dot(vt_ref[g * VT_ROWS:(g + 1) * VT_ROWS, j * BLK:(j + 2) * BLK], pcat,
                     preferred_element_type=F32)
        inv = 1.0 / (ot[HEAD_DIM:HEAD_DIM + 1] + p_sink)
        o = (ot[:HEAD_DIM] * inv).astype(BF16)
        for e in range(HEADS_PER_ITEM):
            ybt_ref[(h0 + e) * HEAD_DIM:(h0 + e + 1) * HEAD_DIM, j * BLK:(j + 1) * BLK] = (
                o[:, e * BLK:(e + 1) * BLK])

    def gate_a(half_idx):
        rs = slice(half_idx * (TS_MIX // 2), (half_idx + 1) * (TS_MIX // 2))
        gated_u(half_idx)
        acc = jnp.dot(ya_ref[rs, :], wa_ref[...], preferred_element_type=F32)
        mg_ref[rs, :] = mg_ref[rs, :] * acc

    n_items = len(items)
    fill = {n_items // 4 + ATT_LAG: 0, (3 * n_items) // 4 + ATT_LAG: 1}
    st, pc = {}, {}
    for step in range(n_items + 2 * ATT_LAG):
        if step < n_items:
            st[step] = scores(items[step])
        if step in fill:
            gate_a(fill[step])
        i = step - ATT_LAG
        if 0 <= i < n_items:
            pc[i] = softmax(items[i], st.pop(i))
        i = step - 2 * ATT_LAG
        if 0 <= i < n_items:
            attend(items[i], *pc.pop(i))

    kb_ref[0:BLK, :] = kb_ref[TS_MIX:TS_MIX + BLK, :]
    vt_ref[:, 0:BLK] = vt_ref[:, TS_MIX:TS_MIX + BLK]

    acc_b = lax.dot_general(ybt_ref[...], wb_ref[...], TN_DIMS, preferred_element_type=F32)
    merged = mg_ref[...] + sgb_ref[...].astype(F32) * acc_b
    y = jnp.dot(merged.astype(BF16), wo_ref[...], preferred_element_type=F32)
    o_ref[...] = x_ref[...] + mod_ref[2:3, :] * y


def _resident(shape):
    nd = len(shape)
    return pl.BlockSpec(shape, lambda *_: (0,) * nd, pipeline_mode=pl.Buffered(1))


def _mixer_call(x, mod, rows, w_in, b_t, ws_t, wa, wb, wo, w_up, w_down):
    batch, seq, _ = x.shape
    n_tiles = seq // TS_MIX
    n_steps = batch * n_tiles
    tile = pl.BlockSpec((None, TS_MIX, D_MODEL), lambda b, t: (b, t, 0))
    up_rows = pl.BlockSpec((D_MODEL // n_steps, D_FF), lambda b, t: (b * n_tiles + t, 0))
    dn_rows = pl.BlockSpec((D_FF // n_steps, D_MODEL), lambda b, t: (b * n_tiles + t, 0))
    return pl.pallas_call(
        _mixer_kernel,
        grid=(batch, n_tiles),
        in_specs=[
            tile,
            pl.BlockSpec((None, N_MOD, D_MODEL), lambda b, t: (b, 0, 0)),
            _resident((ROWS_SUBLANES, ROWS_WIDTH)),
            _resident((D_MODEL, IN_WIDTH)),
            _resident((T_ROWS, LANES)),
            _resident((GM_GROUPS, BLK, BLK)),
            _resident((D_MODEL, D_MODEL)),
            _resident((D_MODEL, D_MODEL)),
            _resident((D_MODEL, D_MODEL)),
            up_rows,
            dn_rows,
        ],
        out_specs=[tile, up_rows, dn_rows],
        out_shape=[jax.ShapeDtypeStruct(x.shape, F32),
                   jax.ShapeDtypeStruct(w_up.shape, BF16),
                   jax.ShapeDtypeStruct(w_down.shape, BF16)],
        scratch_shapes=[
            pltpu.VMEM((T_ROWS, D_MODEL), BF16),
            pltpu.VMEM((TS_MIX, D_MODEL), BF16),
            pltpu.VMEM((TS_MIX, D_MODEL), BF16),
            pltpu.VMEM((TS_MIX, D_MODEL), BF16),
            pltpu.VMEM((D_MODEL, TS_MIX), BF16),
            pltpu.VMEM((BLK + TS_MIX, KV_WIDTH), BF16),
            pltpu.VMEM((N_KV_HEADS * VT_ROWS, BLK + TS_MIX), BF16),
            pltpu.VMEM((D_MODEL, TS_MIX), BF16),
            pltpu.VMEM((TS_MIX, D_MODEL), F32),
            pltpu.VMEM((TS_MIX, D_MODEL), BF16),
        ],
        compiler_params=pltpu.CompilerParams(
            dimension_semantics=("arbitrary", "arbitrary"),
            vmem_limit_bytes=VMEM_LIMIT_BYTES),
        name="token_mixer",
    )(x, mod, rows, w_in, b_t, ws_t, wa, wb, wo, w_up, w_down)


def _mlp_kernel(x_ref, mod_ref, gmlp_ref, wup_ref, wdn_ref, gfin_ref, o_ref, hb_ref, act_ref):
    x = x_ref[...]
    h = _rms_modulate(x, gmlp_ref[...], mod_ref[4:5, :], mod_ref[3:4, :])
    hb_ref[...] = h.astype(BF16)
    for c in range(D_FF // D_MODEL):
        cs = slice(c * D_MODEL, (c + 1) * D_MODEL)
        a = jnp.maximum(_dot(hb_ref[...], wup_ref[:, cs]), 0.0)
        act_ref[:, cs] = (a * a).astype(BF16)
    x2 = x_ref[...] + mod_ref[5:6, :] * _dot(act_ref[...], wdn_ref[...])
    ms = jnp.mean(x2 * x2, axis=-1, keepdims=True)
    o_ref[...] = x2 * lax.rsqrt(ms + EPS) * gfin_ref[...]


def _mlp_call(x, mod, g_mlp, w_up, w_down, g_final):
    batch, seq, _ = x.shape
    tile = pl.BlockSpec((None, TM_MLP, D_MODEL), lambda b, t: (b, t, 0))
    return pl.pallas_call(
        _mlp_kernel,
        grid=(batch, seq // TM_MLP),
        in_specs=[
            tile,
            pl.BlockSpec((None, N_MOD, D_MODEL), lambda b, t: (b, 0, 0)),
            _resident((1, D_MODEL)),
            _resident((D_MODEL, D_FF)),
            _resident((D_FF, D_MODEL)),
            _resident((1, D_MODEL)),
        ],
        out_specs=tile,
        out_shape=jax.ShapeDtypeStruct(x.shape, F32),
        scratch_shapes=[
            pltpu.VMEM((TM_MLP, D_MODEL), BF16),
            pltpu.VMEM((TM_MLP, D_FF), BF16),
        ],
        compiler_params=pltpu.CompilerParams(
            dimension_semantics=("arbitrary", "arbitrary"),
            vmem_limit_bytes=VMEM_LIMIT_BYTES),
        name="channel_mlp",
    )(x, mod, g_mlp, w_up, w_down, g_final)


def kernel(x, c, w_ada, b_ada, g_norm_mix, w_in, b_in, gm_ln_g, gm_ln_b, gm_ws, gm_bs, attn_sinks,
           w_branch_a, w_branch_b, w_out, g_norm_mlp, w_up, w_down, g_final):
    batch = x.shape[0]
    depth = w_in.shape[0]
    row = lambda v: v.reshape(1, -1)
    lane_col = lambda v: jnp.broadcast_to(v[:, None], (v.shape[0], LANES))
    for l in range(depth):
        mod = _ada_call(c, w_ada[l], row(b_ada[l])).reshape(batch, N_MOD, D_MODEL)
        bl = b_in[l]
        b_t = lane_col(jnp.concatenate([bl[O_Q:O_K] * (HEAD_DIM ** -0.5), bl[O_VA:O_GA]]))
        pad = lambda v: jnp.pad(v.reshape(-1), (0, ROWS_WIDTH - v.size))
        rows = jnp.zeros((ROWS_SUBLANES, ROWS_WIDTH), F32)
        rows = rows.at[ROW_GMIX].set(pad(g_norm_mix[l])).at[ROW_BIN].set(pad(bl))
        rows = rows.at[ROW_BS].set(pad(gm_bs[l])).at[ROW_SINK].set(pad(jnp.repeat(attn_sinks[l], BLK)))
        rows = rows.at[ROW_LNG].set(pad(gm_ln_g[l])).at[ROW_LNB].set(pad(gm_ln_b[l]))
        x, w_up_b, w_down_b = _mixer_call(
            x, mod, rows, w_in[l].astype(BF16), b_t,
            jnp.swapaxes(gm_ws[l], 1, 2),
            w_branch_a[l].astype(BF16), w_branch_b[l].astype(BF16), w_out[l].astype(BF16),
            w_up[l], w_down[l])
        assert depth == 1
        x = _mlp_call(x, mod, row(g_norm_mlp[l]), w_up_b, w_down_b, row(g_final))
    return x
```

```python
import jax
import jax.numpy as jnp
from jax import lax
from jax.experimental import pallas as pl
from jax.experimental.pallas import tpu as pltpu

D_MODEL = 1024
BLK = 128
GM_GROUPS = 8
GM_GROUP_DIM = D_MODEL // GM_GROUPS
N_Q_HEADS = 16
N_KV_HEADS = 4
HEAD_DIM = 64
Q_REP = N_Q_HEADS // N_KV_HEADS
KV_WIDTH = N_KV_HEADS * HEAD_DIM
D_FF = 4 * D_MODEL
N_MOD = 6
EPS = 1e-6

O_U = 0
O_V = O_U + D_MODEL
O_Q = O_V + D_MODEL
O_K = O_Q + D_MODEL
O_VA = O_K + KV_WIDTH
O_GA = O_VA + KV_WIDTH
O_GB = O_GA + D_MODEL
IN_WIDTH = O_GB + D_MODEL

T_Q = 0
T_VA = T_Q + D_MODEL
T_ROWS = T_VA + KV_WIDTH

LANES = 128
BF16_ROWS = 16
HEADS_PER_ITEM = 2
VT_ROWS = HEAD_DIM + BF16_ROWS

ROW_GMIX, ROW_BIN, ROW_BS, ROW_SINK, ROW_LNG, ROW_LNB = 0, 1, 2, 3, 4, 5
ROWS_SUBLANES = 8
ROWS_WIDTH = 8192
ATT_LAG = 5
TS_MIX = 512
TM_MLP = 1024
VMEM_LIMIT_BYTES = 56 * 1024 * 1024

F32 = jnp.float32
BF16 = jnp.bfloat16
NT_DIMS = (((1,), (1,)), ((), ()))
TN_DIMS = (((0,), (0,)), ((), ()))

GELU_C0 = 0.7978845608028654
GELU_C1 = GELU_C0 * 0.044715


def _sigmoid(z):
    return 0.5 * jnp.tanh(0.5 * z) + 0.5


def _gelu_tanh(z):
    hz = 0.5 * z
    return hz + hz * jnp.tanh(z * (GELU_C0 + GELU_C1 * (z * z)))


def _dot(a, b):
    return jnp.dot(a, b, preferred_element_type=F32)


def _rms_modulate(x, g, scale, shift):
    ms = jnp.mean(x * x, axis=-1, keepdims=True)
    h = x * lax.rsqrt(ms + EPS) * g
    return h * (1.0 + scale) + shift


def _lane_tile(col, n):
    return jnp.concatenate([col] * n, axis=1)


def _ada_kernel(c_ref, w_ref, b_ref, o_ref):
    c = c_ref[...]
    o_ref[...] = jnp.dot(c * _sigmoid(c), w_ref[...], preferred_element_type=F32) + b_ref[...]


def _ada_call(c, w_ada, b_ada):
    batch = c.shape[0]
    return pl.pallas_call(
        _ada_kernel,
        grid=(N_MOD,),
        in_specs=[
            pl.BlockSpec((batch, D_MODEL), lambda n: (0, 0)),
            pl.BlockSpec((D_MODEL, D_MODEL), lambda n: (0, n)),
            pl.BlockSpec((1, D_MODEL), lambda n: (0, n)),
        ],
        out_specs=pl.BlockSpec((batch, D_MODEL), lambda n: (0, n)),
        out_shape=jax.ShapeDtypeStruct((batch, N_MOD * D_MODEL), F32),
        name="ada_mod",
    )(c, w_ada, b_ada)


def _mixer_kernel(x_ref, mod_ref, rows_ref, win_ref, bt_ref,
                  wst_ref, wa_ref, wb_ref, wo_ref, wup_ref, wdn_ref,
                  o_ref, wupb_ref, wdnb_ref,
                  wt_ref, hb_ref, vn_ref, ya_ref, qt_ref, kb_ref, vt_ref, ybt_ref, mg_ref, sgb_ref):
    t = pl.program_id(1)
    n_blk = TS_MIX // BLK

    @pl.when(t == 0)
    def _():
        kb_ref[0:BLK, :] = jnp.zeros((BLK, KV_WIDTH), BF16)
        r = lax.broadcasted_iota(jnp.int32, vt_ref.shape, 0)
        ones_row = r == HEAD_DIM
        for g in range(1, N_KV_HEADS):
            ones_row = ones_row | (r == g * VT_ROWS + HEAD_DIM)
        vt_ref[...] = jnp.where(ones_row, 1.0, 0.0).astype(BF16)

    @pl.when((pl.program_id(0) == 0) & (t == 0))
    def _():
        for lo in range(0, D_MODEL, BLK * 4):
            wt_ref[T_Q + lo:T_Q + lo + BLK * 4, :] = (
                win_ref[:, O_Q + lo:O_Q + lo + BLK * 4] * (HEAD_DIM ** -0.5)).astype(BF16).T
        wt_ref[T_VA:T_ROWS, :] = win_ref[:, O_VA:O_GA].T

    wupb_ref[...] = wup_ref[...].astype(BF16)
    wdnb_ref[...] = wdn_ref[...].astype(BF16)

    x = x_ref[...]
    h = _rms_modulate(x, rows_ref[ROW_GMIX:ROW_GMIX + 1, :D_MODEL], mod_ref[1:2, :], mod_ref[0:1, :])
    hb_ref[...] = h.astype(BF16)

    def proj_t(lo, hi):
        z = lax.dot_general(wt_ref[lo:hi, :], hb_ref[...], NT_DIMS, preferred_element_type=F32)
        return z + _lane_tile(bt_ref[lo:hi, :], n_blk)

    def proj_n(lo, hi):
        z = jnp.dot(hb_ref[...], win_ref[:, lo:hi], preferred_element_type=F32)
        return z + rows_ref[ROW_BIN:ROW_BIN + 1, lo:hi]

    half = D_MODEL // 2
    zv0 = proj_n(O_V, O_V + half)
    zv1 = proj_n(O_V + half, O_Q)
    zu0 = proj_n(O_U, O_U + half)
    zu1 = proj_n(O_U + half, O_V)
    gv0 = _gelu_tanh(zv0)
    zq0 = proj_t(T_Q, T_Q + half)
    gv1 = _gelu_tanh(zv1)
    zq1 = proj_t(T_Q + half, T_VA)
    gv = jnp.concatenate([gv0, gv1], axis=1)
    mu = jnp.mean(gv, axis=-1, keepdims=True)
    gc = gv - mu
    var = jnp.mean(gc * gc, axis=-1, keepdims=True)
    vn = (gc * lax.rsqrt(var + EPS) * rows_ref[ROW_LNG:ROW_LNG + 1, :D_MODEL]
          + rows_ref[ROW_LNB:ROW_LNB + 1, :D_MODEL])
    vn_ref[...] = vn.astype(BF16)
    vat = proj_t(T_VA, T_ROWS)
    kb_ref[BLK:, :] = proj_n(O_K, O_VA).astype(BF16)
    mg_ref[:, :half] = _sigmoid(proj_n(O_GA, O_GA + half))
    mg_ref[:, half:] = _sigmoid(proj_n(O_GA + half, O_GB))
    sgb_ref[...] = _sigmoid(proj_n(O_GB, IN_WIDTH)).astype(BF16)
    qt_ref[:half, :] = zq0.astype(BF16)
    qt_ref[half:, :] = zq1.astype(BF16)
    for g in range(N_KV_HEADS):
        vt_ref[g * VT_ROWS:g * VT_ROWS + HEAD_DIM, BLK:] = (
            vat[g * HEAD_DIM:(g + 1) * HEAD_DIM].astype(BF16))

    src = lax.broadcasted_iota(jnp.int32, (BLK, BLK), 0)
    dst = lax.broadcasted_iota(jnp.int32, (BLK, BLK), 1)
    causal_t = src <= dst
    zero_blk = jnp.zeros((BLK, BLK), BF16)
    gatings = []
    for pg in range(GM_GROUPS // 2):
        g0, g1 = 2 * pg, 2 * pg + 1
        r0 = slice(g0 * GM_GROUP_DIM, (g0 + 1) * GM_GROUP_DIM)
        r1 = slice(g1 * GM_GROUP_DIM, (g1 + 1) * GM_GROUP_DIM)
        w0 = jnp.where(causal_t, wst_ref[g0], 0.0).astype(BF16)
        w1 = jnp.where(causal_t, wst_ref[g1], 0.0).astype(BF16)
        bd = jnp.concatenate([jnp.concatenate([w0, zero_blk], axis=1),
                              jnp.concatenate([zero_blk, w1], axis=1)], axis=0)
        lhs = jnp.concatenate(
            [jnp.concatenate([vn_ref[j * BLK:(j + 1) * BLK, r0].T, vn_ref[j * BLK:(j + 1) * BLK, r1].T],
                             axis=1) for j in range(n_blk)], axis=0)
        gatings.append(
            jnp.dot(lhs, bd, preferred_element_type=F32) + rows_ref[ROW_BS:ROW_BS + 1, g0 * BLK:(g1 + 1) * BLK])

    zu = (zu0, zu1)

    def gated_u(half_idx):
        for pg, sg in enumerate(gatings):
            for e in range(2):
                g = 2 * pg + e
                cols = slice(g * GM_GROUP_DIM, (g + 1) * GM_GROUP_DIM)
                zpart = zu[g * GM_GROUP_DIM // half]
                zcols = slice(g * GM_GROUP_DIM % half, g * GM_GROUP_DIM % half + GM_GROUP_DIM)
                for j in range(half_idx * (n_blk // 2), (half_idx + 1) * (n_blk // 2)):
                    cs = slice(j * BLK, (j + 1) * BLK)
                    ya_ref[cs, cols] = (
                        _gelu_tanh(zpart[cs, zcols]) * sg[cs, e * BLK:(e + 1) * BLK].T).astype(BF16)

    width = HEADS_PER_ITEM * BLK
    key = lax.broadcasted_iota(jnp.int32, (BLK, width), 0)
    qry = lax.broadcasted_iota(jnp.int32, (BLK, width), 1) & (BLK - 1)
    cur = key <= qry
    no_prev = jnp.where(t == 0, -jnp.inf, 0.0).astype(F32)
    zero_half = jnp.zeros((HEAD_DIM, width), BF16)

    items = [(j, g, p) for j in range(n_blk) for g in range(N_KV_HEADS)
             for p in range(Q_REP // HEADS_PER_ITEM)]

    def scores(item):
        j, g, p = item
        h0 = g * Q_REP + p * HEADS_PER_ITEM
        cs = slice(j * BLK, (j + 1) * BLK)
        kband = kb_ref[j * BLK:(j + 2) * BLK, (g // 2) * LANES:(g // 2 + 1) * LANES]
        qg = jnp.concatenate(
            [qt_ref[(h0 + e) * HEAD_DIM:(h0 + e + 1) * HEAD_DIM, cs] for e in range(HEADS_PER_ITEM)],
            axis=1)
        rhs = jnp.concatenate([qg, zero_half] if g % 2 == 0 else [zero_half, qg], axis=0)
        return jnp.dot(kband, rhs, preferred_element_type=F32)

    def softmax(item, st):
        j, g, p = item
        h0 = g * Q_REP + p * HEADS_PER_ITEM
        s_prev = st[:BLK]
        if j == 0:
            s_prev = s_prev + no_prev
        live = jnp.where(cur, st[BLK:], s_prev)
        sink = rows_ref[ROW_SINK:ROW_SINK + 1, h0 * BLK:h0 * BLK + width]
        m = jnp.maximum(jnp.max(live, axis=0, keepdims=True), sink)
        pr = jnp.exp(live - m)
        p_sink = jnp.exp(sink - m)
        pcat = jnp.concatenate([jnp.where(cur, 0.0, pr), jnp.where(cur, pr, 0.0)], axis=0)
        return pcat.astype(BF16), p_sink

    def attend(item, pcat, p_sink):
        j, g, p = item
        h0 = g * Q_REP + p * HEADS_PER_ITEM
        ot = jnp.dot(vt_ref[g * VT_ROWS:(g + 1) * VT_ROWS, j * BLK:(j + 2) * BLK], pcat,
                     preferred_element_type=F32)
        inv = 1.0 / (ot[HEAD_DIM:HEAD_DIM + 1] + p_sink)
        o = (ot[:HEAD_DIM] * inv).astype(BF16)
        for e in range(HEADS_PER_ITEM):
            ybt_ref[(h0 + e) * HEAD_DIM:(h0 + e + 1) * HEAD_DIM, j * BLK:(j + 1) * BLK] = (
                o[:, e * BLK:(e + 1) * BLK])

    def gate_a(half_idx):
        rs = slice(half_idx * (TS_MIX // 2), (half_idx + 1) * (TS_MIX // 2))
        gated_u(half_idx)
        acc = jnp.dot(ya_ref[rs, :], wa_ref[...], preferred_element_type=F32)
        mg_ref[rs, :] = mg_ref[rs, :] * acc

    n_items = len(items)
    fill = {n_items // 4 + ATT_LAG: 0, (3 * n_items) // 4 + ATT_LAG: 1}
    st, pc = {}, {}
    for step in range(n_items + 2 * ATT_LAG):
        if step < n_items:
            st[step] = scores(items[step])
        if step in fill:
            gate_a(fill[step])
        i = step - ATT_LAG
        if 0 <= i < n_items:
            pc[i] = softmax(items[i], st.pop(i))
        i = step - 2 * ATT_LAG
        if 0 <= i < n_items:
            attend(items[i], *pc.pop(i))

    kb_ref[0:BLK, :] = kb_ref[TS_MIX:TS_MIX + BLK, :]
    vt_ref[:, 0:BLK] = vt_ref[:, TS_MIX:TS_MIX + BLK]

    acc_b = lax.dot_general(ybt_ref[...], wb_ref[...], TN_DIMS, preferred_element_type=F32)
    merged = mg_ref[...] + sgb_ref[...].astype(F32) * acc_b
    y = jnp.dot(merged.astype(BF16), wo_ref[...], preferred_element_type=F32)
    o_ref[...] = x_ref[...] + mod_ref[2:3, :] * y


def _resident(shape):
    nd = len(shape)
    return pl.BlockSpec(shape, lambda *_: (0,) * nd, pipeline_mode=pl.Buffered(1))


def _mixer_call(x, mod, rows, w_in, b_t, ws_t, wa, wb, wo, w_up, w_down):
    batch, seq, _ = x.shape
    n_tiles = seq // TS_MIX
    n_steps = batch * n_tiles
    tile = pl.BlockSpec((None, TS_MIX, D_MODEL), lambda b, t: (b, t, 0))
    up_rows = pl.BlockSpec((D_MODEL // n_steps, D_FF), lambda b, t: (b * n_tiles + t, 0))
    dn_rows = pl.BlockSpec((D_FF // n_steps, D_MODEL), lambda b, t: (b * n_tiles + t, 0))
    return pl.pallas_call(
        _mixer_kernel,
        grid=(batch, n_tiles),
        in_specs=[
            tile,
            pl.BlockSpec((None, N_MOD, D_MODEL), lambda b, t: (b, 0, 0)),
            _resident((ROWS_SUBLANES, ROWS_WIDTH)),
            _resident((D_MODEL, IN_WIDTH)),
            _resident((T_ROWS, LANES)),
            _resident((GM_GROUPS, BLK, BLK)),
            _resident((D_MODEL, D_MODEL)),
            _resident((D_MODEL, D_MODEL)),
            _resident((D_MODEL, D_MODEL)),
            up_rows,
            dn_rows,
        ],
        out_specs=[tile, up_rows, dn_rows],
        out_shape=[jax.ShapeDtypeStruct(x.shape, F32),
                   jax.ShapeDtypeStruct(w_up.shape, BF16),
                   jax.ShapeDtypeStruct(w_down.shape, BF16)],
        scratch_shapes=[
            pltpu.VMEM((T_ROWS, D_MODEL), BF16),
            pltpu.VMEM((TS_MIX, D_MODEL), BF16),
            pltpu.VMEM((TS_MIX, D_MODEL), BF16),
            pltpu.VMEM((TS_MIX, D_MODEL), BF16),
            pltpu.VMEM((D_MODEL, TS_MIX), BF16),
            pltpu.VMEM((BLK + TS_MIX, KV_WIDTH), BF16),
            pltpu.VMEM((N_KV_HEADS * VT_ROWS, BLK + TS_MIX), BF16),
            pltpu.VMEM((D_MODEL, TS_MIX), BF16),
            pltpu.VMEM((TS_MIX, D_MODEL), F32),
            pltpu.VMEM((TS_MIX, D_MODEL), BF16),
        ],
        compiler_params=pltpu.CompilerParams(
            dimension_semantics=("arbitrary", "arbitrary"),
            vmem_limit_bytes=VMEM_LIMIT_BYTES),
        name="token_mixer",
    )(x, mod, rows, w_in, b_t, ws_t, wa, wb, wo, w_up, w_down)


def _mlp_kernel(x_ref, mod_ref, gmlp_ref, wup_ref, wdn_ref, gfin_ref, o_ref, hb_ref, act_ref):
    x = x_ref[...]
    h = _rms_modulate(x, gmlp_ref[...], mod_ref[4:5, :], mod_ref[3:4, :])
    hb_ref[...] = h.astype(BF16)
    for c in range(D_FF // D_MODEL):
        cs = slice(c * D_MODEL, (c + 1) * D_MODEL)
        a = jnp.maximum(_dot(hb_ref[...], wup_ref[:, cs]), 0.0)
        act_ref[:, cs] = (a * a).astype(BF16)
    x2 = x_ref[...] + mod_ref[5:6, :] * _dot(act_ref[...], wdn_ref[...])
    ms = jnp.mean(x2 * x2, axis=-1, keepdims=True)
    o_ref[...] = x2 * lax.rsqrt(ms + EPS) * gfin_ref[...]


def _mlp_call(x, mod, g_mlp, w_up, w_down, g_final):
    batch, seq, _ = x.shape
    tile = pl.BlockSpec((None, TM_MLP, D_MODEL), lambda b, t: (b, t, 0))
    return pl.pallas_call(
        _mlp_kernel,
        grid=(batch, seq // TM_MLP),
        in_specs=[
            tile,
            pl.BlockSpec((None, N_MOD, D_MODEL), lambda b, t: (b, 0, 0)),
            _resident((1, D_MODEL)),
            _resident((D_MODEL, D_FF)),
            _resident((D_FF, D_MODEL)),
            _resident((1, D_MODEL)),
        ],
        out_specs=tile,
        out_shape=jax.ShapeDtypeStruct(x.shape, F32),
        scratch_shapes=[
            pltpu.VMEM((TM_MLP, D_MODEL), BF16),
            pltpu.VMEM((TM_MLP, D_FF), BF16),
        ],
        compiler_params=pltpu.CompilerParams(
            dimension_semantics=("arbitrary", "arbitrary"),
            vmem_limit_bytes=VMEM_LIMIT_BYTES),
        name="channel_mlp",
    )(x, mod, g_mlp, w_up, w_down, g_final)


def kernel(x, c, w_ada, b_ada, g_norm_mix, w_in, b_in, gm_ln_g, gm_ln_b, gm_ws, gm_bs, attn_sinks,
           w_branch_a, w_branch_b, w_out, g_norm_mlp, w_up, w_down, g_final):
    batch = x.shape[0]
    depth = w_in.shape[0]
    row = lambda v: v.reshape(1, -1)
    lane_col = lambda v: jnp.broadcast_to(v[:, None], (v.shape[0], LANES))
    for l in range(depth):
        mod = _ada_call(c, w_ada[l], row(b_ada[l])).reshape(batch, N_MOD, D_MODEL)
        bl = b_in[l]
        b_t = lane_col(jnp.concatenate([bl[O_Q:O_K] * (HEAD_DIM ** -0.5), bl[O_VA:O_GA]]))
        pad = lambda v: jnp.pad(v.reshape(-1), (0, ROWS_WIDTH - v.size))
        rows = jnp.zeros((ROWS_SUBLANES, ROWS_WIDTH), F32)
        rows = rows.at[ROW_GMIX].set(pad(g_norm_mix[l])).at[ROW_BIN].set(pad(bl))
        rows = rows.at[ROW_BS].set(pad(gm_bs[l])).at[ROW_SINK].set(pad(jnp.repeat(attn_sinks[l], BLK)))
        rows = rows.at[ROW_LNG].set(pad(gm_ln_g[l])).at[ROW_LNB].set(pad(gm_ln_b[l]))
        x, w_up_b, w_down_b = _mixer_call(
            x, mod, rows, w_in[l].astype(BF16), b_t,
            jnp.swapaxes(gm_ws[l], 1, 2),
            w_branch_a[l].astype(BF16), w_branch_b[l].astype(BF16), w_out[l].astype(BF16),
            w_up[l], w_down[l])
        assert depth == 1
        x = _mlp_call(x, mod, row(g_norm_mlp[l]), w_up_b, w_down_b, row(g_final))
    return x
```
